```python
import jax
import jax.numpy as jnp
from jax import lax
import numpy as np

D_MODEL = 1024
BATCH = 4
SEQ = 4096
DEPTH = 2

CTX_LEN = 256
GRID_W = 64
EPS = 1e-6
N_MOD = 6
CONV_W = 512
CONV_K = 3
N_HEADS = 8
N_KV = 2
GROUP = N_HEADS // N_KV
HEAD_DIM = 64
Q_W = N_HEADS * HEAD_DIM
KV_W = N_KV * HEAD_DIM
WINDOW = 128
BLOCK = 128
ROT_AXIS = HEAD_DIM // 2
ROPE_BASE = 10000.0
NEG_INF = -1e30
RNN_W = 1024
RNN_BLOCKS = 8
RNN_BLK = RNN_W // RNN_BLOCKS
RNN_CONV_K = 4
RG_C = 8.0
N_EXPERTS = 16
EXPERT_FF = 1024
CAP_FACTOR = 2
IN_LAYOUT = (('a_x', CONV_W), ('a_b', CONV_W), ('a_c', CONV_W),
             ('q', Q_W), ('k', KV_W), ('v', KV_W),
             ('r_x', RNN_W), ('r_y', RNN_W),
             ('g_conv', D_MODEL), ('g_attn', D_MODEL), ('g_rnn', D_MODEL))
IN_W = 3 * CONV_W + Q_W + 2 * KV_W + 2 * RNN_W + 3 * D_MODEL

kernel_name = 'hybrid_diffusion_conv_swa_rglru_ecmoe'


def _part(a, name):
    off = 0
    for n, w in IN_LAYOUT:
        if n == name:
            return a[..., off:off + w]
        off += w
    raise KeyError(name)


def rmsnorm(x, g):
    xf = x.astype(jnp.float32)
    y = xf * lax.rsqrt(jnp.mean(xf * xf, axis=-1, keepdims=True) + EPS)
    return y.astype(x.dtype) * g


def depthwise_conv(x, w, pad):
    ch = x.shape[-1]
    return lax.conv_general_dilated(x, w[:, None, :].astype(x.dtype), window_strides=(1,), padding=[pad],
                                    dimension_numbers=('NWC', 'WIO', 'NWC'), feature_group_count=ch)


def rope_tables(n_tokens):
    n_rows = n_tokens // GRID_W
    rows = jnp.repeat(jnp.arange(n_rows, dtype=jnp.float32), GRID_W)
    cols = jnp.tile(jnp.arange(GRID_W, dtype=jnp.float32), n_rows)
    inv = 1.0 / (ROPE_BASE ** (jnp.arange(0, ROT_AXIS, 2, dtype=jnp.float32) / ROT_AXIS))
    ang_r = (rows[:, None] * inv)[:, None, :]
    ang_c = (cols[:, None] * inv)[:, None, :]
    return (jnp.cos(ang_r), jnp.sin(ang_r), jnp.cos(ang_c), jnp.sin(ang_c))


def _rotate_half(x, cos, sin):
    x1, x2 = jnp.split(x, 2, axis=-1)
    return jnp.concatenate([x1 * cos - x2 * sin, x1 * sin + x2 * cos], axis=-1)


def apply_rope_2d(x, rope):
    cos_r, sin_r, cos_c, sin_c = rope
    xf = x.astype(jnp.float32)
    x_row, x_col = jnp.split(xf, 2, axis=-1)
    return jnp.concatenate([_rotate_half(x_row, cos_r, sin_r), _rotate_half(x_col, cos_c, sin_c)], axis=-1).astype(x.dtype)


def latent_window_attention(q, k, v, k_ctx, v_ctx, sink):
    b, n = q.shape[:2]
    nb = n // BLOCK
    qb = q.reshape(b, nb, BLOCK, N_KV, GROUP, HEAD_DIM)
    pad = ((0, 0), (BLOCK, BLOCK), (0, 0), (0, 0))
    kp = jnp.pad(k, pad).reshape(b, nb + 2, BLOCK, N_KV, HEAD_DIM)
    vp = jnp.pad(v, pad).reshape(b, nb + 2, BLOCK, N_KV, HEAD_DIM)
    kw = jnp.concatenate([kp[:, :-2], kp[:, 1:-1], kp[:, 2:]], axis=2)
    vw = jnp.concatenate([vp[:, :-2], vp[:, 1:-1], vp[:, 2:]], axis=2)
    qpos = jnp.arange(nb)[:, None] * BLOCK + jnp.arange(BLOCK)[None, :]
    kpos = jnp.arange(nb)[:, None] * BLOCK - BLOCK + jnp.arange(3 * BLOCK)[None, :]
    valid = ((kpos[:, None, :] >= 0) & (kpos[:, None, :] < n)
             & (jnp.abs(qpos[:, :, None] - kpos[:, None, :]) <= WINDOW))
    scale = HEAD_DIM ** -0.5
    s_win = jnp.einsum('bnqhgd,bnkhd->bnhgqk', qb, kw).astype(jnp.float32) * scale
    s_win = jnp.where(valid[None, :, None, None], s_win, NEG_INF)
    s_ctx = jnp.einsum('bnqhgd,blhd->bnhgql', qb, k_ctx).astype(jnp.float32) * scale
    sink_l = sink.astype(jnp.float32).reshape(1, 1, N_KV, GROUP, 1, 1)
    m = jnp.maximum(jnp.maximum(s_win.max(-1, keepdims=True), s_ctx.max(-1, keepdims=True)), sink_l)
    p_win = jnp.exp(s_win - m)
    p_ctx = jnp.exp(s_ctx - m)
    denom = p_win.sum(-1, keepdims=True) + p_ctx.sum(-1, keepdims=True) + jnp.exp(sink_l - m)
    o = (jnp.einsum('bnhgqk,bnkhd->bnqhgd', (p_win / denom).astype(v.dtype), vw)
         + jnp.einsum('bnhgql,blhd->bnqhgd', (p_ctx / denom).astype(v.dtype), v_ctx))
    return o.reshape(b, n, Q_W)


def context_attention(q, k, v, sink):
    b, n = q.shape[:2]
    qg = q.reshape(b, n, N_KV, GROUP, HEAD_DIM)
    s = jnp.einsum('blhgd,bmhd->bhglm', qg, k).astype(jnp.float32) * HEAD_DIM ** -0.5
    sk = jnp.broadcast_to(sink.astype(jnp.float32).reshape(1, N_KV, GROUP, 1, 1), s.shape[:-1] + (1,))
    p = jax.nn.softmax(jnp.concatenate([s, sk], axis=-1), axis=-1)[..., :-1]
    o = jnp.einsum('bhglm,bmhd->blhgd', p.astype(v.dtype), v)
    return o.reshape(b, n, Q_W)


def short_conv_mixer(u, gate_b, gate_c, conv_w):
    return gate_b * depthwise_conv(gate_c * u, conv_w, (CONV_K // 2, CONV_K // 2))


def rg_lru_coeffs(xt, w_a, b_a, w_x, b_x, lam):
    b, n = xt.shape[:2]
    xb = xt.reshape(b, n, RNN_BLOCKS, RNN_BLK)
    r = jax.nn.sigmoid((jnp.einsum('btnk,nkj->btnj', xb, w_a).reshape(b, n, RNN_W) + b_a).astype(jnp.float32))
    i = jax.nn.sigmoid((jnp.einsum('btnk,nkj->btnj', xb, w_x).reshape(b, n, RNN_W) + b_x).astype(jnp.float32))
    log_a = -RG_C * r * jax.nn.softplus(-lam.astype(jnp.float32))
    a = jnp.exp(log_a)
    bt = jnp.sqrt(-jnp.expm1(2.0 * log_a)) * (i * xt.astype(jnp.float32))
    return a, bt


def _combine(lhs, rhs):
    return (lhs[0] * rhs[0], rhs[0] * lhs[1] + rhs[1])


def linear_scan(a, bt, h0):
    a_acc, b_acc = lax.associative_scan(_combine, (a, bt), axis=1)
    if h0 is None:
        return b_acc
    return b_acc + a_acc * h0[:, None, :]


def rg_lru_bidirectional(u_lat, u_ctx, conv_w, conv_b, w_a, b_a, w_x, b_x, lam, need_ctx):
    h_lat = None
    h_ctx = None
    for d in range(2):
        pad = (RNN_CONV_K - 1, 0) if d == 0 else (0, RNN_CONV_K - 1)
        xl = depthwise_conv(u_lat, conv_w[d], pad) + conv_b[d]
        xc = depthwise_conv(u_ctx, conv_w[d], pad) + conv_b[d]
        a_l, b_l = rg_lru_coeffs(xl, w_a[d], b_a[d], w_x[d], b_x[d], lam[d])
        a_c, b_c = rg_lru_coeffs(xc, w_a[d], b_a[d], w_x[d], b_x[d], lam[d])
        if d == 1:
            a_l, b_l, a_c, b_c = (jnp.flip(t, axis=1) for t in (a_l, b_l, a_c, b_c))
        s_c = linear_scan(a_c, b_c, None)
        s_l = linear_scan(a_l, b_l, s_c[:, -1])
        if d == 1:
            s_c = jnp.flip(s_c, axis=1)
            s_l = jnp.flip(s_l, axis=1)
        h_lat = s_l if h_lat is None else h_lat + s_l
        if need_ctx:
            h_ctx = s_c if h_ctx is None else h_ctx + s_c
    return h_lat.astype(u_lat.dtype), (h_ctx.astype(u_ctx.dtype) if need_ctx else None)


def expert_choice_moe(h, w_router, w_gate, w_up, w_down):
    b, n, _ = h.shape
    cap = CAP_FACTOR * n // N_EXPERTS
    aff = jax.nn.softmax(jnp.einsum('bnd,de->bne', h, w_router).astype(jnp.float32), axis=-1)
    vals, idx = lax.top_k(jnp.swapaxes(aff, 1, 2), cap)
    bidx = jnp.arange(b)[:, None, None]
    tok = h[bidx, idx]
    g = jnp.einsum('becd,edf->becf', tok, w_gate)
    u = jnp.einsum('becd,edf->becf', tok, w_up)
    o = jnp.einsum('becf,efd->becd', jax.nn.silu(g) * u, w_down) * vals[..., None].astype(h.dtype)
    return jnp.zeros_like(h).at[bidx, idx].add(o)


def setup_inputs(seed: int = 0) -> dict:
    key = jax.random.key(seed)
    ks = jax.random.split(key, 32)
    f32 = jnp.float32

    def nrm(i, shape, scale):
        return jax.random.normal(ks[i], shape, f32) * scale

    u = jax.random.uniform(ks[20], (DEPTH, 2, RNN_W), f32, 0.9, 0.999)
    a0 = u ** (1.0 / RG_C)
    return {
        'x': nrm(0, (BATCH, SEQ, D_MODEL), 1.0),
        'c': nrm(1, (BATCH, D_MODEL), 1.0),
        'ctx': nrm(2, (BATCH, CTX_LEN, D_MODEL), 1.0),
        'c_ctx': nrm(3, (D_MODEL,), 1.0),
        'norm1': 1.0 + nrm(4, (DEPTH, D_MODEL), 0.02),
        'norm2': 1.0 + nrm(5, (DEPTH, D_MODEL), 0.02),
        'w_mod': nrm(6, (DEPTH, D_MODEL, N_MOD * D_MODEL), 0.5 * D_MODEL ** -0.5),
        'b_mod': nrm(7, (DEPTH, N_MOD * D_MODEL), 0.02),
        'w_in': nrm(8, (DEPTH, D_MODEL, IN_W), D_MODEL ** -0.5),
        'conv_a': nrm(9, (DEPTH, CONV_K, CONV_W), CONV_K ** -0.5),
        'w_conv_out': nrm(10, (DEPTH, CONV_W, D_MODEL), CONV_W ** -0.5),
        'attn_sink': nrm(11, (DEPTH, N_HEADS), 0.5),
        'w_attn_out': nrm(12, (DEPTH, Q_W, D_MODEL), Q_W ** -0.5),
        'rnn_conv_w': nrm(13, (DEPTH, 2, RNN_CONV_K, RNN_W), RNN_CONV_K ** -0.5),
        'rnn_conv_b': nrm(14, (DEPTH, 2, RNN_W), 0.02),
        'rnn_w_a': nrm(15, (DEPTH, 2, RNN_BLOCKS, RNN_BLK, RNN_BLK), RNN_BLK ** -0.5),
        'rnn_b_a': nrm(16, (DEPTH, 2, RNN_W), 0.02),
        'rnn_w_x': nrm(17, (DEPTH, 2, RNN_BLOCKS, RNN_BLK, RNN_BLK), RNN_BLK ** -0.5),
        'rnn_b_x': nrm(18, (DEPTH, 2, RNN_W), 0.02),
        'rnn_lam': jnp.log(a0) - jnp.log1p(-a0),
        'w_rnn_out': nrm(19, (DEPTH, RNN_W, D_MODEL), RNN_W ** -0.5),
        'w_o': nrm(21, (DEPTH, D_MODEL, D_MODEL), D_MODEL ** -0.5),
        'w_router': nrm(22, (DEPTH, D_MODEL, N_EXPERTS), D_MODEL ** -0.5),
        'w_e_gate': nrm(23, (DEPTH, N_EXPERTS, D_MODEL, EXPERT_FF), D_MODEL ** -0.5),
        'w_e_up': nrm(24, (DEPTH, N_EXPERTS, D_MODEL, EXPERT_FF), D_MODEL ** -0.5),
        'w_e_down': nrm(25, (DEPTH, N_EXPERTS, EXPERT_FF, D_MODEL), EXPERT_FF ** -0.5),
        'final_norm': 1.0 + nrm(26, (D_MODEL,), 0.02),
    }


def reference(x, c, ctx, c_ctx, norm1, norm2, w_mod, b_mod, w_in, conv_a, w_conv_out, attn_sink, w_attn_out,
              rnn_conv_w, rnn_conv_b, rnn_w_a, rnn_b_a, rnn_w_x, rnn_b_x, rnn_lam, w_rnn_out, w_o, w_router,
              w_e_gate, w_e_up, w_e_down, final_norm):
    b, n, _ = x.shape
    n_ctx = ctx.shape[1]
    rope = rope_tables(n)
    xc = ctx
    for l in range(DEPTH):
        ctx_out = l < DEPTH - 1
        mod = (jax.nn.silu(c) @ w_mod[l] + b_mod[l])[:, None, :]
        mod_c = (jax.nn.silu(c_ctx) @ w_mod[l] + b_mod[l])[None, None, :]
        sh1, sc1, g1, sh2, sc2, g2 = jnp.split(mod, N_MOD, axis=-1)
        sh1c, sc1c, g1c, sh2c, sc2c, g2c = jnp.split(mod_c, N_MOD, axis=-1)

        h = rmsnorm(x, norm1[l]) * (1.0 + sc1) + sh1
        hc = rmsnorm(xc, norm1[l]) * (1.0 + sc1c) + sh1c
        z = h @ w_in[l]
        if ctx_out:
            zc = hc @ w_in[l]

            def pc(name):
                return _part(zc, name)
        else:
            def pc(name):
                return hc @ _part(w_in[l], name)

        q = apply_rope_2d(_part(z, 'q').reshape(b, n, N_HEADS, HEAD_DIM), rope)
        k = apply_rope_2d(_part(z, 'k').reshape(b, n, N_KV, HEAD_DIM), rope)
        v = _part(z, 'v').reshape(b, n, N_KV, HEAD_DIM)
        kc = pc('k').reshape(b, n_ctx, N_KV, HEAD_DIM)
        vc = pc('v').reshape(b, n_ctx, N_KV, HEAD_DIM)
        att = latent_window_attention(q, k, v, kc, vc, attn_sink[l]) @ w_attn_out[l]
        rec, rec_c = rg_lru_bidirectional(_part(z, 'r_x'), pc('r_x'), rnn_conv_w[l], rnn_conv_b[l], rnn_w_a[l],
                                          rnn_b_a[l], rnn_w_x[l], rnn_b_x[l], rnn_lam[l], ctx_out)
        rnn = (jax.nn.gelu(_part(z, 'r_y')) * rec) @ w_rnn_out[l]
        cnv = short_conv_mixer(_part(z, 'a_x'), _part(z, 'a_b'), _part(z, 'a_c'), conv_a[l]) @ w_conv_out[l]
        mix = (jax.nn.sigmoid(_part(z, 'g_conv')) * cnv + jax.nn.sigmoid(_part(z, 'g_attn')) * att
               + jax.nn.sigmoid(_part(z, 'g_rnn')) * rnn)
        x = x + g1 * (mix @ w_o[l])

        if ctx_out:
            qc = pc('q').reshape(b, n_ctx, N_HEADS, HEAD_DIM)
            att_c = context_attention(qc, kc, vc, attn_sink[l]) @ w_attn_out[l]
            rnn_c = (jax.nn.gelu(pc('r_y')) * rec_c) @ w_rnn_out[l]
            cnv_c = short_conv_mixer(pc('a_x'), pc('a_b'), pc('a_c'), conv_a[l]) @ w_conv_out[l]
            mix_c = (jax.nn.sigmoid(pc('g_conv')) * cnv_c + jax.nn.sigmoid(pc('g_attn')) * att_c
                     + jax.nn.sigmoid(pc('g_rnn')) * rnn_c)
            xc = xc + g1c * (mix_c @ w_o[l])
            hc2 = rmsnorm(xc, norm2[l]) * (1.0 + sc2c) + sh2c
            xc = xc + g2c * expert_choice_moe(hc2, w_router[l], w_e_gate[l], w_e_up[l], w_e_down[l])

        h2 = rmsnorm(x, norm2[l]) * (1.0 + sc2) + sh2
        x = x + g2 * expert_choice_moe(h2, w_router[l], w_e_gate[l], w_e_up[l], w_e_down[l])
    return rmsnorm(x, final_norm)
```

```python
import functools

import jax
import jax.numpy as jnp
from jax import lax
from jax.experimental import pallas as pl
from jax.experimental.pallas import tpu as pltpu

f32 = jnp.float32
bf16 = jnp.bfloat16
i32 = jnp.int32

D = 1024
EPS = 1e-6
CONV_W = 512
N_HEADS = 8
N_KV = 2
HEAD_DIM = 64
Q_W = N_HEADS * HEAD_DIM
KV_W = N_KV * HEAD_DIM
WINDOW = 128
GRID_W = 64
ROPE_BASE = 10000.0
NEG_INF = -1e30
RNN_W = 1024
RNN_BLOCKS = 8
RNN_BLK = RNN_W // RNN_BLOCKS
RNN_CONV_K = 4
RG_C = 8.0
N_EXPERTS = 16
EXPERT_FF = 1024
CAP_FACTOR = 2
N_MOD = 6

LANES = 128
SUBLANES = 8
BF16_ROWS = 16
VMEM_LIMIT = 56 * 1024 * 1024

TM = 256
QB = 128
K2_W = 2 * KV_W

G_A = 3 * CONV_W
G_Q = Q_W
G_K = K2_W
G_V = K2_W
G_RX = RNN_W
G_RY = RNN_W
G_G = 3 * D
IN_COLS = (G_A, G_Q, G_K, G_V, G_RX, G_RY, G_G)
IN_TOTAL = sum(IN_COLS)


def _cparams(*sem):
    return pltpu.CompilerParams(dimension_semantics=sem, vmem_limit_bytes=VMEM_LIMIT)


def _sigmoid(x):
    return jax.nn.sigmoid(x)


def _rms_mod(x, g, sc, sh):
    y = x * lax.rsqrt(jnp.mean(x * x, axis=-1, keepdims=True) + EPS)
    return (y * g) * (1.0 + sc) + sh


def _mod_kernel(c_ref, w_ref, b_ref, o_ref):
    c = c_ref[...]
    s = (c * _sigmoid(c)).astype(bf16)
    o_ref[...] = jnp.dot(s, w_ref[...].astype(bf16), preferred_element_type=f32) + b_ref[...]


def _modulation(cstack, w_mod, b_mod):
    depth = w_mod.shape[0]
    tn = 1536
    return pl.pallas_call(
        _mod_kernel,
        out_shape=jax.ShapeDtypeStruct((depth, SUBLANES, N_MOD * D), f32),
        grid=(depth, N_MOD * D // tn),
        in_specs=[
            pl.BlockSpec((SUBLANES, D), lambda l, j: (0, 0)),
            pl.BlockSpec((None, D, tn), lambda l, j: (l, 0, j)),
            pl.BlockSpec((None, 1, tn), lambda l, j: (l, 0, j)),
        ],
        out_specs=pl.BlockSpec((None, SUBLANES, tn), lambda l, j: (l, 0, j)),
        compiler_params=_cparams("arbitrary", "arbitrary"),
        name="modulation",
    )(cstack, w_mod, b_mod.reshape(depth, 1, N_MOD * D))


def _tiles_to_rows(ref, rows):
    return jnp.concatenate([ref[pl.ds(c, rows, stride=SUBLANES), :] for c in range(D // LANES)], axis=1)


def _rows_to_tiles(ref, val, rows):
    per = D // LANES
    for r in range(rows // SUBLANES):
        for c in range(per):
            ref[pl.ds(r * SUBLANES * per + c, SUBLANES, stride=per), :] = val[r * SUBLANES:(r + 1) * SUBLANES,
                                                                              c * LANES:(c + 1) * LANES]


def _rope(z, cos, s1, s2):
    outs = []
    for c in range(z.shape[1] // LANES):
        x = z[:, c * LANES:(c + 1) * LANES]
        outs.append(x * cos + pltpu.roll(x, LANES - 16, axis=1) * s1 + pltpu.roll(x, 16, axis=1) * s2)
    return jnp.concatenate(outs, axis=1)


def _inproj_kernel(*refs, with_moe):
    if with_moe:
        x_ref, acc_ref, modp_ref, mod_ref, n_ref, cos_ref, s1_ref, s2_ref, w_ref = refs[:9]
        xo_ref, za_ref, zq_ref, zk_ref, zv_ref, zrx_ref, zry_ref, zg_ref = refs[9:]
    else:
        x_ref, ctx_ref, mod_ref, n_ref, cos_ref, s1_ref, s2_ref, w_ref = refs[:8]
        za_ref, zq_ref, zk_ref, zv_ref, zrx_ref, zry_ref, zg_ref = refs[8:]
    if with_moe:
        x = x_ref[...] + modp_ref[:, 5 * D:6 * D] * _tiles_to_rows(acc_ref, TM)
        xo_ref[...] = x
    else:
        x = jnp.where(pl.program_id(1) == pl.num_programs(1) - 1, ctx_ref[...], x_ref[...])
    h = _rms_mod(x, n_ref[...], mod_ref[:, D:2 * D], mod_ref[:, 0:D]).astype(bf16)
    outs = (za_ref, zq_ref, zk_ref, zv_ref, zrx_ref, zry_ref, zg_ref)
    off = 0
    for gi, (o_ref, w) in enumerate(zip(outs, IN_COLS)):
        z = jnp.dot(h, w_ref[:, off:off + w], preferred_element_type=f32)
        if gi in (1, 2):
            z = _rope(z, cos_ref[...], s1_ref[...], s2_ref[...])
        o_ref[...] = z.astype(bf16)
        off += w


def _in_proj(xs, ctx, acc, modsel_prev, modsel, norm, tabs, w_all):
    with_moe = acc is not None
    b = xs.shape[0]
    nt = xs.shape[1] + (0 if with_moe else ctx.shape[1])
    n_tiles = nt // TM
    ct = n_tiles - 1
    tile = lambda w: pl.BlockSpec((None, TM, w), lambda i, t: (i, t, 0))
    mod_spec = pl.BlockSpec((None, None, 1, N_MOD * D), lambda i, t: (i, t // ct, 0, 0))
    tab_spec = pl.BlockSpec((TM, LANES), lambda i, t: (t, 0))
    if with_moe:
        in_specs = [tile(D), pl.BlockSpec((None, TM * SUBLANES, LANES), lambda i, t: (i, t, 0)), mod_spec]
        args = [xs, acc, modsel_prev]
    else:
        in_specs = [pl.BlockSpec((None, TM, D), lambda i, t: (i, jnp.minimum(t, ct - 1), 0)),
                    pl.BlockSpec((None, TM, D), lambda i, t: (i, 0, 0))]
        args = [xs, ctx]
    in_specs += [mod_spec, pl.BlockSpec((1, D), lambda i, t: (0, 0)), tab_spec, tab_spec, tab_spec,
                 pl.BlockSpec((D, IN_TOTAL), lambda i, t: (0, 0), pipeline_mode=pl.Buffered(1))]
    args += [modsel, norm, *tabs, w_all]
    out_shape = [jax.ShapeDtypeStruct((b, nt, w), bf16) for w in IN_COLS]
    out_specs = [tile(w) for w in IN_COLS]
    if with_moe:
        out_shape = [jax.ShapeDtypeStruct((b, nt, D), f32)] + out_shape
        out_specs = [tile(D)] + out_specs
    return pl.pallas_call(
        functools.partial(_inproj_kernel, with_moe=with_moe),
        out_shape=out_shape, grid=(b, n_tiles), in_specs=in_specs, out_specs=out_specs,
        compiler_params=_cparams("arbitrary", "arbitrary"), name="in_proj",
    )(*args)


def _attn_kernel(sink_ref, q_ref, kp_ref, kc_ref, kn_ref, kx_ref, vp_ref, vc_ref, vn_ref, vx_ref, o_ref, *,
                 n_lat_blocks, n_ctx):
    j = pl.program_id(1)
    group = N_HEADS // N_KV
    rows = group * QB
    nk = 3 * QB + n_ctx
    lo = lax.broadcasted_iota(i32, (QB, LANES), 1) < HEAD_DIM
    row = lax.broadcasted_iota(i32, (rows, nk), 0) % QB
    col = lax.broadcasted_iota(i32, (rows, nk), 1)
    qpos = j * QB + row
    kpos = (j - 1) * QB + col
    valid = ((col >= 3 * QB)
             | ((j < n_lat_blocks) & (kpos >= 0) & (kpos < n_lat_blocks * QB) & (jnp.abs(qpos - kpos) <= WINDOW)))
    zero = jnp.zeros((QB, LANES), bf16)
    for g in range(N_KV):
        sl = slice(g * LANES, (g + 1) * LANES)
        k = jnp.concatenate([kp_ref[:, sl], kc_ref[:, sl], kn_ref[:, sl], kx_ref[:, sl]], axis=0)
        v = jnp.concatenate([vp_ref[:, sl], vc_ref[:, sl], vn_ref[:, sl], vx_ref[:, sl]], axis=0)
        parts = []
        for pr in range(group // 2):
            qp = q_ref[:, (g * (group // 2) + pr) * LANES:(g * (group // 2) + pr + 1) * LANES]
            parts += [jnp.where(lo, qp, zero), jnp.where(lo, zero, qp)]
        q4 = jnp.concatenate(parts, axis=0)
        s = lax.dot_general(q4, k, (((1,), (1,)), ((), ())), preferred_element_type=f32) * (HEAD_DIM ** -0.5)
        s = jnp.where(valid, s, NEG_INF)
        sink = jnp.concatenate([jnp.full((QB, 1), sink_ref[0, g * group + hh], f32) for hh in range(group)], axis=0)
        m = jnp.maximum(jnp.max(s, axis=1, keepdims=True), sink)
        p = jnp.exp(s - m)
        den = jnp.sum(p, axis=1, keepdims=True) + jnp.exp(sink - m)
        o = jnp.dot(p.astype(bf16), v, preferred_element_type=f32) / den
        for pr in range(group // 2):
            pair = jnp.where(lo, o[(2 * pr) * QB:(2 * pr + 1) * QB], o[(2 * pr + 1) * QB:(2 * pr + 2) * QB])
            c0 = (g * (group // 2) + pr) * LANES
            o_ref[:, c0:c0 + LANES] = pair.astype(bf16)


def _attention(sink, zq, zk, zv, n_lat, n_qblocks):
    b, nt, _ = zq.shape
    n_ctx = nt - n_lat
    nlb = n_lat // QB
    cidx = n_lat // n_ctx
    win = lambda d: pl.BlockSpec((None, QB, K2_W), lambda i, j: (i, jnp.clip(j + d, 0, nlb - 1), 0))
    ctx = pl.BlockSpec((None, n_ctx, K2_W), lambda i, j: (i, cidx, 0))
    return pl.pallas_call(
        functools.partial(_attn_kernel, n_lat_blocks=nlb, n_ctx=n_ctx),
        out_shape=jax.ShapeDtypeStruct((b, nt, Q_W), bf16),
        grid=(b, n_qblocks),
        in_specs=[pl.BlockSpec(memory_space=pltpu.SMEM),
                  pl.BlockSpec((None, QB, Q_W), lambda i, j: (i, j, 0)),
                  win(-1), win(0), win(1), ctx, win(-1), win(0), win(1), ctx],
        out_specs=pl.BlockSpec((None, QB, Q_W), lambda i, j: (i, j, 0)),
        compiler_params=_cparams("arbitrary", "arbitrary"), name="attention",
    )(sink, zq, zk, zk, zk, zk, zv, zv, zv, zv)


def _rnn_chunk(s, ct, rev):
    return jnp.where(s == 0, ct, (ct - s) if rev else (s - 1))


def _rnn_kernel(u_ref, halo_ref, cw_ref, cb_ref, wg_ref, ba_ref, bx_ref, lam_ref, o_ref, ubuf, abuf, bbuf, hcar, *,
                rev, ct):
    s = pl.program_id(1)
    chunk = _rnn_chunk(s, ct, rev)

    @pl.when(s == 0)
    def _():
        hcar[...] = jnp.zeros_like(hcar)

    if rev:
        halo_ok = (chunk != ct - 1) & (chunk != ct)
        hal = halo_ref[0:SUBLANES, :].astype(f32)
        ubuf[0:TM, :] = u_ref[...].astype(f32)
        ubuf[TM:TM + SUBLANES, :] = jnp.where(halo_ok, hal, 0.0)
        base = 0
    else:
        halo_ok = (chunk != 0) & (chunk != ct)
        hal = halo_ref[BF16_ROWS - SUBLANES:BF16_ROWS, :].astype(f32)
        ubuf[0:SUBLANES, :] = jnp.where(halo_ok, hal, 0.0)
        ubuf[SUBLANES:SUBLANES + TM, :] = u_ref[...].astype(f32)
        base = SUBLANES - (RNN_CONV_K - 1)
    xt = cb_ref[...] + ubuf[base:base + TM, :] * cw_ref[0:1, :]
    for k in range(1, RNN_CONV_K):
        xt = xt + ubuf[base + k:base + k + TM, :] * cw_ref[k:k + 1, :]
    sp = jnp.logaddexp(-lam_ref[...], 0.0)
    for n in range(RNN_BLOCKS):
        sl = slice(n * RNN_BLK, (n + 1) * RNN_BLK)
        xn = xt[:, sl]
        g = jnp.dot(xn.astype(bf16), wg_ref[n], preferred_element_type=f32)
        r = _sigmoid(g[:, :RNN_BLK] + ba_ref[:, sl])
        gi = _sigmoid(g[:, RNN_BLK:] + bx_ref[:, sl])
        log_a = -RG_C * r * sp[:, sl]
        a = jnp.exp(log_a)
        abuf[:, sl] = a
        bbuf[:, sl] = jnp.sqrt(-jnp.tanh(log_a) * (a * a + 1.0)) * (gi * xn)

    rowi = lax.broadcasted_iota(i32, (SUBLANES, RNN_W), 0)
    n_groups = TM // SUBLANES

    def body(it, h):
        gidx = (n_groups - 1 - it) if rev else it
        r0 = pl.multiple_of(gidx * SUBLANES, SUBLANES)
        a = abuf[pl.ds(r0, SUBLANES), :]
        bb = bbuf[pl.ds(r0, SUBLANES), :]
        for sh in (1, 2, 4):
            rs = SUBLANES - sh if rev else sh
            a_sh = pltpu.roll(a, rs, axis=0)
            b_sh = pltpu.roll(bb, rs, axis=0)
            msk = (rowi < SUBLANES - sh) if rev else (rowi >= sh)
            bb = jnp.where(msk, bb + a * b_sh, bb)
            a = jnp.where(msk, a * a_sh, a)
        hh = bb + a * h
        bbuf[pl.ds(r0, SUBLANES), :] = hh
        last = hh[0:1, :] if rev else hh[SUBLANES - 1:SUBLANES, :]
        return jnp.broadcast_to(last, (SUBLANES, RNN_W))

    hcar[...] = lax.fori_loop(0, n_groups, body, hcar[...])
    o_ref[...] = bbuf[...].astype(bf16)


def _rnn(zrx, cw, cb, wg, ba, bx, lam, rev):
    b, nt, _ = zrx.shape
    n_tiles = nt // TM
    ct = n_tiles - 1
    per = TM // BF16_ROWS
    if rev:
        halo_map = lambda i, s: (i, jnp.minimum((_rnn_chunk(s, ct, rev) + 1) * per, nt // BF16_ROWS - 1), 0)
    else:
        halo_map = lambda i, s: (i, jnp.maximum(_rnn_chunk(s, ct, rev) * per - 1, 0), 0)
    full = lambda shape: pl.BlockSpec(shape, lambda i, s: (0,) * len(shape))
    return pl.pallas_call(
        functools.partial(_rnn_kernel, rev=rev, ct=ct),
        out_shape=jax.ShapeDtypeStruct((b, nt, RNN_W), bf16),
        grid=(b, n_tiles),
        in_specs=[pl.BlockSpec((None, TM, RNN_W), lambda i, s: (i, _rnn_chunk(s, ct, rev), 0)),
                  pl.BlockSpec((None, BF16_ROWS, RNN_W), halo_map),
                  full((RNN_CONV_K, RNN_W)), full((1, RNN_W)), full((RNN_BLOCKS, RNN_BLK, 2 * RNN_BLK)),
                  full((1, RNN_W)), full((1, RNN_W)), full((1, RNN_W))],
        out_specs=pl.BlockSpec((None, TM, RNN_W), lambda i, s: (i, _rnn_chunk(s, ct, rev), 0)),
        scratch_shapes=[pltpu.VMEM((TM + SUBLANES, RNN_W), f32), pltpu.VMEM((TM, RNN_W), f32),
                        pltpu.VMEM((TM, RNN_W), f32), pltpu.VMEM((SUBLANES, RNN_W), f32)],
        compiler_params=_cparams("arbitrary", "arbitrary"), name="rnn_bwd" if rev else "rnn_fwd",
    )(zrx, zrx, cw, cb, wg, ba, bx, lam)


def _merge_kernel(*refs, ct, split):
    (za_ref, zap_ref, zan_ref, att_ref, hf_ref, hb_ref, ry_ref, zg_ref, mod_ref, ca_ref, wc_ref,
     wa_ref, wr_ref, wo_ref, n2_ref, wrt_ref, xo_ref, h2_ref, aff_ref, cbuf) = refs[2 if split else 1:]
    t = pl.program_id(1)
    if split:
        x_in = jnp.where(t == ct, refs[1][...], refs[0][...])
    else:
        x_in = refs[0][...]
    cw = CONV_W
    za = za_ref[...].astype(f32)
    cu = za[:, 2 * cw:3 * cw] * za[:, 0:cw]
    zp = zap_ref[BF16_ROWS - 1:BF16_ROWS, :].astype(f32)
    zn = zan_ref[0:1, :].astype(f32)
    prev_ok = (t != 0) & (t != ct)
    next_ok = (t != ct - 1) & (t != ct)
    cu_p = jnp.where(prev_ok, zp[:, 2 * cw:3 * cw] * zp[:, 0:cw], 0.0)
    cu_n = jnp.where(next_ok, zn[:, 2 * cw:3 * cw] * zn[:, 0:cw], 0.0)
    cbuf[0:SUBLANES, :] = jnp.broadcast_to(cu_p, (SUBLANES, cw))
    cbuf[SUBLANES:SUBLANES + TM, :] = cu
    cbuf[SUBLANES + TM:2 * SUBLANES + TM, :] = jnp.broadcast_to(cu_n, (SUBLANES, cw))
    y = (cbuf[SUBLANES - 1:SUBLANES - 1 + TM, :] * ca_ref[0:1, :] + cu * ca_ref[1:2, :]
         + cbuf[SUBLANES + 1:SUBLANES + 1 + TM, :] * ca_ref[2:3, :])
    cnv = jnp.dot((za[:, cw:2 * cw] * y).astype(bf16), wc_ref[...], preferred_element_type=f32)
    att = jnp.dot(att_ref[...], wa_ref[...], preferred_element_type=f32)
    ry = ry_ref[...].astype(f32)
    gelu = 0.5 * ry * (1.0 + jnp.tanh(0.7978845608028654 * (ry + 0.044715 * (ry * ry * ry))))
    rec = hf_ref[...].astype(f32) + hb_ref[...].astype(f32)
    rnn = jnp.dot((gelu * rec).astype(bf16), wr_ref[...], preferred_element_type=f32)
    mix = (_sigmoid(zg_ref[:, 0:D].astype(f32)) * cnv + _sigmoid(zg_ref[:, D:2 * D].astype(f32)) * att
           + _sigmoid(zg_ref[:, 2 * D:3 * D].astype(f32)) * rnn)
    x = x_in + mod_ref[:, 2 * D:3 * D] * jnp.dot(mix.astype(bf16), wo_ref[...], preferred_element_type=f32)
    xo_ref[...] = x
    h2 = _rms_mod(x, n2_ref[...], mod_ref[:, 4 * D:5 * D], mod_ref[:, 3 * D:4 * D])
    _rows_to_tiles(h2_ref, h2, TM)
    logits = lax.dot_general(wrt_ref[...], h2.astype(bf16), (((1,), (1,)), ((), ())), preferred_element_type=f32)
    e = jnp.exp(logits - jnp.max(logits, axis=0, keepdims=True))
    aff_ref[...] = e / jnp.sum(e, axis=0, keepdims=True)


def _merge(xs, ctx, za, att, hf, hb, zry, zg, modsel, conv_a, wc, wa, wr, wo, norm2, wrt, n_tiles_eff):
    b, nt, _ = za.shape
    ct = nt // TM - 1
    per = TM // BF16_ROWS
    split = ctx is not None
    tile = lambda w: pl.BlockSpec((None, TM, w), lambda i, t: (i, t, 0))
    full = lambda shape: pl.BlockSpec(shape, lambda i, t: (0,) * len(shape), pipeline_mode=pl.Buffered(1))
    if split:
        x_specs = [pl.BlockSpec((None, TM, D), lambda i, t: (i, jnp.minimum(t, ct - 1), 0)),
                   pl.BlockSpec((None, TM, D), lambda i, t: (i, 0, 0))]
        x_args = [xs, ctx]
    else:
        x_specs = [tile(D)]
        x_args = [xs]
    return pl.pallas_call(
        functools.partial(_merge_kernel, ct=ct, split=split),
        out_shape=[jax.ShapeDtypeStruct((b, nt, D), f32),
                   jax.ShapeDtypeStruct((b, nt * SUBLANES, LANES), f32),
                   jax.ShapeDtypeStruct((b, N_EXPERTS, nt), f32)],
        grid=(b, n_tiles_eff),
        in_specs=x_specs + [tile(G_A),
                  pl.BlockSpec((None, BF16_ROWS, G_A), lambda i, t: (i, jnp.maximum(t * per - 1, 0), 0)),
                  pl.BlockSpec((None, BF16_ROWS, G_A),
                               lambda i, t: (i, jnp.minimum((t + 1) * per, nt // BF16_ROWS - 1), 0)),
                  tile(Q_W), tile(RNN_W), tile(RNN_W), tile(RNN_W), tile(G_G),
                  pl.BlockSpec((None, None, 1, N_MOD * D), lambda i, t: (i, t // ct, 0, 0)),
                  full((3, CONV_W)), full((CONV_W, D)), full((Q_W, D)), full((RNN_W, D)), full((D, D)),
                  full((1, D)), full((N_EXPERTS, D))],
        out_specs=[tile(D), pl.BlockSpec((None, TM * SUBLANES, LANES), lambda i, t: (i, t, 0)),
                   pl.BlockSpec((None, N_EXPERTS, TM), lambda i, t: (i, 0, t))],
        scratch_shapes=[pltpu.VMEM((TM + 2 * SUBLANES, CONV_W), f32)],
        compiler_params=_cparams("arbitrary", "arbitrary"), name="merge",
    )(*x_args, za, za, za, att, hf, hb, zry, zg, modsel, conv_a, wc, wa, wr, wo, norm2, wrt)


ROUTE_PC = 64
CUM_BLK = 256


def _cumsum_lanes(x, n):
    blk = min(CUM_BLK, n)
    tri = (lax.broadcasted_iota(i32, (blk, blk), 0) <= lax.broadcasted_iota(i32, (blk, blk), 1)).astype(bf16)
    carry = jnp.zeros((x.shape[0], 1), f32)
    outs = []
    for j in range(n // blk):
        c = jnp.dot(x[:, j * blk:(j + 1) * blk].astype(bf16), tri, preferred_element_type=f32) + carry
        outs.append(c)
        carry = c[:, blk - 1:blk]
    return jnp.concatenate(outs, axis=1)


def _route_kernel(aff_ref, idx_ref, val_ref, cum_s, av_s, *, n, cap):
    aff = aff_ref[...]
    thr = jnp.zeros((N_EXPERTS, 1), i32)
    for bit in range(30, -1, -1):
        cand = thr | (1 << bit)
        cnt = jnp.sum(jnp.where(aff >= lax.bitcast_convert_type(cand, f32), 1.0, 0.0), axis=1, keepdims=True)
        thr = jnp.where(cnt >= float(cap), cand, thr)
    thr_f = lax.bitcast_convert_type(thr, f32)
    gt = aff > thr_f
    eq = aff == thr_f
    need = float(cap) - jnp.sum(jnp.where(gt, 1.0, 0.0), axis=1, keepdims=True)
    sel = gt | (eq & (_cumsum_lanes(eq.astype(f32), n) <= need))
    cum_s[...] = _cumsum_lanes(sel.astype(f32), n).astype(i32)
    av_s[...] = jnp.where(sel, aff, 0.0)
    pc = min(ROUTE_PC, cap)
    n_pc = cap // pc
    piota = lax.broadcasted_iota(i32, (pc, n), 0)

    def body(it, carry):
        e = it // n_pc
        p0 = (it % n_pc) * pc
        crow = cum_s[pl.ds(e, 1), :]
        arow = av_s[pl.ds(e, 1), :]
        p = piota + p0
        idx = jnp.sum(jnp.where(crow <= p, 1.0, 0.0), axis=1, keepdims=True).astype(i32)
        val = jnp.sum(jnp.where(crow == p + 1, arow, 0.0), axis=1, keepdims=True)
        idx_ref[e, pl.ds(pl.multiple_of(p0, pc), pc), :] = idx
        val_ref[e, pl.ds(pl.multiple_of(p0, pc), pc), :] = val
        return carry

    lax.fori_loop(0, N_EXPERTS * n_pc, body, 0)


def _route(aff_t, n, cap, lane_block):
    b = aff_t.shape[0]
    idx, val = pl.pallas_call(
        functools.partial(_route_kernel, n=n, cap=cap),
        out_shape=[jax.ShapeDtypeStruct((b, N_EXPERTS, cap, 1), i32),
                   jax.ShapeDtypeStruct((b, N_EXPERTS, cap, 1), f32)],
        grid=(b,),
        in_specs=[pl.BlockSpec((None, N_EXPERTS, n), lambda i: (i, 0, lane_block))],
        out_specs=[pl.BlockSpec((None, N_EXPERTS, cap, 1), lambda i: (i, 0, 0, 0)),
                   pl.BlockSpec((None, N_EXPERTS, cap, 1), lambda i: (i, 0, 0, 0))],
        scratch_shapes=[pltpu.VMEM((N_EXPERTS, n), i32), pltpu.VMEM((N_EXPERTS, n), f32)],
        compiler_params=_cparams("arbitrary"), name="route",
    )(aff_t)
    return idx.reshape(-1), val.reshape(-1)


def _gather_kernel(idx_ref, h_ref, tok_ref, tbuf, *, cap):
    base = (pl.program_id(0) * N_EXPERTS + pl.program_id(1)) * cap
    sp = cap + SUBLANES
    for p in range(cap):
        i = idx_ref[base + p]
        tbuf[pl.ds(p, SUBLANES, stride=sp), :] = h_ref[pl.ds(pl.multiple_of(i * SUBLANES, SUBLANES), SUBLANES), :]
    tok_ref[...] = jnp.concatenate([tbuf[pl.ds(c * sp, cap), :] for c in range(D // LANES)], axis=1).astype(bf16)


def _gather(idx, h2t, n, cap, row_block):
    b = h2t.shape[0]
    return pl.pallas_call(
        functools.partial(_gather_kernel, cap=cap),
        out_shape=jax.ShapeDtypeStruct((b, N_EXPERTS, cap, D), bf16),
        grid_spec=pltpu.PrefetchScalarGridSpec(
            num_scalar_prefetch=1, grid=(b, N_EXPERTS),
            in_specs=[pl.BlockSpec((None, n * SUBLANES, LANES), lambda i, e, idx: (i, row_block, 0),
                                   pipeline_mode=pl.Buffered(1))],
            out_specs=pl.BlockSpec((None, None, cap, D), lambda i, e, idx: (i, e, 0, 0)),
            scratch_shapes=[pltpu.VMEM((SUBLANES * (cap + SUBLANES), LANES), f32)]),
        compiler_params=_cparams("arbitrary", "arbitrary"), name="moe_gather",
    )(idx, h2t)


def _ffn_kernel(tok_ref, wg_ref, wu_ref, wd_ref, o_ref, wg_s, wu_s, wd_s):
    @pl.when(pl.program_id(1) == 0)
    def _():
        wg_s[...] = wg_ref[...].astype(bf16)
        wu_s[...] = wu_ref[...].astype(bf16)
        wd_s[...] = wd_ref[...].astype(bf16)

    t = tok_ref[...]
    g = jnp.dot(t, wg_s[...], preferred_element_type=f32)
    u = jnp.dot(t, wu_s[...], preferred_element_type=f32)
    a = (g * _sigmoid(g) * u).astype(bf16)
    o_ref[...] = jnp.dot(a, wd_s[...], preferred_element_type=f32)


def _ffn(tok, wg, wu, wd, layer):
    b, _, cap, _ = tok.shape
    wspec = lambda s: pl.BlockSpec((None, None) + s, lambda e, i: (layer, e, 0, 0))
    return pl.pallas_call(
        _ffn_kernel,
        out_shape=jax.ShapeDtypeStruct((b, N_EXPERTS, cap, D), f32),
        grid=(N_EXPERTS, b),
        in_specs=[pl.BlockSpec((None, None, cap, D), lambda e, i: (i, e, 0, 0)),
                  wspec((D, EXPERT_FF)), wspec((D, EXPERT_FF)), wspec((EXPERT_FF, D))],
        out_specs=pl.BlockSpec((None, None, cap, D), lambda e, i: (i, e, 0, 0)),
        scratch_shapes=[pltpu.VMEM((D, EXPERT_FF), bf16), pltpu.VMEM((D, EXPERT_FF), bf16),
                        pltpu.VMEM((EXPERT_FF, D), bf16)],
        compiler_params=_cparams("arbitrary", "arbitrary"), name="moe_ffn",
    )(tok, wg, wu, wd)


SCATTER_UNROLL = 8


def _scatter_kernel(idx_ref, val_ref, o_ref, *rest, cap, aliased):
    acc_ref, tbuf = rest[-2:]
    e = pl.program_id(1)
    base = (pl.program_id(0) * N_EXPERTS + e) * cap
    sp = cap + SUBLANES

    @pl.when(e == 0)
    def _():
        acc_ref[...] = jnp.zeros_like(acc_ref)

    for c in range(D // LANES):
        tbuf[pl.ds(c * sp, cap), :] = o_ref[:, c * LANES:(c + 1) * LANES]
    u = min(SCATTER_UNROLL, cap)
    for p0 in range(0, cap, u):
        pend = []
        for p in range(p0, p0 + u):
            r0 = pl.multiple_of(idx_ref[base + p] * SUBLANES, SUBLANES)
            pend.append((r0, acc_ref[pl.ds(r0, SUBLANES), :] + tbuf[pl.ds(p, SUBLANES, stride=sp), :] * val_ref[base + p]))
        for r0, v in pend:
            acc_ref[pl.ds(r0, SUBLANES), :] = v


def _scatter(idx, val, o, acc_in, nt, n, cap, row_block):
    b = o.shape[0]
    aliased = acc_in is not None
    in_specs = [pl.BlockSpec((None, None, cap, D), lambda i, e, idx, val: (i, e, 0, 0))]
    args = [idx, val, o]
    if aliased:
        in_specs.append(pl.BlockSpec(memory_space=pl.ANY))
        args.append(acc_in)
    return pl.pallas_call(
        functools.partial(_scatter_kernel, cap=cap, aliased=aliased),
        out_shape=jax.ShapeDtypeStruct((b, nt * SUBLANES, LANES), f32),
        grid_spec=pltpu.PrefetchScalarGridSpec(
            num_scalar_prefetch=2, grid=(b, N_EXPERTS),
            in_specs=in_specs,
            out_specs=pl.BlockSpec((None, n * SUBLANES, LANES), lambda i, e, idx, val: (i, row_block, 0)),
            scratch_shapes=[pltpu.VMEM((SUBLANES * (cap + SUBLANES), LANES), f32)]),
        input_output_aliases=({3: 0} if aliased else {}),
        compiler_params=_cparams("arbitrary", "arbitrary"), name="moe_scatter",
    )(*args)


def _moe(aff_t, h2t, wg, wu, wd, layer, acc_in, nt, n, cap, lane_block):
    idx, val = _route(aff_t, n, cap, lane_block)
    tok = _gather(idx, h2t, n, cap, lane_block)
    o = _ffn(tok, wg, wu, wd, layer)
    return _scatter(idx, val, o, acc_in, nt, n, cap, lane_block)


def _final_kernel(x_ref, acc_ref, mod_ref, n_ref, o_ref):
    x = x_ref[...] + mod_ref[:, 5 * D:6 * D] * _tiles_to_rows(acc_ref, TM)
    o_ref[...] = (x * lax.rsqrt(jnp.mean(x * x, axis=-1, keepdims=True) + EPS)) * n_ref[...]


def _final(xs, acc, modsel, norm, n_lat):
    b = xs.shape[0]
    return pl.pallas_call(
        _final_kernel,
        out_shape=jax.ShapeDtypeStruct((b, n_lat, D), f32),
        grid=(b, n_lat // TM),
        in_specs=[pl.BlockSpec((None, TM, D), lambda i, t: (i, t, 0)),
                  pl.BlockSpec((None, TM * SUBLANES, LANES), lambda i, t: (i, t, 0)),
                  pl.BlockSpec((None, None, 1, N_MOD * D), lambda i, t: (i, 0, 0, 0)),
                  pl.BlockSpec((1, D), lambda i, t: (0, 0))],
        out_specs=pl.BlockSpec((None, TM, D), lambda i, t: (i, t, 0)),
        compiler_params=_cparams("arbitrary", "arbitrary"), name="final_norm",
    )(xs, acc, modsel, norm)


def _rope_tables(n_lat, n_ctx):
    pos = jnp.arange(n_lat, dtype=f32)
    rows = jnp.floor(pos / GRID_W)
    cols = pos - rows * GRID_W
    half = HEAD_DIM // 4
    inv = 1.0 / (ROPE_BASE ** (jnp.arange(0, 2 * half, 2, dtype=f32) / (2 * half)))
    lane = jnp.arange(LANES)
    d = lane % HEAD_DIM
    ang = jnp.where((d < 2 * half)[None, :], rows[:, None], cols[:, None]) * inv[d % half][None, :]
    first = ((d % (2 * half)) < half)[None, :]
    cos = jnp.cos(ang)
    sin = jnp.sin(ang)
    s1 = jnp.where(first, -sin, 0.0)
    s2 = jnp.where(first, 0.0, sin)
    pad = lambda t, v: jnp.concatenate([t, jnp.full((n_ctx, LANES), v, f32)], axis=0)
    return pad(cos, 1.0), pad(s1, 0.0), pad(s2, 0.0)


def _dup_heads(w):
    parts = []
    for g in range(N_KV):
        hd = w[:, g * HEAD_DIM:(g + 1) * HEAD_DIM]
        parts += [hd, hd]
    return jnp.concatenate(parts, axis=1)


def _fused_in_weights(w):
    a = w[:, 0:G_A]
    q = w[:, G_A:G_A + Q_W]
    k = w[:, G_A + Q_W:G_A + Q_W + KV_W]
    v = w[:, G_A + Q_W + KV_W:G_A + Q_W + 2 * KV_W]
    rest = w[:, G_A + Q_W + 2 * KV_W:]
    return jnp.concatenate([a, q, _dup_heads(k), _dup_heads(v), rest], axis=1).astype(bf16)


def kernel(x, c, ctx, c_ctx, norm1, norm2, w_mod, b_mod, w_in, conv_a, w_conv_out, attn_sink, w_attn_out, rnn_conv_w,
           rnn_conv_b, rnn_w_a, rnn_b_a, rnn_w_x, rnn_b_x, rnn_lam, w_rnn_out, w_o, w_router, w_e_gate, w_e_up,
           w_e_down, final_norm):
    b, n_lat, d = x.shape
    n_ctx = ctx.shape[1]
    depth = w_mod.shape[0]
    assert d == D and n_ctx == TM and n_lat % TM == 0 and b < SUBLANES
    nt = n_lat + n_ctx
    ct = nt // TM - 1
    cap_lat = CAP_FACTOR * n_lat // N_EXPERTS
    cap_ctx = CAP_FACTOR * n_ctx // N_EXPERTS

    cstack = jnp.concatenate([c, c_ctx[None, :], jnp.zeros((SUBLANES - b - 1, D), f32)], axis=0)
    mod = _modulation(cstack, w_mod, b_mod)
    tabs = _rope_tables(n_lat, n_ctx)

    xs = x
    acc = None
    modsel_prev = None
    for l in range(depth):
        ctx_out = l < depth - 1
        ctx_sep = ctx if l == 0 else None
        modsel = jnp.stack([mod[l, :b], jnp.broadcast_to(mod[l, b], (b, N_MOD * D))], axis=1)[:, :, None, :]
        outs = _in_proj(xs, ctx_sep, acc, modsel_prev, modsel, norm1[l][None, :], tabs, _fused_in_weights(w_in[l]))
        if acc is not None:
            xs, outs = outs[0], outs[1:]
        za, zq, zk, zv, zrx, zry, zg = outs
        att = _attention(attn_sink[l][None, :], zq, zk, zv, n_lat, (nt if ctx_out else n_lat) // QB)
        hs = []
        for dr in range(2):
            wg = jnp.concatenate([rnn_w_a[l, dr], rnn_w_x[l, dr]], axis=-1).astype(bf16)
            hs.append(_rnn(zrx, rnn_conv_w[l, dr], rnn_conv_b[l, dr][None, :], wg, rnn_b_a[l, dr][None, :],
                           rnn_b_x[l, dr][None, :], rnn_lam[l, dr][None, :], rev=bool(dr)))
        xs, h2t, aff_t = _merge(xs, ctx_sep, za, att, hs[0], hs[1], zry, zg, modsel, conv_a[l],
                                w_conv_out[l].astype(bf16),
                                w_attn_out[l].astype(bf16), w_rnn_out[l].astype(bf16), w_o[l].astype(bf16),
                                norm2[l][None, :], w_router[l].T.astype(bf16), (ct + 1) if ctx_out else ct)
        acc = _moe(aff_t, h2t, w_e_gate, w_e_up, w_e_down, l, None, nt, n_lat, cap_lat, 0)
        if ctx_out:
            acc = _moe(aff_t, h2t, w_e_gate, w_e_up, w_e_down, l, acc, nt, n_ctx, cap_ctx, n_lat // n_ctx)
        modsel_prev = modsel
    return _final(xs, acc, modsel_prev, final_norm[None, :], n_lat)
```

```python
import functools

import jax
import jax.numpy as jnp
from jax import lax
from jax.experimental import pallas as pl
from jax.experimental.pallas import tpu as pltpu

f32 = jnp.float32
bf16 = jnp.bfloat16
i32 = jnp.int32

D = 1024
EPS = 1e-6
CONV_W = 512
N_HEADS = 8
N_KV = 2
HEAD_DIM = 64
Q_W = N_HEADS * HEAD_DIM
KV_W = N_KV * HEAD_DIM
WINDOW = 128
GRID_W = 64
ROPE_BASE = 10000.0
NEG_INF = -1e30
RNN_W = 1024
RNN_BLOCKS = 8
RNN_BLK = RNN_W // RNN_BLOCKS
RNN_CONV_K = 4
RG_C = 8.0
N_EXPERTS = 16
EXPERT_FF = 1024
CAP_FACTOR = 2
N_MOD = 6

LANES = 128
SUBLANES = 8
BF16_ROWS = 16
VMEM_LIMIT = 56 * 1024 * 1024

TM = 256
QB = 128
K2_W = 2 * KV_W

G_A = 3 * CONV_W
G_Q = Q_W
G_K = K2_W
G_V = K2_W
G_RX = RNN_W
G_RY = RNN_W
G_G = 3 * D
IN_COLS = (G_A, G_Q, G_K, G_V, G_RX, G_RY, G_G)
IN_TOTAL = sum(IN_COLS)


def _cparams(*sem):
    return pltpu.CompilerParams(dimension_semantics=sem, vmem_limit_bytes=VMEM_LIMIT)


def _sigmoid(x):
    return 0.5 * jnp.tanh(0.5 * x) + 0.5


def _rms_mod(x, g, sc, sh):
    y = x * lax.rsqrt(jnp.mean(x * x, axis=-1, keepdims=True) + EPS)
    return (y * g) * (1.0 + sc) + sh


def _mod_kernel(c_ref, w_ref, b_ref, o_ref):
    c = c_ref[...]
    s = (c * _sigmoid(c)).astype(bf16)
    o_ref[...] = jnp.dot(s, w_ref[...].astype(bf16), preferred_element_type=f32) + b_ref[...]


def _modulation(cstack, w_mod, b_mod):
    depth = w_mod.shape[0]
    tn = 1536
    return pl.pallas_call(
        _mod_kernel,
        out_shape=jax.ShapeDtypeStruct((depth, SUBLANES, N_MOD * D), f32),
        grid=(depth, N_MOD * D // tn),
        in_specs=[
            pl.BlockSpec((SUBLANES, D), lambda l, j: (0, 0)),
            pl.BlockSpec((None, D, tn), lambda l, j: (l, 0, j)),
            pl.BlockSpec((None, 1, tn), lambda l, j: (l, 0, j)),
        ],
        out_specs=pl.BlockSpec((None, SUBLANES, tn), lambda l, j: (l, 0, j)),
        compiler_params=_cparams("arbitrary", "arbitrary"),
        name="modulation",
    )(cstack, w_mod, b_mod.reshape(depth, 1, N_MOD * D))


def _tiles_to_rows(ref, rows):
    return jnp.concatenate([ref[pl.ds(c, rows, stride=SUBLANES), :] for c in range(D // LANES)], axis=1)


def _rows_to_tiles(ref, val, rows):
    per = D // LANES
    for r in range(rows // SUBLANES):
        for c in range(per):
            ref[pl.ds(r * SUBLANES * per + c, SUBLANES, stride=per), :] = val[r * SUBLANES:(r + 1) * SUBLANES,
                                                                              c * LANES:(c + 1) * LANES]


def _rope(z, cos, s1, s2):
    outs = []
    for c in range(z.shape[1] // LANES):
        x = z[:, c * LANES:(c + 1) * LANES]
        outs.append(x * cos + pltpu.roll(x, LANES - 16, axis=1) * s1 + pltpu.roll(x, 16, axis=1) * s2)
    return jnp.concatenate(outs, axis=1)


def _inproj_kernel(*refs, with_moe):
    if with_moe:
        x_ref, acc_ref, modp_ref, mod_ref, n_ref, cos_ref, s1_ref, s2_ref, w_ref = refs[:9]
        xo_ref, za_ref, zq_ref, zk_ref, zv_ref, zrx_ref, zry_ref, zg_ref = refs[9:]
    else:
        x_ref, ctx_ref, mod_ref, n_ref, cos_ref, s1_ref, s2_ref, w_ref = refs[:8]
        za_ref, zq_ref, zk_ref, zv_ref, zrx_ref, zry_ref, zg_ref = refs[8:]
    if with_moe:
        x = x_ref[...] + modp_ref[:, 5 * D:6 * D] * _tiles_to_rows(acc_ref, TM)
        xo_ref[...] = x
    else:
        x = jnp.where(pl.program_id(1) == pl.num_programs(1) - 1, ctx_ref[...], x_ref[...])
    h = _rms_mod(x, n_ref[...], mod_ref[:, D:2 * D], mod_ref[:, 0:D]).astype(bf16)
    outs = (za_ref, zq_ref, zk_ref, zv_ref, zrx_ref, zry_ref, zg_ref)
    off = 0
    for gi, (o_ref, w) in enumerate(zip(outs, IN_COLS)):
        z = jnp.dot(h, w_ref[:, off:off + w], preferred_element_type=f32)
        if gi in (1, 2):
            z = _rope(z, cos_ref[...], s1_ref[...], s2_ref[...])
        if gi == 1:
            z = z * (HEAD_DIM ** -0.5)
        o_ref[...] = z.astype(bf16)
        off += w


def _in_proj(xs, ctx, acc, modsel_prev, modsel, norm, tabs, w_all):
    with_moe = acc is not None
    b = xs.shape[0]
    nt = xs.shape[1] + (0 if with_moe else ctx.shape[1])
    n_tiles = nt // TM
    ct = n_tiles - 1
    tile = lambda w: pl.BlockSpec((None, TM, w), lambda i, t: (i, t, 0))
    mod_spec = pl.BlockSpec((None, None, 1, N_MOD * D), lambda i, t: (i, t // ct, 0, 0))
    tab_spec = pl.BlockSpec((TM, LANES), lambda i, t: (t, 0))
    if with_moe:
        in_specs = [tile(D), pl.BlockSpec((None, TM * SUBLANES, LANES), lambda i, t: (i, t, 0)), mod_spec]
        args = [xs, acc, modsel_prev]
    else:
        in_specs = [pl.BlockSpec((None, TM, D), lambda i, t: (i, jnp.minimum(t, ct - 1), 0)),
                    pl.BlockSpec((None, TM, D), lambda i, t: (i, 0, 0))]
        args = [xs, ctx]
    in_specs += [mod_spec, pl.BlockSpec((1, D), lambda i, t: (0, 0)), tab_spec, tab_spec, tab_spec,
                 pl.BlockSpec((D, IN_TOTAL), lambda i, t: (0, 0), pipeline_mode=pl.Buffered(1))]
    args += [modsel, norm, *tabs, w_all]
    out_shape = [jax.ShapeDtypeStruct((b, nt, w), bf16) for w in IN_COLS]
    out_specs = [tile(w) for w in IN_COLS]
    if with_moe:
        out_shape = [jax.ShapeDtypeStruct((b, nt, D), f32)] + out_shape
        out_specs = [tile(D)] + out_specs
    return pl.pallas_call(
        functools.partial(_inproj_kernel, with_moe=with_moe),
        out_shape=out_shape, grid=(b, n_tiles), in_specs=in_specs, out_specs=out_specs,
        compiler_params=_cparams("arbitrary", "arbitrary"), name="in_proj",
    )(*args)


def _attn_kernel(sink_ref, q_ref, kp_ref, kc_ref, kn_ref, kx_ref, vp_ref, vc_ref, vn_ref, vx_ref, o_ref, *,
                 n_lat_blocks, n_ctx):
    j = pl.program_id(1)
    group = N_HEADS // N_KV
    nlb = n_lat_blocks
    lo = lax.broadcasted_iota(i32, (QB, LANES), 1) < HEAD_DIM
    diff = lax.broadcasted_iota(i32, (QB, QB), 1) - lax.broadcasted_iota(i32, (QB, QB), 0)
    lat = j < nlb
    ok = (diff >= jnp.where(lat & (j >= 1), 0, QB),
          diff >= jnp.where(lat, -QB, QB),
          -diff >= jnp.where(lat & (j <= nlb - 2), 0, QB))
    zero = jnp.zeros((QB, LANES), bf16)
    for g in range(N_KV):
        sl = slice(g * LANES, (g + 1) * LANES)
        k = jnp.concatenate([kp_ref[:, sl], kc_ref[:, sl], kn_ref[:, sl], kx_ref[:, sl]], axis=0)
        v = jnp.concatenate([vp_ref[:, sl], vc_ref[:, sl], vn_ref[:, sl], vx_ref[:, sl]], axis=0)
        parts = []
        for pr in range(group // 2):
            qp = q_ref[:, (g * (group // 2) + pr) * LANES:(g * (group // 2) + pr + 1) * LANES]
            parts += [jnp.where(lo, qp, zero), jnp.where(lo, zero, qp)]
        q4 = jnp.concatenate(parts, axis=0)
        s = lax.dot_general(q4, k, (((1,), (1,)), ((), ())), preferred_element_type=f32)
        ps, inv = [], []
        for hh in range(group):
            sh = s[hh * QB:(hh + 1) * QB]
            sm = jnp.concatenate([jnp.where(ok[kb], sh[:, kb * QB:(kb + 1) * QB], NEG_INF) for kb in range(3)]
                                 + [sh[:, 3 * QB:]], axis=1)
            sink = sink_ref[0, g * group + hh]
            m = jnp.maximum(jnp.max(sm, axis=1, keepdims=True), sink)
            p = jnp.exp(sm - m)
            inv.append(1.0 / (jnp.sum(p, axis=1, keepdims=True) + jnp.exp(sink - m)))
            ps.append(p.astype(bf16))
        o = jnp.dot(jnp.concatenate(ps, axis=0), v, preferred_element_type=f32) * jnp.concatenate(inv, axis=0)
        for pr in range(group // 2):
            pair = jnp.where(lo, o[(2 * pr) * QB:(2 * pr + 1) * QB], o[(2 * pr + 1) * QB:(2 * pr + 2) * QB])
            c0 = (g * (group // 2) + pr) * LANES
            o_ref[:, c0:c0 + LANES] = pair.astype(bf16)


def _attention(sink, zq, zk, zv, n_lat, n_qblocks):
    b, nt, _ = zq.shape
    n_ctx = nt - n_lat
    nlb = n_lat // QB
    cidx = n_lat // n_ctx
    win = lambda d: pl.BlockSpec((None, QB, K2_W), lambda i, j: (i, jnp.clip(j + d, 0, nlb - 1), 0))
    ctx = pl.BlockSpec((None, n_ctx, K2_W), lambda i, j: (i, cidx, 0))
    return pl.pallas_call(
        functools.partial(_attn_kernel, n_lat_blocks=nlb, n_ctx=n_ctx),
        out_shape=jax.ShapeDtypeStruct((b, n_qblocks * QB, Q_W), bf16),
        grid=(b, n_qblocks),
        in_specs=[pl.BlockSpec(memory_space=pltpu.SMEM),
                  pl.BlockSpec((None, QB, Q_W), lambda i, j: (i, j, 0)),
                  win(-1), win(0), win(1), ctx, win(-1), win(0), win(1), ctx],
        out_specs=pl.BlockSpec((None, QB, Q_W), lambda i, j: (i, j, 0)),
        compiler_params=_cparams("arbitrary", "arbitrary"), name="attention",
    )(sink, zq, zk, zk, zk, zk, zv, zv, zv, zv)


def _rnn_chunk(s, ct, rev):
    return jnp.where(s == 0, ct, (ct - s) if rev else (s - 1))


def _block_scan(a, b, h_in, rowi, rev):
    n = SUBLANES
    order = range(n - 1, -1, -1) if rev else range(n)
    hs = [None] * n
    ps = [None] * n
    prev = None
    for j in order:
        if prev is None:
            hs[j], ps[j] = b[j], a[j]
        else:
            hs[j], ps[j] = a[j] * hs[prev] + b[j], a[j] * ps[prev]
        prev = j
    he, pe = hs[prev], ps[prev]
    for sh in (1, 2, 4):
        rs = n - sh if rev else sh
        msk = (rowi < n - sh) if rev else (rowi >= sh)
        he, pe = (jnp.where(msk, he + pe * pltpu.roll(he, rs, axis=0), he),
                  jnp.where(msk, pe * pltpu.roll(pe, rs, axis=0), pe))
    e = he + pe * h_in
    carry = jnp.where((rowi == n - 1) if rev else (rowi == 0), h_in, pltpu.roll(e, n - 1 if rev else 1, axis=0))
    out = [hs[j] + ps[j] * carry for j in range(n)]
    last = e[0:1, :] if rev else e[n - 1:n, :]
    return out, jnp.broadcast_to(last, (n, LANES))


def _rnn_kernel(u_ref, halo_ref, cw_ref, cb_ref, wg_ref, ba_ref, bx_ref, lam_ref, o_ref, ubuf, obuf, hcar, *,
                rev, ct):
    s = pl.program_id(1)
    chunk = _rnn_chunk(s, ct, rev)

    @pl.when(s == 0)
    def _():
        hcar[...] = jnp.zeros_like(hcar)

    u = u_ref[...].astype(f32)
    if rev:
        halo_ok = (chunk != ct - 1) & (chunk != ct)
        base = 0
    else:
        halo_ok = (chunk != 0) & (chunk != ct)
        base = BF16_ROWS - (RNN_CONV_K - 1)
    hal = jnp.where(halo_ok, halo_ref[...].astype(f32), 0.0)
    n_lt = RNN_W // LANES
    for c in range(n_lt):
        sl = slice(c * LANES, (c + 1) * LANES)
        if rev:
            ubuf[c, 0:TM, :] = u[:, sl]
            ubuf[c, TM:TM + BF16_ROWS, :] = hal[:, sl]
        else:
            ubuf[c, 0:BF16_ROWS, :] = hal[:, sl]
            ubuf[c, BF16_ROWS:BF16_ROWS + TM, :] = u[:, sl]
    sp = jnp.logaddexp(-lam_ref[...], 0.0)
    rowi = lax.broadcasted_iota(i32, (SUBLANES, LANES), 0)
    blk_rows = SUBLANES * SUBLANES
    n_blk = TM // blk_rows
    for c in range(n_lt):
        sl = slice(c * LANES, (c + 1) * LANES)
        taps = [cw_ref[k:k + 1, sl] for k in range(RNN_CONV_K)]
        xt = []
        for blk in range(n_blk):
            for j in range(SUBLANES):
                r0 = base + blk * blk_rows + j
                acc = cb_ref[:, sl] + ubuf[c, pl.ds(r0, SUBLANES, stride=SUBLANES), :] * taps[0]
                for k in range(1, RNN_CONV_K):
                    acc = acc + ubuf[c, pl.ds(r0 + k, SUBLANES, stride=SUBLANES), :] * taps[k]
                xt.append(acc)
        xp = jnp.concatenate(xt, axis=0)
        g = jnp.dot(xp.astype(bf16), wg_ref[c], preferred_element_type=f32)
        t_r = jnp.tanh(g[:, :RNN_BLK] + 0.5 * ba_ref[:, sl])
        t_i = jnp.tanh(g[:, RNN_BLK:] + 0.5 * bx_ref[:, sl])
        c4 = (-0.5 * RG_C) * sp[:, sl]
        log_a = c4 * t_r + c4
        a = jnp.exp(log_a)
        y = -jnp.tanh(log_a) * (a * a + 1.0)
        xh = 0.5 * xp
        bb = jnp.where(y > 0.0, y * lax.rsqrt(y), 0.0) * (xh * t_i + xh)
        h_in = hcar[:, sl]
        for blk in (range(n_blk - 1, -1, -1) if rev else range(n_blk)):
            rows = lambda j: slice((blk * SUBLANES + j) * SUBLANES, (blk * SUBLANES + j + 1) * SUBLANES)
            hs, h_in = _block_scan([a[rows(j)] for j in range(SUBLANES)], [bb[rows(j)] for j in range(SUBLANES)],
                                   h_in, rowi, rev)
            for j in range(SUBLANES):
                obuf[c, pl.ds(blk * blk_rows + j, SUBLANES, stride=SUBLANES), :] = hs[j]
        hcar[:, sl] = h_in
    o_ref[...] = jnp.concatenate([obuf[c] for c in range(n_lt)], axis=1).astype(bf16)


def _rnn(zrx, cw, cb, wg, ba, bx, lam, rev):
    b, nt, _ = zrx.shape
    n_tiles = nt // TM
    ct = n_tiles - 1
    per = TM // BF16_ROWS
    if rev:
        halo_map = lambda i, s: (i, jnp.minimum((_rnn_chunk(s, ct, rev) + 1) * per, nt // BF16_ROWS - 1), 0)
    else:
        halo_map = lambda i, s: (i, jnp.maximum(_rnn_chunk(s, ct, rev) * per - 1, 0), 0)
    full = lambda shape: pl.BlockSpec(shape, lambda i, s: (0,) * len(shape))
    return pl.pallas_call(
        functools.partial(_rnn_kernel, rev=rev, ct=ct),
        out_shape=jax.ShapeDtypeStruct((b, nt, RNN_W), bf16),
        grid=(b, n_tiles),
        in_specs=[pl.BlockSpec((None, TM, RNN_W), lambda i, s: (i, _rnn_chunk(s, ct, rev), 0)),
                  pl.BlockSpec((None, BF16_ROWS, RNN_W), halo_map),
                  full((RNN_CONV_K, RNN_W)), full((1, RNN_W)), full((RNN_BLOCKS, RNN_BLK, 2 * RNN_BLK)),
                  full((1, RNN_W)), full((1, RNN_W)), full((1, RNN_W))],
        out_specs=pl.BlockSpec((None, TM, RNN_W), lambda i, s: (i, _rnn_chunk(s, ct, rev), 0)),
        scratch_shapes=[pltpu.VMEM((RNN_W // LANES, TM + BF16_ROWS, LANES), f32),
                        pltpu.VMEM((RNN_W // LANES, TM, LANES), f32), pltpu.VMEM((SUBLANES, RNN_W), f32)],
        compiler_params=_cparams("arbitrary", "arbitrary"), name="rnn_bwd" if rev else "rnn_fwd",
    )(zrx, zrx, cw, cb, wg, ba, bx, lam)


def _merge_kernel(*refs, ct, split):
    (za_ref, zap_ref, zan_ref, att_ref, hf_ref, hb_ref, ry_ref, zg_ref, mod_ref, ca_ref, wc_ref,
     wa_ref, wr_ref, wo_ref, n2_ref, wrt_ref, xo_ref, h2_ref, aff_ref) = refs[2 if split else 1:]
    t = pl.program_id(1)
    if split:
        x_in = jnp.where(t == ct, refs[1][...], refs[0][...])
    else:
        x_in = refs[0][...]
    cw = CONV_W
    za = za_ref[...].astype(f32)
    cu = za[:, 2 * cw:3 * cw] * za[:, 0:cw]
    zp = zap_ref[BF16_ROWS - 1:BF16_ROWS, :].astype(f32)
    zn = zan_ref[0:1, :].astype(f32)
    prev_ok = (t != 0) & (t != ct)
    next_ok = (t != ct - 1) & (t != ct)
    cu_p = jnp.where(prev_ok, zp[:, 2 * cw:3 * cw] * zp[:, 0:cw], 0.0)
    cu_n = jnp.where(next_ok, zn[:, 2 * cw:3 * cw] * zn[:, 0:cw], 0.0)
    rowi = lax.broadcasted_iota(i32, (TM, cw), 0)
    cu_prev = jnp.where(rowi == 0, cu_p, pltpu.roll(cu, 1, axis=0))
    cu_next = jnp.where(rowi == TM - 1, cu_n, pltpu.roll(cu, TM - 1, axis=0))
    y = cu_prev * ca_ref[0:1, :] + cu * ca_ref[1:2, :] + cu_next * ca_ref[2:3, :]
    cnv = jnp.dot((za[:, cw:2 * cw] * y).astype(bf16), wc_ref[...], preferred_element_type=f32)
    att = jnp.dot(att_ref[...], wa_ref[...], preferred_element_type=f32)
    ry = ry_ref[...].astype(f32)
    t_y = jnp.tanh(ry * (0.7978845608028654 + (0.7978845608028654 * 0.044715) * (ry * ry)))
    gelu2 = ry * t_y + ry
    rec = hf_ref[...].astype(f32) + hb_ref[...].astype(f32)
    rnn = jnp.dot((gelu2 * rec).astype(bf16), wr_ref[...], preferred_element_type=f32)
    mix2 = None
    for k, br in enumerate((cnv, att, rnn)):
        term = jnp.tanh(zg_ref[:, k * D:(k + 1) * D].astype(f32)) * br + br
        mix2 = term if mix2 is None else mix2 + term
    x = x_in + mod_ref[:, 2 * D:3 * D] * jnp.dot(mix2.astype(bf16), wo_ref[...], preferred_element_type=f32)
    xo_ref[...] = x
    h2 = _rms_mod(x, n2_ref[...], mod_ref[:, 4 * D:5 * D], mod_ref[:, 3 * D:4 * D])
    _rows_to_tiles(h2_ref, h2, TM)
    logits = lax.dot_general(wrt_ref[...], h2.astype(bf16), (((1,), (1,)), ((), ())), preferred_element_type=f32)
    e = jnp.exp(logits - jnp.max(logits, axis=0, keepdims=True))
    aff_ref[...] = e / jnp.sum(e, axis=0, keepdims=True)


def _merge(xs, ctx, za, att, hf, hb, zry, zg, modsel, conv_a, wc, wa, wr, wo, norm2, wrt, n_tiles_eff):
    b, nt, _ = za.shape
    ct = nt // TM - 1
    per = TM // BF16_ROWS
    split = ctx is not None
    tile = lambda w: pl.BlockSpec((None, TM, w), lambda i, t: (i, t, 0))
    full = lambda shape: pl.BlockSpec(shape, lambda i, t: (0,) * len(shape), pipeline_mode=pl.Buffered(1))
    if split:
        x_specs = [pl.BlockSpec((None, TM, D), lambda i, t: (i, jnp.minimum(t, ct - 1), 0)),
                   pl.BlockSpec((None, TM, D), lambda i, t: (i, 0, 0))]
        x_args = [xs, ctx]
    else:
        x_specs = [tile(D)]
        x_args = [xs]
    return pl.pallas_call(
        functools.partial(_merge_kernel, ct=ct, split=split),
        out_shape=[jax.ShapeDtypeStruct((b, n_tiles_eff * TM, D), f32),
                   jax.ShapeDtypeStruct((b, n_tiles_eff * TM * SUBLANES, LANES), f32),
                   jax.ShapeDtypeStruct((b, N_EXPERTS, n_tiles_eff * TM), f32)],
        grid=(b, n_tiles_eff),
        in_specs=x_specs + [tile(G_A),
                  pl.BlockSpec((None, BF16_ROWS, G_A), lambda i, t: (i, jnp.maximum(t * per - 1, 0), 0)),
                  pl.BlockSpec((None, BF16_ROWS, G_A),
                               lambda i, t: (i, jnp.minimum((t + 1) * per, nt // BF16_ROWS - 1), 0)),
                  tile(Q_W), tile(RNN_W), tile(RNN_W), tile(RNN_W), tile(G_G),
                  pl.BlockSpec((None, None, 1, N_MOD * D), lambda i, t: (i, t // ct, 0, 0)),
                  full((3, CONV_W)), full((CONV_W, D)), full((Q_W, D)), full((RNN_W, D)), full((D, D)),
                  full((1, D)), full((N_EXPERTS, D))],
        out_specs=[tile(D), pl.BlockSpec((None, TM * SUBLANES, LANES), lambda i, t: (i, t, 0)),
                   pl.BlockSpec((None, N_EXPERTS, TM), lambda i, t: (i, 0, t))],
        compiler_params=_cparams("arbitrary", "arbitrary"), name="merge",
    )(*x_args, za, za, za, att, hf, hb, zry, zg, modsel, conv_a, wc, wa, wr, wo, norm2, wrt)


ROUTE_QW = 64
ROUTE_TOK_SHIFT = 6
CUM_BLK = 256


def _cumsum_lanes(x, n):
    blk = min(CUM_BLK, n)
    tri = (lax.broadcasted_iota(i32, (blk, blk), 0) <= lax.broadcasted_iota(i32, (blk, blk), 1)).astype(bf16)
    carry = jnp.zeros((x.shape[0], 1), f32)
    outs = []
    for j in range(n // blk):
        c = jnp.dot(x[:, j * blk:(j + 1) * blk].astype(bf16), tri, preferred_element_type=f32) + carry
        outs.append(c)
        carry = c[:, blk - 1:blk]
    return jnp.concatenate(outs, axis=1)


def _route_kernel(aff_ref, idx_ref, val_ref, pos_s, *, n, cap):
    aff = aff_ref[...]
    thr = jnp.zeros((N_EXPERTS, 1), i32)
    for bit in range(30, -1, -1):
        cand = thr | (1 << bit)
        cnt = jnp.sum(jnp.where(aff >= lax.bitcast_convert_type(cand, f32), 1.0, 0.0), axis=1, keepdims=True)
        thr = jnp.where(cnt >= float(cap), cand, thr)
    thr_f = lax.bitcast_convert_type(thr, f32)
    gt = aff > thr_f
    eq = aff == thr_f
    need = float(cap) - jnp.sum(jnp.where(gt, 1.0, 0.0), axis=1, keepdims=True)
    sel = gt | (eq & (_cumsum_lanes(eq.astype(f32), n) <= need))
    pos_s[...] = jnp.where(sel, _cumsum_lanes(sel.astype(f32), n).astype(i32) - 1, -1)

    qw = min(ROUTE_QW, cap)
    shift = qw.bit_length() - 1
    tok = lax.broadcasted_iota(i32, (1, n), 1)
    tok_hi = (tok >> ROUTE_TOK_SHIFT).astype(f32)
    tok_lo = (tok & ((1 << ROUTE_TOK_SHIFT) - 1)).astype(f32)
    hi_iota = lax.broadcasted_iota(i32, (SUBLANES, n), 0)
    lo_iota = lax.broadcasted_iota(i32, (qw, n), 0)

    def body(e, carry):
        pos = pos_s[pl.ds(e, 1), :]
        a0 = aff_ref[pl.ds(e, 1), :]
        a_h = a0.astype(bf16).astype(f32)
        a_m = (a0 - a_h).astype(bf16).astype(f32)
        a_l = ((a0 - a_h) - a_m).astype(bf16).astype(f32)
        in_hi = (pos >> shift) == hi_iota
        lhs = jnp.concatenate([jnp.where(in_hi, r, 0.0) for r in (tok_hi, tok_lo, a_h, a_m, a_l)], axis=0)
        onehot = jnp.where((pos & (qw - 1)) == lo_iota, 1.0, 0.0).astype(bf16)
        res = lax.dot_general(lhs.astype(bf16), onehot, (((1,), (1,)), ((), ())), preferred_element_type=f32)
        r_hi, r_lo, v_h, v_m, v_l = (res[k * SUBLANES:(k + 1) * SUBLANES] for k in range(5))
        idx_ref[e] = (r_hi * float(1 << ROUTE_TOK_SHIFT) + r_lo).astype(i32)[:cap // qw]
        val_ref[e] = ((v_h + v_m) + v_l)[:cap // qw]
        return carry

    lax.fori_loop(0, N_EXPERTS, body, 0)


def _route(aff_t, n, cap, lane_block):
    b = aff_t.shape[0]
    qw = min(ROUTE_QW, cap)
    assert cap % qw == 0 and cap // qw <= SUBLANES and qw & (qw - 1) == 0
    idx, val = pl.pallas_call(
        functools.partial(_route_kernel, n=n, cap=cap),
        out_shape=[jax.ShapeDtypeStruct((b, N_EXPERTS, cap // qw, qw), i32),
                   jax.ShapeDtypeStruct((b, N_EXPERTS, cap // qw, qw), f32)],
        grid=(b,),
        in_specs=[pl.BlockSpec((None, N_EXPERTS, n), lambda i: (i, 0, lane_block))],
        out_specs=[pl.BlockSpec((None, N_EXPERTS, cap // qw, qw), lambda i: (i, 0, 0, 0)),
                   pl.BlockSpec((None, N_EXPERTS, cap // qw, qw), lambda i: (i, 0, 0, 0))],
        scratch_shapes=[pltpu.VMEM((N_EXPERTS, n), i32)],
        compiler_params=_cparams("arbitrary"), name="route",
    )(aff_t)
    return idx.reshape(b, N_EXPERTS, cap), val.reshape(b, N_EXPERTS, cap)


def _gather_kernel(idx_ref, h_ref, tok_ref, tbuf, *, cap):
    base = (pl.program_id(0) * N_EXPERTS + pl.program_id(1)) * cap
    sp = cap + SUBLANES
    for p in range(cap):
        i = idx_ref[base + p]
        tbuf[pl.ds(p, SUBLANES, stride=sp), :] = h_ref[pl.ds(pl.multiple_of(i * SUBLANES, SUBLANES), SUBLANES), :]
    tok_ref[...] = jnp.concatenate([tbuf[pl.ds(c * sp, cap), :] for c in range(D // LANES)], axis=1).astype(bf16)


def _gather(idx, h2t, cap):
    b, rows, _ = h2t.shape
    return pl.pallas_call(
        functools.partial(_gather_kernel, cap=cap),
        out_shape=jax.ShapeDtypeStruct((b, N_EXPERTS, cap, D), bf16),
        grid_spec=pltpu.PrefetchScalarGridSpec(
            num_scalar_prefetch=1, grid=(b, N_EXPERTS),
            in_specs=[pl.BlockSpec((None, rows, LANES), lambda i, e, idx: (i, 0, 0),
                                   pipeline_mode=pl.Buffered(1))],
            out_specs=pl.BlockSpec((None, None, cap, D), lambda i, e, idx: (i, e, 0, 0)),
            scratch_shapes=[pltpu.VMEM((SUBLANES * (cap + SUBLANES), LANES), f32)]),
        compiler_params=_cparams("arbitrary", "arbitrary"), name="moe_gather",
    )(idx, h2t)


def _ffn_kernel(tok_ref, wg_ref, wu_ref, wd_ref, o_ref, wg_s, wu_s, wd_s):
    @pl.when(pl.program_id(1) == 0)
    def _():
        wg_s[...] = wg_ref[...].astype(bf16)
        wu_s[...] = wu_ref[...].astype(bf16)
        wd_s[...] = wd_ref[...].astype(bf16)

    t = tok_ref[...]
    g = jnp.dot(t, wg_s[...], preferred_element_type=f32)
    u = jnp.dot(t, wu_s[...], preferred_element_type=f32)
    a = (g * _sigmoid(g) * u).astype(bf16)
    o_ref[...] = jnp.dot(a, wd_s[...], preferred_element_type=f32)


def _ffn(tok, wg, wu, wd, layer):
    b, _, cap, _ = tok.shape
    wspec = lambda s: pl.BlockSpec((None, None) + s, lambda e, i: (layer, e, 0, 0))
    return pl.pallas_call(
        _ffn_kernel,
        out_shape=jax.ShapeDtypeStruct((b, N_EXPERTS, cap, D), f32),
        grid=(N_EXPERTS, b),
        in_specs=[pl.BlockSpec((None, None, cap, D), lambda e, i: (i, e, 0, 0)),
                  wspec((D, EXPERT_FF)), wspec((D, EXPERT_FF)), wspec((EXPERT_FF, D))],
        out_specs=pl.BlockSpec((None, None, cap, D), lambda e, i: (i, e, 0, 0)),
        scratch_shapes=[pltpu.VMEM((D, EXPERT_FF), bf16), pltpu.VMEM((D, EXPERT_FF), bf16),
                        pltpu.VMEM((EXPERT_FF, D), bf16)],
        compiler_params=_cparams("arbitrary", "arbitrary"), name="moe_ffn",
    )(tok, wg, wu, wd)


SCATTER_UNROLL = 8


def _scatter_kernel(idx_ref, val_ref, o_ref, acc_ref, tbuf, *, cap):
    e = pl.program_id(1)
    base = (pl.program_id(0) * N_EXPERTS + e) * cap
    sp = cap + SUBLANES

    @pl.when(e == 0)
    def _():
        acc_ref[...] = jnp.zeros_like(acc_ref)

    for c in range(D // LANES):
        tbuf[pl.ds(c * sp, cap), :] = o_ref[:, c * LANES:(c + 1) * LANES]
    u = min(SCATTER_UNROLL, cap)
    for p0 in range(0, cap, u):
        pend = []
        for p in range(p0, p0 + u):
            r0 = pl.multiple_of(idx_ref[base + p] * SUBLANES, SUBLANES)
            pend.append((r0, acc_ref[pl.ds(r0, SUBLANES), :] + tbuf[pl.ds(p, SUBLANES, stride=sp), :] * val_ref[base + p]))
        for r0, v in pend:
            acc_ref[pl.ds(r0, SUBLANES), :] = v


def _scatter(idx, val, o, rows):
    b, _, cap, _ = o.shape
    return pl.pallas_call(
        functools.partial(_scatter_kernel, cap=cap),
        out_shape=jax.ShapeDtypeStruct((b, rows, LANES), f32),
        grid_spec=pltpu.PrefetchScalarGridSpec(
            num_scalar_prefetch=2, grid=(b, N_EXPERTS),
            in_specs=[pl.BlockSpec((None, None, cap, D), lambda i, e, idx, val: (i, e, 0, 0))],
            out_specs=pl.BlockSpec((None, rows, LANES), lambda i, e, idx, val: (i, 0, 0)),
            scratch_shapes=[pltpu.VMEM((SUBLANES * (cap + SUBLANES), LANES), f32)]),
        compiler_params=_cparams("arbitrary", "arbitrary"), name="moe_scatter",
    )(idx, val, o)


def _moe(aff_t, h2t, wg, wu, wd, layer, n_lat, n_ctx):
    idx, val = _route(aff_t, n_lat, CAP_FACTOR * n_lat // N_EXPERTS, 0)
    if n_ctx:
        idx_c, val_c = _route(aff_t, n_ctx, CAP_FACTOR * n_ctx // N_EXPERTS, n_lat // n_ctx)
        idx = jnp.concatenate([idx, idx_c + n_lat], axis=-1)
        val = jnp.concatenate([val, val_c], axis=-1)
    cap = idx.shape[-1]
    idx = idx.reshape(-1)
    val = val.reshape(-1)
    tok = _gather(idx, h2t, cap)
    o = _ffn(tok, wg, wu, wd, layer)
    return _scatter(idx, val, o, h2t.shape[1])


def _final_kernel(x_ref, acc_ref, mod_ref, n_ref, o_ref):
    x = x_ref[...] + mod_ref[:, 5 * D:6 * D] * _tiles_to_rows(acc_ref, TM)
    o_ref[...] = (x * lax.rsqrt(jnp.mean(x * x, axis=-1, keepdims=True) + EPS)) * n_ref[...]


def _final(xs, acc, modsel, norm, n_lat):
    b = xs.shape[0]
    return pl.pallas_call(
        _final_kernel,
        out_shape=jax.ShapeDtypeStruct((b, n_lat, D), f32),
        grid=(b, n_lat // TM),
        in_specs=[pl.BlockSpec((None, TM, D), lambda i, t: (i, t, 0)),
                  pl.BlockSpec((None, TM * SUBLANES, LANES), lambda i, t: (i, t, 0)),
                  pl.BlockSpec((None, None, 1, N_MOD * D), lambda i, t: (i, 0, 0, 0)),
                  pl.BlockSpec((1, D), lambda i, t: (0, 0))],
        out_specs=pl.BlockSpec((None, TM, D), lambda i, t: (i, t, 0)),
        compiler_params=_cparams("arbitrary", "arbitrary"), name="final_norm",
    )(xs, acc, modsel, norm)


def _rope_tables(n_lat, n_ctx):
    pos = jnp.arange(n_lat, dtype=f32)
    rows = jnp.floor(pos / GRID_W)
    cols = pos - rows * GRID_W
    half = HEAD_DIM // 4
    inv = 1.0 / (ROPE_BASE ** (jnp.arange(0, 2 * half, 2, dtype=f32) / (2 * half)))
    lane = jnp.arange(LANES)
    d = lane % HEAD_DIM
    ang = jnp.where((d < 2 * half)[None, :], rows[:, None], cols[:, None]) * inv[d % half][None, :]
    first = ((d % (2 * half)) < half)[None, :]
    cos = jnp.cos(ang)
    sin = jnp.sin(ang)
    s1 = jnp.where(first, -sin, 0.0)
    s2 = jnp.where(first, 0.0, sin)
    pad = lambda t, v: jnp.concatenate([t, jnp.full((n_ctx, LANES), v, f32)], axis=0)
    return pad(cos, 1.0), pad(s1, 0.0), pad(s2, 0.0)


def _dup_heads(w):
    parts = []
    for g in range(N_KV):
        hd = w[:, g * HEAD_DIM:(g + 1) * HEAD_DIM]
        parts += [hd, hd]
    return jnp.concatenate(parts, axis=1)


def _fused_in_weights(w):
    a = w[:, 0:G_A]
    q = w[:, G_A:G_A + Q_W]
    k = w[:, G_A + Q_W:G_A + Q_W + KV_W]
    v = w[:, G_A + Q_W + KV_W:G_A + Q_W + 2 * KV_W]
    r_xy = w[:, G_A + Q_W + 2 * KV_W:G_A + Q_W + 2 * KV_W + G_RX + G_RY]
    gates = 0.5 * w[:, G_A + Q_W + 2 * KV_W + G_RX + G_RY:]
    return jnp.concatenate([a, q, _dup_heads(k), _dup_heads(v), r_xy, gates], axis=1).astype(bf16)


def kernel(x, c, ctx, c_ctx, norm1, norm2, w_mod, b_mod, w_in, conv_a, w_conv_out, attn_sink, w_attn_out, rnn_conv_w,
           rnn_conv_b, rnn_w_a, rnn_b_a, rnn_w_x, rnn_b_x, rnn_lam, w_rnn_out, w_o, w_router, w_e_gate, w_e_up,
           w_e_down, final_norm):
    b, n_lat, d = x.shape
    n_ctx = ctx.shape[1]
    depth = w_mod.shape[0]
    assert d == D and n_ctx == TM and n_lat % TM == 0 and b < SUBLANES
    nt = n_lat + n_ctx
    ct = nt // TM - 1

    cstack = jnp.concatenate([c, c_ctx[None, :], jnp.zeros((SUBLANES - b - 1, D), f32)], axis=0)
    mod = _modulation(cstack, w_mod, b_mod)
    tabs = _rope_tables(n_lat, n_ctx)

    xs = x
    acc = None
    modsel_prev = None
    for l in range(depth):
        ctx_out = l < depth - 1
        ctx_sep = ctx if l == 0 else None
        modsel = jnp.stack([mod[l, :b], jnp.broadcast_to(mod[l, b], (b, N_MOD * D))], axis=1)[:, :, None, :]
        outs = _in_proj(xs, ctx_sep, acc, modsel_prev, modsel, norm1[l][None, :], tabs, _fused_in_weights(w_in[l]))
        if acc is not None:
            xs, outs = outs[0], outs[1:]
        za, zq, zk, zv, zrx, zry, zg = outs
        att = _attention(attn_sink[l][None, :], zq, zk, zv, n_lat, (nt if ctx_out else n_lat) // QB)
        hs = []
        for dr in range(2):
            wg = (0.5 * jnp.concatenate([rnn_w_a[l, dr], rnn_w_x[l, dr]], axis=-1)).astype(bf16)
            hs.append(_rnn(zrx, rnn_conv_w[l, dr], rnn_conv_b[l, dr][None, :], wg, rnn_b_a[l, dr][None, :],
                           rnn_b_x[l, dr][None, :], rnn_lam[l, dr][None, :], rev=bool(dr)))
        xs, h2t, aff_t = _merge(xs, ctx_sep, za, att, hs[0], hs[1], zry, zg, modsel, conv_a[l],
                                w_conv_out[l].astype(bf16),
                                w_attn_out[l].astype(bf16), (0.5 * w_rnn_out[l]).astype(bf16),
                                (0.5 * w_o[l]).astype(bf16),
                                norm2[l][None, :], w_router[l].T.astype(bf16), (ct + 1) if ctx_out else ct)
        acc = _moe(aff_t, h2t, w_e_gate, w_e_up, w_e_down, l, n_lat, n_ctx if ctx_out else 0)
        modsel_prev = modsel
    return _final(xs, acc, modsel_prev, final_norm[None, :], n_lat)
```

```python
import functools

import jax
import jax.numpy as jnp
from jax import lax
from jax.experimental import pallas as pl
from jax.experimental.pallas import tpu as pltpu

f32 = jnp.float32
bf16 = jnp.bfloat16
i32 = jnp.int32

D = 1024
EPS = 1e-6
CONV_W = 512
N_HEADS = 8
N_KV = 2
HEAD_DIM = 64
Q_W = N_HEADS * HEAD_DIM
KV_W = N_KV * HEAD_DIM
WINDOW = 128
GRID_W = 64
ROPE_BASE = 10000.0
NEG_INF = -1e30
RNN_W = 1024
RNN_BLOCKS = 8
RNN_BLK = RNN_W // RNN_BLOCKS
RNN_CONV_K = 4
RG_C = 8.0
N_EXPERTS = 16
EXPERT_FF = 1024
CAP_FACTOR = 2
N_MOD = 6

LANES = 128
SUBLANES = 8
BF16_ROWS = 16
VMEM_LIMIT = 56 * 1024 * 1024

TM = 256
QB = 128
K2_W = 2 * KV_W

G_A = 3 * CONV_W
G_QKV = Q_W + 2 * KV_W
G_RX = RNN_W
G_RY = RNN_W
G_G = 3 * D
IN_COLS = (G_A, G_QKV, G_RX, G_RY, G_G)
IN_TOTAL = sum(IN_COLS)
KV2_W = 2 * K2_W
OUT_COLS = (G_A, Q_W, KV2_W, G_RX, G_RY, G_G)


def _cparams(*sem):
    return pltpu.CompilerParams(dimension_semantics=sem, vmem_limit_bytes=VMEM_LIMIT)


def _sigmoid(x):
    return 0.5 * jnp.tanh(0.5 * x) + 0.5


def _rms_mod(x, g, sc, sh):
    y = x * lax.rsqrt(jnp.mean(x * x, axis=-1, keepdims=True) + EPS)
    return (y * g) * (1.0 + sc) + sh


def _mod_kernel(c_ref, w_ref, b_ref, o_ref):
    c = c_ref[...]
    s = (c * _sigmoid(c)).astype(bf16)
    o_ref[...] = jnp.dot(s, w_ref[...].astype(bf16), preferred_element_type=f32) + b_ref[...]


def _modulation(cstack, w_mod, b_mod):
    depth = w_mod.shape[0]
    tn = 1536
    return pl.pallas_call(
        _mod_kernel,
        out_shape=jax.ShapeDtypeStruct((depth, SUBLANES, N_MOD * D), f32),
        grid=(depth, N_MOD * D // tn),
        in_specs=[
            pl.BlockSpec((SUBLANES, D), lambda l, j: (0, 0)),
            pl.BlockSpec((None, D, tn), lambda l, j: (l, 0, j)),
            pl.BlockSpec((None, 1, tn), lambda l, j: (l, 0, j)),
        ],
        out_specs=pl.BlockSpec((None, SUBLANES, tn), lambda l, j: (l, 0, j)),
        compiler_params=_cparams("arbitrary", "arbitrary"),
        name="modulation",
    )(cstack, w_mod, b_mod.reshape(depth, 1, N_MOD * D))


def _tiles_to_rows(ref, rows):
    return jnp.concatenate([ref[pl.ds(c, rows, stride=SUBLANES), :] for c in range(D // LANES)], axis=1)


def _rows_to_tiles(ref, val, rows):
    per = D // LANES
    for r in range(rows // SUBLANES):
        for c in range(per):
            ref[pl.ds(r * SUBLANES * per + c, SUBLANES, stride=per), :] = val[r * SUBLANES:(r + 1) * SUBLANES,
                                                                              c * LANES:(c + 1) * LANES]


def _rope(z, cos, s1, s2):
    outs = []
    for c in range(z.shape[1] // LANES):
        x = z[:, c * LANES:(c + 1) * LANES]
        outs.append(x * cos + pltpu.roll(x, LANES - 16, axis=1) * s1 + pltpu.roll(x, 16, axis=1) * s2)
    return jnp.concatenate(outs, axis=1)


def _inproj_kernel(*refs, with_moe):
    if with_moe:
        x_ref, acc_ref, modp_ref, mod_ref, n_ref, cos_ref, s1_ref, s2_ref, w_ref = refs[:9]
        xo_ref, za_ref, zq_ref, zkv_ref, zrx_ref, zry_ref, zg_ref = refs[9:]
    else:
        x_ref, ctx_ref, mod_ref, n_ref, cos_ref, s1_ref, s2_ref, w_ref = refs[:8]
        za_ref, zq_ref, zkv_ref, zrx_ref, zry_ref, zg_ref = refs[8:]
    if with_moe:
        x = x_ref[...] + modp_ref[:, 5 * D:6 * D] * _tiles_to_rows(acc_ref, TM)
        xo_ref[...] = x
    else:
        x = jnp.where(pl.program_id(1) == pl.num_programs(1) - 1, ctx_ref[...], x_ref[...])
    h = _rms_mod(x, n_ref[...], mod_ref[:, D:2 * D], mod_ref[:, 0:D]).astype(bf16)
    off = 0
    for o_ref, w in zip((za_ref, None, zrx_ref, zry_ref, zg_ref), IN_COLS):
        z = jnp.dot(h, w_ref[:, off:off + w], preferred_element_type=f32)
        off += w
        if o_ref is not None:
            o_ref[...] = z.astype(bf16)
            continue
        qk = _rope(z[:, :Q_W + KV_W], cos_ref[...], s1_ref[...], s2_ref[...])
        zq_ref[...] = (qk[:, :Q_W] * (HEAD_DIM ** -0.5)).astype(bf16)
        lo = lax.broadcasted_iota(i32, (TM, LANES), 1) < HEAD_DIM
        dup = []
        for pair in (qk[:, Q_W:], z[:, Q_W + KV_W:]):
            swapped = pltpu.roll(pair, HEAD_DIM, axis=1)
            dup += [jnp.where(lo, pair, swapped), jnp.where(lo, swapped, pair)]
        zkv_ref[...] = jnp.concatenate(dup, axis=1).astype(bf16)


def _in_proj(xs, ctx, acc, modsel_prev, modsel, norm, tabs, w_all):
    with_moe = acc is not None
    b = xs.shape[0]
    nt = xs.shape[1] + (0 if with_moe else ctx.shape[1])
    n_tiles = nt // TM
    ct = n_tiles - 1
    tile = lambda w: pl.BlockSpec((None, TM, w), lambda i, t: (i, t, 0))
    mod_spec = pl.BlockSpec((None, None, 1, N_MOD * D), lambda i, t: (i, t // ct, 0, 0))
    tab_spec = pl.BlockSpec((TM, LANES), lambda i, t: (t, 0))
    if with_moe:
        in_specs = [tile(D), pl.BlockSpec((None, TM * SUBLANES, LANES), lambda i, t: (i, t, 0)), mod_spec]
        args = [xs, acc, modsel_prev]
    else:
        in_specs = [pl.BlockSpec((None, TM, D), lambda i, t: (i, jnp.minimum(t, ct - 1), 0)),
                    pl.BlockSpec((None, TM, D), lambda i, t: (i, 0, 0))]
        args = [xs, ctx]
    in_specs += [mod_spec, pl.BlockSpec((1, D), lambda i, t: (0, 0)), tab_spec, tab_spec, tab_spec,
                 pl.BlockSpec((D, IN_TOTAL), lambda i, t: (0, 0), pipeline_mode=pl.Buffered(1))]
    args += [modsel, norm, *tabs, w_all]
    out_shape = [jax.ShapeDtypeStruct((b, nt, w), bf16) for w in OUT_COLS]
    out_specs = [tile(w) for w in OUT_COLS]
    if with_moe:
        out_shape = [jax.ShapeDtypeStruct((b, nt, D), f32)] + out_shape
        out_specs = [tile(D)] + out_specs
    return pl.pallas_call(
        functools.partial(_inproj_kernel, with_moe=with_moe),
        out_shape=out_shape, grid=(b, n_tiles), in_specs=in_specs, out_specs=out_specs,
        compiler_params=_cparams("arbitrary", "arbitrary"), name="in_proj",
    )(*args)


def _attn_kernel(sink_ref, q_ref, kvp_ref, kvc_ref, kvn_ref, kvx_ref, o_ref, *, n_lat_blocks, n_ctx):
    j = pl.program_id(1)
    group = N_HEADS // N_KV
    nlb = n_lat_blocks
    lo = lax.broadcasted_iota(i32, (QB, LANES), 1) < HEAD_DIM
    diff = lax.broadcasted_iota(i32, (QB, QB), 1) - lax.broadcasted_iota(i32, (QB, QB), 0)
    lat = j < nlb
    ok = (diff >= jnp.where(lat & (j >= 1), 0, QB),
          diff >= jnp.where(lat, -QB, QB),
          -diff >= jnp.where(lat & (j <= nlb - 2), 0, QB))
    zero = jnp.zeros((QB, LANES), bf16)
    scores, values = [], []
    for g in range(N_KV):
        sl = slice(g * LANES, (g + 1) * LANES)
        vsl = slice(K2_W + g * LANES, K2_W + (g + 1) * LANES)
        k = jnp.concatenate([r[:, sl] for r in (kvp_ref, kvc_ref, kvn_ref, kvx_ref)], axis=0)
        v = jnp.concatenate([r[:, vsl] for r in (kvp_ref, kvc_ref, kvn_ref, kvx_ref)], axis=0)
        parts = []
        for pr in range(group // 2):
            qp = q_ref[:, (g * (group // 2) + pr) * LANES:(g * (group // 2) + pr + 1) * LANES]
            parts += [jnp.where(lo, qp, zero), jnp.where(lo, zero, qp)]
        q4 = jnp.concatenate(parts, axis=0)
        scores.append(lax.dot_general(q4, k, (((1,), (1,)), ((), ())), preferred_element_type=f32))
        values.append(v)
    for g in range(N_KV):
        s, v = scores[g], values[g]
        ps, inv = [], []
        for hh in range(group):
            sh = s[hh * QB:(hh + 1) * QB]
            sm = jnp.concatenate([jnp.where(ok[kb], sh[:, kb * QB:(kb + 1) * QB], NEG_INF) for kb in range(3)]
                                 + [sh[:, 3 * QB:]], axis=1)
            sink = sink_ref[0, g * group + hh]
            m = jnp.maximum(jnp.max(sm, axis=1, keepdims=True), sink)
            p = jnp.exp(sm - m)
            inv.append(1.0 / (jnp.sum(p, axis=1, keepdims=True) + jnp.exp(sink - m)))
            ps.append(p.astype(bf16))
        o = jnp.dot(jnp.concatenate(ps, axis=0), v, preferred_element_type=f32) * jnp.concatenate(inv, axis=0)
        for pr in range(group // 2):
            pair = jnp.where(lo, o[(2 * pr) * QB:(2 * pr + 1) * QB], o[(2 * pr + 1) * QB:(2 * pr + 2) * QB])
            c0 = (g * (group // 2) + pr) * LANES
            o_ref[:, c0:c0 + LANES] = pair.astype(bf16)


def _attention(sink, zq, zkv, n_lat, n_qblocks):
    b, nt, _ = zq.shape
    n_ctx = nt - n_lat
    nlb = n_lat // QB
    cidx = n_lat // n_ctx
    win = lambda d: pl.BlockSpec((None, QB, KV2_W), lambda i, j: (i, jnp.clip(j + d, 0, nlb - 1), 0))
    ctx = pl.BlockSpec((None, n_ctx, KV2_W), lambda i, j: (i, cidx, 0))
    return pl.pallas_call(
        functools.partial(_attn_kernel, n_lat_blocks=nlb, n_ctx=n_ctx),
        out_shape=jax.ShapeDtypeStruct((b, n_qblocks * QB, Q_W), bf16),
        grid=(b, n_qblocks),
        in_specs=[pl.BlockSpec(memory_space=pltpu.SMEM),
                  pl.BlockSpec((None, QB, Q_W), lambda i, j: (i, j, 0)),
                  win(-1), win(0), win(1), ctx],
        out_specs=pl.BlockSpec((None, QB, Q_W), lambda i, j: (i, j, 0)),
        compiler_params=_cparams("arbitrary", "arbitrary"), name="attention",
    )(sink, zq, zkv, zkv, zkv, zkv)


def _rnn_chunk(s, ct, rev):
    return jnp.where(s == 0, ct, (ct - s) if rev else (s - 1))


def _block_scan(a, b, h_in, rowi, rev):
    n = SUBLANES
    order = range(n - 1, -1, -1) if rev else range(n)
    hs = [None] * n
    ps = [None] * n
    prev = None
    for j in order:
        if prev is None:
            hs[j], ps[j] = b[j], a[j]
        else:
            hs[j], ps[j] = a[j] * hs[prev] + b[j], a[j] * ps[prev]
        prev = j
    he, pe = hs[prev], ps[prev]
    for sh in (1, 2, 4):
        rs = n - sh if rev else sh
        msk = (rowi < n - sh) if rev else (rowi >= sh)
        he, pe = (jnp.where(msk, he + pe * pltpu.roll(he, rs, axis=0), he),
                  jnp.where(msk, pe * pltpu.roll(pe, rs, axis=0), pe))
    e = he + pe * h_in
    carry = jnp.where((rowi == n - 1) if rev else (rowi == 0), h_in, pltpu.roll(e, n - 1 if rev else 1, axis=0))
    out = [hs[j] + ps[j] * carry for j in range(n)]
    last = e[0:1, :] if rev else e[n - 1:n, :]
    return out, jnp.broadcast_to(last, (n, LANES))


def _rnn_kernel(u_ref, halo_ref, cw_ref, cb_ref, wg_ref, ba_ref, bx_ref, lam_ref, o_ref, ubuf, obuf, hcar, *,
                rev, ct):
    s = pl.program_id(1)
    chunk = _rnn_chunk(s, ct, rev)

    @pl.when(s == 0)
    def _():
        hcar[...] = jnp.zeros_like(hcar)

    u = u_ref[...].astype(f32)
    if rev:
        halo_ok = (chunk != ct - 1) & (chunk != ct)
        base = 0
    else:
        halo_ok = (chunk != 0) & (chunk != ct)
        base = BF16_ROWS - (RNN_CONV_K - 1)
    hal = jnp.where(halo_ok, halo_ref[...].astype(f32), 0.0)
    n_lt = RNN_W // LANES
    for c in range(n_lt):
        sl = slice(c * LANES, (c + 1) * LANES)
        if rev:
            ubuf[c, 0:TM, :] = u[:, sl]
            ubuf[c, TM:TM + BF16_ROWS, :] = hal[:, sl]
        else:
            ubuf[c, 0:BF16_ROWS, :] = hal[:, sl]
            ubuf[c, BF16_ROWS:BF16_ROWS + TM, :] = u[:, sl]
    sp = jnp.logaddexp(-lam_ref[...], 0.0)
    rowi = lax.broadcasted_iota(i32, (SUBLANES, LANES), 0)
    blk_rows = SUBLANES * SUBLANES
    n_blk = TM // blk_rows
    for c in range(n_lt):
        sl = slice(c * LANES, (c + 1) * LANES)
        taps = [cw_ref[k:k + 1, sl] for k in range(RNN_CONV_K)]
        xt = []
        for blk in range(n_blk):
            for j in range(SUBLANES):
                r0 = base + blk * blk_rows + j
                acc = cb_ref[:, sl] + ubuf[c, pl.ds(r0, SUBLANES, stride=SUBLANES), :] * taps[0]
                for k in range(1, RNN_CONV_K):
                    acc = acc + ubuf[c, pl.ds(r0 + k, SUBLANES, stride=SUBLANES), :] * taps[k]
                xt.append(acc)
        xp = jnp.concatenate(xt, axis=0)
        g = jnp.dot(xp.astype(bf16), wg_ref[c], preferred_element_type=f32)
        t_r = jnp.tanh(g[:, :RNN_BLK] + 0.5 * ba_ref[:, sl])
        t_i = jnp.tanh(g[:, RNN_BLK:] + 0.5 * bx_ref[:, sl])
        c4 = (-0.5 * RG_C) * sp[:, sl]
        log_a = c4 * t_r + c4
        a = jnp.exp(log_a)
        y = -jnp.tanh(log_a) * (a * a + 1.0)
        xh = 0.5 * xp
        bb = jnp.where(y > 0.0, y * lax.rsqrt(y), 0.0) * (xh * t_i + xh)
        h_in = hcar[:, sl]
        for blk in (range(n_blk - 1, -1, -1) if rev else range(n_blk)):
            rows = lambda j: slice((blk * SUBLANES + j) * SUBLANES, (blk * SUBLANES + j + 1) * SUBLANES)
            hs, h_in = _block_scan([a[rows(j)] for j in range(SUBLANES)], [bb[rows(j)] for j in range(SUBLANES)],
                                   h_in, rowi, rev)
            for j in range(SUBLANES):
                obuf[c, pl.ds(blk * blk_rows + j, SUBLANES, stride=SUBLANES), :] = hs[j]
        hcar[:, sl] = h_in
    o_ref[...] = jnp.concatenate([obuf[c] for c in range(n_lt)], axis=1).astype(bf16)


def _rnn(zrx, cw, cb, wg, ba, bx, lam, rev):
    b, nt, _ = zrx.shape
    n_tiles = nt // TM
    ct = n_tiles - 1
    per = TM // BF16_ROWS
    if rev:
        halo_map = lambda i, s: (i, jnp.minimum((_rnn_chunk(s, ct, rev) + 1) * per, nt // BF16_ROWS - 1), 0)
    else:
        halo_map = lambda i, s: (i, jnp.maximum(_rnn_chunk(s, ct, rev) * per - 1, 0), 0)
    full = lambda shape: pl.BlockSpec(shape, lambda i, s: (0,) * len(shape))
    return pl.pallas_call(
        functools.partial(_rnn_kernel, rev=rev, ct=ct),
        out_shape=jax.ShapeDtypeStruct((b, nt, RNN_W), bf16),
        grid=(b, n_tiles),
        in_specs=[pl.BlockSpec((None, TM, RNN_W), lambda i, s: (i, _rnn_chunk(s, ct, rev), 0)),
                  pl.BlockSpec((None, BF16_ROWS, RNN_W), halo_map),
                  full((RNN_CONV_K, RNN_W)), full((1, RNN_W)), full((RNN_BLOCKS, RNN_BLK, 2 * RNN_BLK)),
                  full((1, RNN_W)), full((1, RNN_W)), full((1, RNN_W))],
        out_specs=pl.BlockSpec((None, TM, RNN_W), lambda i, s: (i, _rnn_chunk(s, ct, rev), 0)),
        scratch_shapes=[pltpu.VMEM((RNN_W // LANES, TM + BF16_ROWS, LANES), f32),
                        pltpu.VMEM((RNN_W // LANES, TM, LANES), f32), pltpu.VMEM((SUBLANES, RNN_W), f32)],
        compiler_params=_cparams("arbitrary", "arbitrary"), name="rnn_bwd" if rev else "rnn_fwd",
    )(zrx, zrx, cw, cb, wg, ba, bx, lam)


def _merge_kernel(*refs, ct, split):
    (za_ref, zap_ref, zan_ref, att_ref, hf_ref, hb_ref, ry_ref, zg_ref, mod_ref, ca_ref, wc_ref,
     wa_ref, wr_ref, wo_ref, n2_ref, wrt_ref, xo_ref, h2_ref, aff_ref) = refs[2 if split else 1:]
    t = pl.program_id(1)
    if split:
        x_in = jnp.where(t == ct, refs[1][...], refs[0][...])
    else:
        x_in = refs[0][...]
    cw = CONV_W
    za = za_ref[...].astype(f32)
    cu = za[:, 2 * cw:3 * cw] * za[:, 0:cw]
    zp = zap_ref[BF16_ROWS - 1:BF16_ROWS, :].astype(f32)
    zn = zan_ref[0:1, :].astype(f32)
    prev_ok = (t != 0) & (t != ct)
    next_ok = (t != ct - 1) & (t != ct)
    cu_p = jnp.where(prev_ok, zp[:, 2 * cw:3 * cw] * zp[:, 0:cw], 0.0)
    cu_n = jnp.where(next_ok, zn[:, 2 * cw:3 * cw] * zn[:, 0:cw], 0.0)
    rowi = lax.broadcasted_iota(i32, (TM, cw), 0)
    cu_prev = jnp.where(rowi == 0, cu_p, pltpu.roll(cu, 1, axis=0))
    cu_next = jnp.where(rowi == TM - 1, cu_n, pltpu.roll(cu, TM - 1, axis=0))
    y = cu_prev * ca_ref[0:1, :] + cu * ca_ref[1:2, :] + cu_next * ca_ref[2:3, :]
    cnv = jnp.dot((za[:, cw:2 * cw] * y).astype(bf16), wc_ref[...], preferred_element_type=f32)
    att = jnp.dot(att_ref[...], wa_ref[...], preferred_element_type=f32)
    ry = ry_ref[...].astype(f32)
    t_y = jnp.tanh(ry * (0.7978845608028654 + (0.7978845608028654 * 0.044715) * (ry * ry)))
    gelu2 = ry * t_y + ry
    rec = hf_ref[...].astype(f32) + hb_ref[...].astype(f32)
    rnn = jnp.dot((gelu2 * rec).astype(bf16), wr_ref[...], preferred_element_type=f32)
    mix2 = None
    for k, br in enumerate((cnv, att, rnn)):
        term = jnp.tanh(zg_ref[:, k * D:(k + 1) * D].astype(f32)) * br + br
        mix2 = term if mix2 is None else mix2 + term
    x = x_in + mod_ref[:, 2 * D:3 * D] * jnp.dot(mix2.astype(bf16), wo_ref[...], preferred_element_type=f32)
    xo_ref[...] = x
    h2 = _rms_mod(x, n2_ref[...], mod_ref[:, 4 * D:5 * D], mod_ref[:, 3 * D:4 * D])
    _rows_to_tiles(h2_ref, h2, TM)
    logits = lax.dot_general(wrt_ref[...], h2.astype(bf16), (((1,), (1,)), ((), ())), preferred_element_type=f32)
    e = jnp.exp(logits - jnp.max(logits, axis=0, keepdims=True))
    aff_ref[...] = e / jnp.sum(e, axis=0, keepdims=True)


def _merge(xs, ctx, za, att, hf, hb, zry, zg, modsel, conv_a, wc, wa, wr, wo, norm2, wrt, n_tiles_eff):
    b, nt, _ = za.shape
    ct = nt // TM - 1
    per = TM // BF16_ROWS
    split = ctx is not None
    tile = lambda w: pl.BlockSpec((None, TM, w), lambda i, t: (i, t, 0))
    full = lambda shape: pl.BlockSpec(shape, lambda i, t: (0,) * len(shape), pipeline_mode=pl.Buffered(1))
    if split:
        x_specs = [pl.BlockSpec((None, TM, D), lambda i, t: (i, jnp.minimum(t, ct - 1), 0)),
                   pl.BlockSpec((None, TM, D), lambda i, t: (i, 0, 0))]
        x_args = [xs, ctx]
    else:
        x_specs = [tile(D)]
        x_args = [xs]
    return pl.pallas_call(
        functools.partial(_merge_kernel, ct=ct, split=split),
        out_shape=[jax.ShapeDtypeStruct((b, n_tiles_eff * TM, D), f32),
                   jax.ShapeDtypeStruct((b, n_tiles_eff * TM * SUBLANES, LANES), f32),
                   jax.ShapeDtypeStruct((b, N_EXPERTS, n_tiles_eff * TM), f32)],
        grid=(b, n_tiles_eff),
        in_specs=x_specs + [tile(G_A),
                  pl.BlockSpec((None, BF16_ROWS, G_A), lambda i, t: (i, jnp.maximum(t * per - 1, 0), 0)),
                  pl.BlockSpec((None, BF16_ROWS, G_A),
                               lambda i, t: (i, jnp.minimum((t + 1) * per, nt // BF16_ROWS - 1), 0)),
                  tile(Q_W), tile(RNN_W), tile(RNN_W), tile(RNN_W), tile(G_G),
                  pl.BlockSpec((None, None, 1, N_MOD * D), lambda i, t: (i, t // ct, 0, 0)),
                  full((3, CONV_W)), full((CONV_W, D)), full((Q_W, D)), full((RNN_W, D)), full((D, D)),
                  full((1, D)), full((N_EXPERTS, D))],
        out_specs=[tile(D), pl.BlockSpec((None, TM * SUBLANES, LANES), lambda i, t: (i, t, 0)),
                   pl.BlockSpec((None, N_EXPERTS, TM), lambda i, t: (i, 0, t))],
        compiler_params=_cparams("arbitrary", "arbitrary"), name="merge",
    )(*x_args, za, za, za, att, hf, hb, zry, zg, modsel, conv_a, wc, wa, wr, wo, norm2, wrt)


ROUTE_QW = 64
ROUTE_TOK_SHIFT = 6
CUM_BLK = 256


def _cumsum_lanes(x, n):
    blk = min(CUM_BLK, n)
    tri = (lax.broadcasted_iota(i32, (blk, blk), 0) <= lax.broadcasted_iota(i32, (blk, blk), 1)).astype(bf16)
    carry = jnp.zeros((x.shape[0], 1), f32)
    outs = []
    for j in range(n // blk):
        c = jnp.dot(x[:, j * blk:(j + 1) * blk].astype(bf16), tri, preferred_element_type=f32) + carry
        outs.append(c)
        carry = c[:, blk - 1:blk]
    return jnp.concatenate(outs, axis=1)


def _route_kernel(aff_ref, idx_ref, val_ref, pos_s, *, n, cap):
    aff = aff_ref[...]
    thr = jnp.zeros((N_EXPERTS, 1), i32)
    for bit in range(30, -1, -1):
        cand = thr | (1 << bit)
        cnt = jnp.sum(jnp.where(aff >= lax.bitcast_convert_type(cand, f32), 1.0, 0.0), axis=1, keepdims=True)
        thr = jnp.where(cnt >= float(cap), cand, thr)
    thr_f = lax.bitcast_convert_type(thr, f32)
    gt = aff > thr_f
    eq = aff == thr_f
    need = float(cap) - jnp.sum(jnp.where(gt, 1.0, 0.0), axis=1, keepdims=True)
    sel = gt | (eq & (_cumsum_lanes(eq.astype(f32), n) <= need))
    pos_s[...] = jnp.where(sel, _cumsum_lanes(sel.astype(f32), n).astype(i32) - 1, -1)

    qw = min(ROUTE_QW, cap)
    shift = qw.bit_length() - 1
    tok = lax.broadcasted_iota(i32, (1, n), 1)
    tok_hi = (tok >> ROUTE_TOK_SHIFT).astype(f32)
    tok_lo = (tok & ((1 << ROUTE_TOK_SHIFT) - 1)).astype(f32)
    hi_iota = lax.broadcasted_iota(i32, (SUBLANES, n), 0)
    lo_iota = lax.broadcasted_iota(i32, (qw, n), 0)

    def body(e, carry):
        pos = pos_s[pl.ds(e, 1), :]
        a0 = aff_ref[pl.ds(e, 1), :]
        a_h = a0.astype(bf16).astype(f32)
        a_m = (a0 - a_h).astype(bf16).astype(f32)
        a_l = ((a0 - a_h) - a_m).astype(bf16).astype(f32)
        in_hi = (pos >> shift) == hi_iota
        lhs = jnp.concatenate([jnp.where(in_hi, r, 0.0) for r in (tok_hi, tok_lo, a_h, a_m, a_l)], axis=0)
        onehot = jnp.where((pos & (qw - 1)) == lo_iota, 1.0, 0.0).astype(bf16)
        res = lax.dot_general(lhs.astype(bf16), onehot, (((1,), (1,)), ((), ())), preferred_element_type=f32)
        r_hi, r_lo, v_h, v_m, v_l = (res[k * SUBLANES:(k + 1) * SUBLANES] for k in range(5))
        idx_ref[e] = (r_hi * float(1 << ROUTE_TOK_SHIFT) + r_lo).astype(i32)[:cap // qw]
        val_ref[e] = ((v_h + v_m) + v_l)[:cap // qw]
        return carry

    lax.fori_loop(0, N_EXPERTS, body, 0)


def _route(aff_t, n, cap, lane_block):
    b = aff_t.shape[0]
    qw = min(ROUTE_QW, cap)
    assert cap % qw == 0 and cap // qw <= SUBLANES and qw & (qw - 1) == 0
    idx, val = pl.pallas_call(
        functools.partial(_route_kernel, n=n, cap=cap),
        out_shape=[jax.ShapeDtypeStruct((b, N_EXPERTS, cap // qw, qw), i32),
                   jax.ShapeDtypeStruct((b, N_EXPERTS, cap // qw, qw), f32)],
        grid=(b,),
        in_specs=[pl.BlockSpec((None, N_EXPERTS, n), lambda i: (i, 0, lane_block))],
        out_specs=[pl.BlockSpec((None, N_EXPERTS, cap // qw, qw), lambda i: (i, 0, 0, 0)),
                   pl.BlockSpec((None, N_EXPERTS, cap // qw, qw), lambda i: (i, 0, 0, 0))],
        scratch_shapes=[pltpu.VMEM((N_EXPERTS, n), i32)],
        compiler_params=_cparams("arbitrary"), name="route",
    )(aff_t)
    return idx.reshape(b, N_EXPERTS, cap), val.reshape(b, N_EXPERTS, cap)


def _gather_kernel(idx_ref, h_ref, tok_ref, tbuf, *, cap):
    base = (pl.program_id(0) * N_EXPERTS + pl.program_id(1)) * cap
    sp = cap + SUBLANES
    for p in range(cap):
        i = idx_ref[base + p]
        tbuf[pl.ds(p, SUBLANES, stride=sp), :] = h_ref[pl.ds(pl.multiple_of(i * SUBLANES, SUBLANES), SUBLANES), :]
    tok_ref[...] = jnp.concatenate([tbuf[pl.ds(c * sp, cap), :] for c in range(D // LANES)], axis=1).astype(bf16)


def _gather(idx, h2t, cap):
    b, rows, _ = h2t.shape
    return pl.pallas_call(
        functools.partial(_gather_kernel, cap=cap),
        out_shape=jax.ShapeDtypeStruct((b, N_EXPERTS, cap, D), bf16),
        grid_spec=pltpu.PrefetchScalarGridSpec(
            num_scalar_prefetch=1, grid=(b, N_EXPERTS),
            in_specs=[pl.BlockSpec((None, rows, LANES), lambda i, e, idx: (i, 0, 0),
                                   pipeline_mode=pl.Buffered(1))],
            out_specs=pl.BlockSpec((None, None, cap, D), lambda i, e, idx: (i, e, 0, 0)),
            scratch_shapes=[pltpu.VMEM((SUBLANES * (cap + SUBLANES), LANES), f32)]),
        compiler_params=_cparams("arbitrary", "arbitrary"), name="moe_gather",
    )(idx, h2t)


def _cast3_kernel(a_ref, b_ref, c_ref, ao_ref, bo_ref, co_ref):
    ao_ref[...] = a_ref[...].astype(bf16)
    bo_ref[...] = b_ref[...].astype(bf16)
    co_ref[...] = c_ref[...].astype(bf16)


def _expert0_bf16(wg, wu, wd, layer):
    spec = pl.BlockSpec((None, None, D, EXPERT_FF), lambda i: (layer, 0, 0, 0))
    ospec = pl.BlockSpec((D, EXPERT_FF), lambda i: (0, 0))
    return pl.pallas_call(
        _cast3_kernel, out_shape=[jax.ShapeDtypeStruct((D, EXPERT_FF), bf16)] * 3, grid=(1,),
        in_specs=[spec] * 3, out_specs=[ospec] * 3, compiler_params=_cparams("arbitrary"), name="expert0_cast",
    )(wg, wu, wd)


def _ffn_kernel(tok_ref, g0_ref, u0_ref, d0_ref, gq_ref, uq_ref, dq_ref, o_ref, w_even, w_odd):
    e = pl.program_id(0)
    i = pl.program_id(1)

    @pl.when((e == 0) & (i == 0))
    def _():
        w_even[0] = g0_ref[...]
        w_even[1] = u0_ref[...]
        w_even[2] = d0_ref[...]

    rows = gq_ref.shape[0]
    r0 = pl.multiple_of(i * rows, rows)

    def step(use, fill):
        t = tok_ref[...]
        g = jnp.dot(t, use[0], preferred_element_type=f32)
        for k, q_ref in enumerate((gq_ref, uq_ref, dq_ref)):
            fill[k, pl.ds(r0, rows), :] = q_ref[...].astype(bf16)
        u = jnp.dot(t, use[1], preferred_element_type=f32)
        a = (g * _sigmoid(g) * u).astype(bf16)
        o_ref[...] = jnp.dot(a, use[2], preferred_element_type=f32)

    @pl.when(e % 2 == 0)
    def _():
        step(w_even, w_odd)

    @pl.when(e % 2 == 1)
    def _():
        step(w_odd, w_even)


def _ffn(tok, wg, wu, wd, layer):
    b, _, cap, _ = tok.shape
    assert D == EXPERT_FF and D % b == 0 and (D // b) % BF16_ROWS == 0
    rows = D // b
    w0 = _expert0_bf16(wg, wu, wd, layer)
    w0spec = pl.BlockSpec((D, EXPERT_FF), lambda e, i: (0, 0), pipeline_mode=pl.Buffered(1))
    qspec = pl.BlockSpec((None, None, rows, D), lambda e, i: (layer, jnp.minimum(e + 1, N_EXPERTS - 1), i, 0))
    return pl.pallas_call(
        _ffn_kernel,
        out_shape=jax.ShapeDtypeStruct((b, N_EXPERTS, cap, D), f32),
        grid=(N_EXPERTS, b),
        in_specs=[pl.BlockSpec((None, None, cap, D), lambda e, i: (i, e, 0, 0)),
                  w0spec, w0spec, w0spec, qspec, qspec, qspec],
        out_specs=pl.BlockSpec((None, None, cap, D), lambda e, i: (i, e, 0, 0)),
        scratch_shapes=[pltpu.VMEM((3, D, EXPERT_FF), bf16), pltpu.VMEM((3, D, EXPERT_FF), bf16)],
        compiler_params=_cparams("arbitrary", "arbitrary"), name="moe_ffn",
    )(tok, *w0, wg, wu, wd)


SCATTER_UNROLL = 8


def _scatter_kernel(idx_ref, val_ref, o_ref, acc_ref, tbuf, *, cap):
    e = pl.program_id(1)
    base = (pl.program_id(0) * N_EXPERTS + e) * cap
    sp = cap + SUBLANES

    @pl.when(e == 0)
    def _():
        acc_ref[...] = jnp.zeros_like(acc_ref)

    for c in range(D // LANES):
        tbuf[pl.ds(c * sp, cap), :] = o_ref[:, c * LANES:(c + 1) * LANES]
    u = min(SCATTER_UNROLL, cap)
    for p0 in range(0, cap, u):
        pend = []
        for p in range(p0, p0 + u):
            r0 = pl.multiple_of(idx_ref[base + p] * SUBLANES, SUBLANES)
            pend.append((r0, acc_ref[pl.ds(r0, SUBLANES), :] + tbuf[pl.ds(p, SUBLANES, stride=sp), :] * val_ref[base + p]))
        for r0, v in pend:
            acc_ref[pl.ds(r0, SUBLANES), :] = v


def _scatter(idx, val, o, rows):
    b, _, cap, _ = o.shape
    return pl.pallas_call(
        functools.partial(_scatter_kernel, cap=cap),
        out_shape=jax.ShapeDtypeStruct((b, rows, LANES), f32),
        grid_spec=pltpu.PrefetchScalarGridSpec(
            num_scalar_prefetch=2, grid=(b, N_EXPERTS),
            in_specs=[pl.BlockSpec((None, None, cap, D), lambda i, e, idx, val: (i, e, 0, 0))],
            out_specs=pl.BlockSpec((None, rows, LANES), lambda i, e, idx, val: (i, 0, 0)),
            scratch_shapes=[pltpu.VMEM((SUBLANES * (cap + SUBLANES), LANES), f32)]),
        compiler_params=_cparams("arbitrary", "arbitrary"), name="moe_scatter",
    )(idx, val, o)


def _moe(aff_t, h2t, wg, wu, wd, layer, n_lat, n_ctx):
    idx, val = _route(aff_t, n_lat, CAP_FACTOR * n_lat // N_EXPERTS, 0)
    if n_ctx:
        idx_c, val_c = _route(aff_t, n_ctx, CAP_FACTOR * n_ctx // N_EXPERTS, n_lat // n_ctx)
        idx = jnp.concatenate([idx, idx_c + n_lat], axis=-1)
        val = jnp.concatenate([val, val_c], axis=-1)
    cap = idx.shape[-1]
    idx = idx.reshape(-1)
    val = val.reshape(-1)
    tok = _gather(idx, h2t, cap)
    o = _ffn(tok, wg, wu, wd, layer)
    return _scatter(idx, val, o, h2t.shape[1])


def _final_kernel(x_ref, acc_ref, mod_ref, n_ref, o_ref):
    x = x_ref[...] + mod_ref[:, 5 * D:6 * D] * _tiles_to_rows(acc_ref, TM)
    o_ref[...] = (x * lax.rsqrt(jnp.mean(x * x, axis=-1, keepdims=True) + EPS)) * n_ref[...]


def _final(xs, acc, modsel, norm, n_lat):
    b = xs.shape[0]
    return pl.pallas_call(
        _final_kernel,
        out_shape=jax.ShapeDtypeStruct((b, n_lat, D), f32),
        grid=(b, n_lat // TM),
        in_specs=[pl.BlockSpec((None, TM, D), lambda i, t: (i, t, 0)),
                  pl.BlockSpec((None, TM * SUBLANES, LANES), lambda i, t: (i, t, 0)),
                  pl.BlockSpec((None, None, 1, N_MOD * D), lambda i, t: (i, 0, 0, 0)),
                  pl.BlockSpec((1, D), lambda i, t: (0, 0))],
        out_specs=pl.BlockSpec((None, TM, D), lambda i, t: (i, t, 0)),
        compiler_params=_cparams("arbitrary", "arbitrary"), name="final_norm",
    )(xs, acc, modsel, norm)


def _rope_tables(n_lat, n_ctx):
    pos = jnp.arange(n_lat, dtype=f32)
    rows = jnp.floor(pos / GRID_W)
    cols = pos - rows * GRID_W
    half = HEAD_DIM // 4
    inv = 1.0 / (ROPE_BASE ** (jnp.arange(0, 2 * half, 2, dtype=f32) / (2 * half)))
    lane = jnp.arange(LANES)
    d = lane % HEAD_DIM
    ang = jnp.where((d < 2 * half)[None, :], rows[:, None], cols[:, None]) * inv[d % half][None, :]
    first = ((d % (2 * half)) < half)[None, :]
    cos = jnp.cos(ang)
    sin = jnp.sin(ang)
    s1 = jnp.where(first, -sin, 0.0)
    s2 = jnp.where(first, 0.0, sin)
    pad = lambda t, v: jnp.concatenate([t, jnp.full((n_ctx, LANES), v, f32)], axis=0)
    return pad(cos, 1.0), pad(s1, 0.0), pad(s2, 0.0)


def _in_weights_bf16(w):
    scale = jnp.concatenate([jnp.ones((IN_TOTAL - G_G,), f32), jnp.full((G_G,), 0.5, f32)])
    return (w * scale[None, :]).astype(bf16)


def kernel(x, c, ctx, c_ctx, norm1, norm2, w_mod, b_mod, w_in, conv_a, w_conv_out, attn_sink, w_attn_out, rnn_conv_w,
           rnn_conv_b, rnn_w_a, rnn_b_a, rnn_w_x, rnn_b_x, rnn_lam, w_rnn_out, w_o, w_router, w_e_gate, w_e_up,
           w_e_down, final_norm):
    b, n_lat, d = x.shape
    n_ctx = ctx.shape[1]
    depth = w_mod.shape[0]
    assert d == D and n_ctx == TM and n_lat % TM == 0 and b < SUBLANES
    nt = n_lat + n_ctx
    ct = nt // TM - 1

    cstack = jnp.concatenate([c, c_ctx[None, :], jnp.zeros((SUBLANES - b - 1, D), f32)], axis=0)
    mod = _modulation(cstack, w_mod, b_mod)
    tabs = _rope_tables(n_lat, n_ctx)

    xs = x
    acc = None
    modsel_prev = None
    for l in range(depth):
        ctx_out = l < depth - 1
        ctx_sep = ctx if l == 0 else None
        modsel = jnp.stack([mod[l, :b], jnp.broadcast_to(mod[l, b], (b, N_MOD * D))], axis=1)[:, :, None, :]
        outs = _in_proj(xs, ctx_sep, acc, modsel_prev, modsel, norm1[l][None, :], tabs, _in_weights_bf16(w_in[l]))
        if acc is not None:
            xs, outs = outs[0], outs[1:]
        za, zq, zkv, zrx, zry, zg = outs
        att = _attention(attn_sink[l][None, :], zq, zkv, n_lat, (nt if ctx_out else n_lat) // QB)
        hs = []
        for dr in range(2):
            wg = (0.5 * jnp.concatenate([rnn_w_a[l, dr], rnn_w_x[l, dr]], axis=-1)).astype(bf16)
            hs.append(_rnn(zrx, rnn_conv_w[l, dr], rnn_conv_b[l, dr][None, :], wg, rnn_b_a[l, dr][None, :],
                           rnn_b_x[l, dr][None, :], rnn_lam[l, dr][None, :], rev=bool(dr)))
        xs, h2t, aff_t = _merge(xs, ctx_sep, za, att, hs[0], hs[1], zry, zg, modsel, conv_a[l],
                                w_conv_out[l].astype(bf16),
                                w_attn_out[l].astype(bf16), (0.5 * w_rnn_out[l]).astype(bf16),
                                (0.5 * w_o[l]).astype(bf16),
                                norm2[l][None, :], w_router[l].T.astype(bf16), (ct + 1) if ctx_out else ct)
        acc = _moe(aff_t, h2t, w_e_gate, w_e_up, w_e_down, l, n_lat, n_ctx if ctx_out else 0)
        modsel_prev = modsel
    return _final(xs, acc, modsel_prev, final_norm[None, :], n_lat)
```

```python
import functools

import jax
import jax.numpy as jnp
from jax import lax
from jax.experimental import pallas as pl
from jax.experimental.pallas import tpu as pltpu

f32 = jnp.float32
bf16 = jnp.bfloat16
i32 = jnp.int32

D = 1024
EPS = 1e-6
CONV_W = 512
N_HEADS = 8
N_KV = 2
HEAD_DIM = 64
Q_W = N_HEADS * HEAD_DIM
KV_W = N_KV * HEAD_DIM
WINDOW = 128
GRID_W = 64
ROPE_BASE = 10000.0
NEG_INF = -1e30
RNN_W = 1024
RNN_BLOCKS = 8
RNN_BLK = RNN_W // RNN_BLOCKS
RNN_CONV_K = 4
RG_C = 8.0
N_EXPERTS = 16
EXPERT_FF = 1024
CAP_FACTOR = 2
N_MOD = 6

LANES = 128
SUBLANES = 8
BF16_ROWS = 16
VMEM_LIMIT = 56 * 1024 * 1024

TM = 256
QB = 128
K2_W = 2 * KV_W

G_A = 3 * CONV_W
G_QKV = Q_W + 2 * KV_W
G_RX = RNN_W
G_RY = RNN_W
G_G = 3 * D
IN_COLS = (G_A, G_QKV, G_RX, G_RY, G_G)
IN_TOTAL = sum(IN_COLS)
KV2_W = 2 * K2_W
ZA_W = 2 * CONV_W
OUT_COLS = (ZA_W, Q_W, KV2_W, G_RX, G_RY, G_G)


def _cparams(*sem):
    return pltpu.CompilerParams(dimension_semantics=sem, vmem_limit_bytes=VMEM_LIMIT)


def _sigmoid(x):
    return 0.5 * jnp.tanh(0.5 * x) + 0.5


def _rms_mod(x, g, sc, sh):
    y = x * lax.rsqrt(jnp.mean(x * x, axis=-1, keepdims=True) + EPS)
    return (y * g) * (1.0 + sc) + sh


def _mod_kernel(c_ref, w_ref, b_ref, o_ref):
    c = c_ref[...]
    s = (c * _sigmoid(c)).astype(bf16)
    o_ref[...] = jnp.dot(s, w_ref[...].astype(bf16), preferred_element_type=f32) + b_ref[...]


def _modulation(cstack, w_mod, b_mod):
    depth = w_mod.shape[0]
    tn = 1536
    return pl.pallas_call(
        _mod_kernel,
        out_shape=jax.ShapeDtypeStruct((depth, SUBLANES, N_MOD * D), f32),
        grid=(depth, N_MOD * D // tn),
        in_specs=[
            pl.BlockSpec((SUBLANES, D), lambda l, j: (0, 0)),
            pl.BlockSpec((None, D, tn), lambda l, j: (l, 0, j)),
            pl.BlockSpec((None, 1, tn), lambda l, j: (l, 0, j)),
        ],
        out_specs=pl.BlockSpec((None, SUBLANES, tn), lambda l, j: (l, 0, j)),
        compiler_params=_cparams("arbitrary", "arbitrary"),
        name="modulation",
    )(cstack, w_mod, b_mod.reshape(depth, 1, N_MOD * D))


def _tiles_to_rows(ref, rows):
    return jnp.concatenate([ref[pl.ds(c, rows, stride=SUBLANES), :] for c in range(D // LANES)], axis=1)


def _rows_to_tiles(ref, val, rows):
    per = D // LANES
    for r in range(rows // SUBLANES):
        for c in range(per):
            ref[pl.ds(r * SUBLANES * per + c, SUBLANES, stride=per), :] = val[r * SUBLANES:(r + 1) * SUBLANES,
                                                                              c * LANES:(c + 1) * LANES]


def _rope(z, cos, s1, s2):
    outs = []
    for c in range(z.shape[1] // LANES):
        x = z[:, c * LANES:(c + 1) * LANES]
        outs.append(x * cos + pltpu.roll(x, LANES - 16, axis=1) * s1 + pltpu.roll(x, 16, axis=1) * s2)
    return jnp.concatenate(outs, axis=1)


def _inproj_kernel(*refs, with_moe):
    if with_moe:
        x_ref, acc_ref, modp_ref, mod_ref, n_ref, cos_ref, s1_ref, s2_ref, w_ref = refs[:9]
        xo_ref, za_ref, zq_ref, zkv_ref, zrx_ref, zry_ref, zg_ref = refs[9:]
    else:
        x_ref, ctx_ref, mod_ref, n_ref, cos_ref, s1_ref, s2_ref, w_ref = refs[:8]
        za_ref, zq_ref, zkv_ref, zrx_ref, zry_ref, zg_ref = refs[8:]
    if with_moe:
        x = x_ref[...] + modp_ref[:, 5 * D:6 * D] * _tiles_to_rows(acc_ref, TM)
        xo_ref[...] = x
    else:
        x = jnp.where(pl.program_id(1) == pl.num_programs(1) - 1, ctx_ref[...], x_ref[...])
    h = _rms_mod(x, n_ref[...], mod_ref[:, D:2 * D], mod_ref[:, 0:D]).astype(bf16)
    off = 0
    for name, w in zip(("a", "qkv", "rx", "ry", "g"), IN_COLS):
        z = jnp.dot(h, w_ref[:, off:off + w], preferred_element_type=f32)
        off += w
        if name == "a":
            za_ref[...] = jnp.concatenate([z[:, 2 * CONV_W:] * z[:, :CONV_W], z[:, CONV_W:2 * CONV_W]],
                                          axis=1).astype(bf16)
        elif name == "qkv":
            qk = _rope(z[:, :Q_W + KV_W], cos_ref[...], s1_ref[...], s2_ref[...])
            zq_ref[...] = (qk[:, :Q_W] * (HEAD_DIM ** -0.5)).astype(bf16)
            lo = lax.broadcasted_iota(i32, (TM, LANES), 1) < HEAD_DIM
            dup = []
            for pair in (qk[:, Q_W:], z[:, Q_W + KV_W:]):
                swapped = pltpu.roll(pair, HEAD_DIM, axis=1)
                dup += [jnp.where(lo, pair, swapped), jnp.where(lo, swapped, pair)]
            zkv_ref[...] = jnp.concatenate(dup, axis=1).astype(bf16)
        elif name == "rx":
            zrx_ref[...] = z.astype(bf16)
        elif name == "ry":
            t_y = jnp.tanh(z * (0.7978845608028654 + (0.7978845608028654 * 0.044715) * (z * z)))
            zry_ref[...] = (z * t_y + z).astype(bf16)
        else:
            zg_ref[...] = z.astype(bf16)


def _in_proj(xs, ctx, acc, modsel_prev, modsel, norm, tabs, w_all):
    with_moe = acc is not None
    b = xs.shape[0]
    nt = xs.shape[1] + (0 if with_moe else ctx.shape[1])
    n_tiles = nt // TM
    ct = n_tiles - 1
    tile = lambda w: pl.BlockSpec((None, TM, w), lambda i, t: (i, t, 0))
    mod_spec = pl.BlockSpec((None, None, 1, N_MOD * D), lambda i, t: (i, t // ct, 0, 0))
    tab_spec = pl.BlockSpec((TM, LANES), lambda i, t: (t, 0))
    if with_moe:
        in_specs = [tile(D), pl.BlockSpec((None, TM * SUBLANES, LANES), lambda i, t: (i, t, 0)), mod_spec]
        args = [xs, acc, modsel_prev]
    else:
        in_specs = [pl.BlockSpec((None, TM, D), lambda i, t: (i, jnp.minimum(t, ct - 1), 0)),
                    pl.BlockSpec((None, TM, D), lambda i, t: (i, 0, 0))]
        args = [xs, ctx]
    in_specs += [mod_spec, pl.BlockSpec((1, D), lambda i, t: (0, 0)), tab_spec, tab_spec, tab_spec,
                 pl.BlockSpec((D, IN_TOTAL), lambda i, t: (0, 0), pipeline_mode=pl.Buffered(1))]
    args += [modsel, norm, *tabs, w_all]
    out_shape = [jax.ShapeDtypeStruct((b, nt, w), bf16) for w in OUT_COLS]
    out_specs = [tile(w) for w in OUT_COLS]
    if with_moe:
        out_shape = [jax.ShapeDtypeStruct((b, nt, D), f32)] + out_shape
        out_specs = [tile(D)] + out_specs
    return pl.pallas_call(
        functools.partial(_inproj_kernel, with_moe=with_moe),
        out_shape=out_shape, grid=(b, n_tiles), in_specs=in_specs, out_specs=out_specs,
        compiler_params=_cparams("arbitrary", "arbitrary"), name="in_proj",
    )(*args)


def _attn_kernel(sink_ref, q_ref, kvp_ref, kvc_ref, kvn_ref, kvx_ref, o_ref, *, n_lat_blocks, n_ctx):
    j = pl.program_id(1)
    group = N_HEADS // N_KV
    nlb = n_lat_blocks
    lo = lax.broadcasted_iota(i32, (QB, LANES), 1) < HEAD_DIM
    diff = lax.broadcasted_iota(i32, (QB, QB), 1) - lax.broadcasted_iota(i32, (QB, QB), 0)
    lat = j < nlb
    ok = (diff >= jnp.where(lat & (j >= 1), 0, QB),
          diff >= jnp.where(lat, -QB, QB),
          -diff >= jnp.where(lat & (j <= nlb - 2), 0, QB))
    zero = jnp.zeros((QB, LANES), bf16)
    scores, values = [], []
    for g in range(N_KV):
        sl = slice(g * LANES, (g + 1) * LANES)
        vsl = slice(K2_W + g * LANES, K2_W + (g + 1) * LANES)
        k = jnp.concatenate([r[:, sl] for r in (kvp_ref, kvc_ref, kvn_ref, kvx_ref)], axis=0)
        v = jnp.concatenate([r[:, vsl] for r in (kvp_ref, kvc_ref, kvn_ref, kvx_ref)], axis=0)
        parts = []
        for pr in range(group // 2):
            qp = q_ref[:, (g * (group // 2) + pr) * LANES:(g * (group // 2) + pr + 1) * LANES]
            parts += [jnp.where(lo, qp, zero), jnp.where(lo, zero, qp)]
        q4 = jnp.concatenate(parts, axis=0)
        scores.append(lax.dot_general(q4, k, (((1,), (1,)), ((), ())), preferred_element_type=f32))
        values.append(v)
    for g in range(N_KV):
        s, v = scores[g], values[g]
        ps, inv = [], []
        for hh in range(group):
            sh = s[hh * QB:(hh + 1) * QB]
            sm = jnp.concatenate([jnp.where(ok[kb], sh[:, kb * QB:(kb + 1) * QB], NEG_INF) for kb in range(3)]
                                 + [sh[:, 3 * QB:]], axis=1)
            sink = sink_ref[0, g * group + hh]
            m = jnp.maximum(jnp.max(sm, axis=1, keepdims=True), sink)
            p = jnp.exp(sm - m)
            inv.append(1.0 / (jnp.sum(p, axis=1, keepdims=True) + jnp.exp(sink - m)))
            ps.append(p.astype(bf16))
        o = jnp.dot(jnp.concatenate(ps, axis=0), v, preferred_element_type=f32) * jnp.concatenate(inv, axis=0)
        for pr in range(group // 2):
            pair = jnp.where(lo, o[(2 * pr) * QB:(2 * pr + 1) * QB], o[(2 * pr + 1) * QB:(2 * pr + 2) * QB])
            c0 = (g * (group // 2) + pr) * LANES
            o_ref[:, c0:c0 + LANES] = pair.astype(bf16)


def _attention(sink, zq, zkv, n_lat, n_qblocks):
    b, nt, _ = zq.shape
    n_ctx = nt - n_lat
    nlb = n_lat // QB
    cidx = n_lat // n_ctx
    win = lambda d: pl.BlockSpec((None, QB, KV2_W), lambda i, j: (i, jnp.clip(j + d, 0, nlb - 1), 0))
    ctx = pl.BlockSpec((None, n_ctx, KV2_W), lambda i, j: (i, cidx, 0))
    return pl.pallas_call(
        functools.partial(_attn_kernel, n_lat_blocks=nlb, n_ctx=n_ctx),
        out_shape=jax.ShapeDtypeStruct((b, n_qblocks * QB, Q_W), bf16),
        grid=(b, n_qblocks),
        in_specs=[pl.BlockSpec(memory_space=pltpu.SMEM),
                  pl.BlockSpec((None, QB, Q_W), lambda i, j: (i, j, 0)),
                  win(-1), win(0), win(1), ctx],
        out_specs=pl.BlockSpec((None, QB, Q_W), lambda i, j: (i, j, 0)),
        compiler_params=_cparams("arbitrary", "arbitrary"), name="attention",
    )(sink, zq, zkv, zkv, zkv, zkv)


def _rnn_chunk(s, ct, rev):
    return jnp.where(s == 0, ct, (ct - s) if rev else (s - 1))


def _block_scan(a, b, h_in, rowi, rev):
    n = SUBLANES
    order = range(n - 1, -1, -1) if rev else range(n)
    hs = [None] * n
    ps = [None] * n
    prev = None
    for j in order:
        if prev is None:
            hs[j], ps[j] = b[j], a[j]
        else:
            hs[j], ps[j] = a[j] * hs[prev] + b[j], a[j] * ps[prev]
        prev = j
    he, pe = hs[prev], ps[prev]
    for sh in (1, 2, 4):
        rs = n - sh if rev else sh
        msk = (rowi < n - sh) if rev else (rowi >= sh)
        he, pe = (jnp.where(msk, he + pe * pltpu.roll(he, rs, axis=0), he),
                  jnp.where(msk, pe * pltpu.roll(pe, rs, axis=0), pe))
    e = he + pe * h_in
    carry = jnp.where((rowi == n - 1) if rev else (rowi == 0), h_in, pltpu.roll(e, n - 1 if rev else 1, axis=0))
    out = [hs[j] + ps[j] * carry for j in range(n)]
    last = e[0:1, :] if rev else e[n - 1:n, :]
    return out, jnp.broadcast_to(last, (n, LANES))


def _rnn_kernel(u_ref, halo_ref, cw_ref, cb_ref, wg_ref, ba_ref, bx_ref, lam_ref, o_ref, ubuf, obuf, hcar, *,
                rev, ct):
    s = pl.program_id(1)
    chunk = _rnn_chunk(s, ct, rev)

    @pl.when(s == 0)
    def _():
        hcar[...] = jnp.zeros_like(hcar)

    u = u_ref[...].astype(f32)
    if rev:
        halo_ok = (chunk != ct - 1) & (chunk != ct)
        base = 0
    else:
        halo_ok = (chunk != 0) & (chunk != ct)
        base = BF16_ROWS - (RNN_CONV_K - 1)
    hal = jnp.where(halo_ok, halo_ref[...].astype(f32), 0.0)
    n_lt = RNN_W // LANES
    for c in range(n_lt):
        sl = slice(c * LANES, (c + 1) * LANES)
        if rev:
            ubuf[c, 0:TM, :] = u[:, sl]
            ubuf[c, TM:TM + BF16_ROWS, :] = hal[:, sl]
        else:
            ubuf[c, 0:BF16_ROWS, :] = hal[:, sl]
            ubuf[c, BF16_ROWS:BF16_ROWS + TM, :] = u[:, sl]
    sp = jnp.logaddexp(-lam_ref[...], 0.0)
    rowi = lax.broadcasted_iota(i32, (SUBLANES, LANES), 0)
    blk_rows = SUBLANES * SUBLANES
    n_blk = TM // blk_rows
    for c in range(n_lt):
        sl = slice(c * LANES, (c + 1) * LANES)
        taps = [cw_ref[k:k + 1, sl] for k in range(RNN_CONV_K)]
        xt = []
        for blk in range(n_blk):
            for j in range(SUBLANES):
                r0 = base + blk * blk_rows + j
                acc = cb_ref[:, sl] + ubuf[c, pl.ds(r0, SUBLANES, stride=SUBLANES), :] * taps[0]
                for k in range(1, RNN_CONV_K):
                    acc = acc + ubuf[c, pl.ds(r0 + k, SUBLANES, stride=SUBLANES), :] * taps[k]
                xt.append(acc)
        xp = jnp.concatenate(xt, axis=0)
        g = jnp.dot(xp.astype(bf16), wg_ref[c], preferred_element_type=f32)
        t_r = jnp.tanh(g[:, :RNN_BLK] + 0.5 * ba_ref[:, sl])
        t_i = jnp.tanh(g[:, RNN_BLK:] + 0.5 * bx_ref[:, sl])
        c4 = (-0.5 * RG_C) * sp[:, sl]
        log_a = c4 * t_r + c4
        a = jnp.exp(log_a)
        y = -jnp.tanh(log_a) * (a * a + 1.0)
        xh = 0.5 * xp
        bb = jnp.where(y > 0.0, y * lax.rsqrt(y), 0.0) * (xh * t_i + xh)
        h_in = hcar[:, sl]
        for blk in (range(n_blk - 1, -1, -1) if rev else range(n_blk)):
            rows = lambda j: slice((blk * SUBLANES + j) * SUBLANES, (blk * SUBLANES + j + 1) * SUBLANES)
            hs, h_in = _block_scan([a[rows(j)] for j in range(SUBLANES)], [bb[rows(j)] for j in range(SUBLANES)],
                                   h_in, rowi, rev)
            for j in range(SUBLANES):
                obuf[c, pl.ds(blk * blk_rows + j, SUBLANES, stride=SUBLANES), :] = hs[j]
        hcar[:, sl] = h_in
    o_ref[...] = jnp.concatenate([obuf[c] for c in range(n_lt)], axis=1).astype(bf16)


def _rnn(zrx, cw, cb, wg, ba, bx, lam, rev):
    b, nt, _ = zrx.shape
    n_tiles = nt // TM
    ct = n_tiles - 1
    per = TM // BF16_ROWS
    if rev:
        halo_map = lambda i, s: (i, jnp.minimum((_rnn_chunk(s, ct, rev) + 1) * per, nt // BF16_ROWS - 1), 0)
    else:
        halo_map = lambda i, s: (i, jnp.maximum(_rnn_chunk(s, ct, rev) * per - 1, 0), 0)
    full = lambda shape: pl.BlockSpec(shape, lambda i, s: (0,) * len(shape))
    return pl.pallas_call(
        functools.partial(_rnn_kernel, rev=rev, ct=ct),
        out_shape=jax.ShapeDtypeStruct((b, nt, RNN_W), bf16),
        grid=(b, n_tiles),
        in_specs=[pl.BlockSpec((None, TM, RNN_W), lambda i, s: (i, _rnn_chunk(s, ct, rev), 0)),
                  pl.BlockSpec((None, BF16_ROWS, RNN_W), halo_map),
                  full((RNN_CONV_K, RNN_W)), full((1, RNN_W)), full((RNN_BLOCKS, RNN_BLK, 2 * RNN_BLK)),
                  full((1, RNN_W)), full((1, RNN_W)), full((1, RNN_W))],
        out_specs=pl.BlockSpec((None, TM, RNN_W), lambda i, s: (i, _rnn_chunk(s, ct, rev), 0)),
        scratch_shapes=[pltpu.VMEM((RNN_W // LANES, TM + BF16_ROWS, LANES), f32),
                        pltpu.VMEM((RNN_W // LANES, TM, LANES), f32), pltpu.VMEM((SUBLANES, RNN_W), f32)],
        compiler_params=_cparams("arbitrary", "arbitrary"), name="rnn_bwd" if rev else "rnn_fwd",
    )(zrx, zrx, cw, cb, wg, ba, bx, lam)


def _merge_kernel(*refs, ct, split):
    (za_ref, zap_ref, zan_ref, att_ref, hf_ref, hb_ref, ry_ref, zg_ref, mod_ref, ca_ref, wc_ref,
     wa_ref, wr_ref, wo_ref, n2_ref, wrt_ref, shift_ref, xo_ref, h2_ref, aff_ref) = refs[2 if split else 1:]
    t = pl.program_id(1)
    if split:
        x_in = jnp.where(t == ct, refs[1][...], refs[0][...])
    else:
        x_in = refs[0][...]
    cw = CONV_W
    cu_b = za_ref[:, 0:cw]
    sh = jnp.dot(shift_ref[...], cu_b, preferred_element_type=f32)
    prev_ok = (t != 0) & (t != ct)
    next_ok = (t != ct - 1) & (t != ct)
    cu_p = jnp.where(prev_ok, zap_ref[BF16_ROWS - 1:BF16_ROWS, 0:cw].astype(f32), 0.0)
    cu_n = jnp.where(next_ok, zan_ref[0:1, 0:cw].astype(f32), 0.0)
    rowi = lax.broadcasted_iota(i32, (TM, cw), 0)
    cu_prev = jnp.where(rowi == 0, cu_p, sh[:TM])
    cu_next = jnp.where(rowi == TM - 1, cu_n, sh[TM:])
    y = cu_prev * ca_ref[0:1, :] + cu_b.astype(f32) * ca_ref[1:2, :] + cu_next * ca_ref[2:3, :]
    cnv = jnp.dot((za_ref[:, cw:2 * cw].astype(f32) * y).astype(bf16), wc_ref[...], preferred_element_type=f32)
    att = jnp.dot(att_ref[...], wa_ref[...], preferred_element_type=f32)
    rec = hf_ref[...].astype(f32) + hb_ref[...].astype(f32)
    rnn = jnp.dot((ry_ref[...].astype(f32) * rec).astype(bf16), wr_ref[...], preferred_element_type=f32)
    mix2 = None
    for k, br in enumerate((cnv, att, rnn)):
        term = jnp.tanh(zg_ref[:, k * D:(k + 1) * D].astype(f32)) * br + br
        mix2 = term if mix2 is None else mix2 + term
    x = x_in + mod_ref[:, 2 * D:3 * D] * jnp.dot(mix2.astype(bf16), wo_ref[...], preferred_element_type=f32)
    xo_ref[...] = x
    h2 = _rms_mod(x, n2_ref[...], mod_ref[:, 4 * D:5 * D], mod_ref[:, 3 * D:4 * D])
    _rows_to_tiles(h2_ref, h2, TM)
    logits = lax.dot_general(wrt_ref[...], h2.astype(bf16), (((1,), (1,)), ((), ())), preferred_element_type=f32)
    e = jnp.exp(logits - jnp.max(logits, axis=0, keepdims=True))
    aff_ref[...] = e / jnp.sum(e, axis=0, keepdims=True)


def _row_shift_operator():
    r = jnp.arange(2 * TM)[:, None]
    c = jnp.arange(TM)[None, :]
    return (c == jnp.where(r < TM, r - 1, r - TM + 1)).astype(bf16)


def _merge(xs, ctx, za, att, hf, hb, zry, zg, modsel, conv_a, wc, wa, wr, wo, norm2, wrt, n_tiles_eff):
    b, nt, _ = za.shape
    ct = nt // TM - 1
    per = TM // BF16_ROWS
    split = ctx is not None
    tile = lambda w: pl.BlockSpec((None, TM, w), lambda i, t: (i, t, 0))
    full = lambda shape: pl.BlockSpec(shape, lambda i, t: (0,) * len(shape), pipeline_mode=pl.Buffered(1))
    if split:
        x_specs = [pl.BlockSpec((None, TM, D), lambda i, t: (i, jnp.minimum(t, ct - 1), 0)),
                   pl.BlockSpec((None, TM, D), lambda i, t: (i, 0, 0))]
        x_args = [xs, ctx]
    else:
        x_specs = [tile(D)]
        x_args = [xs]
    return pl.pallas_call(
        functools.partial(_merge_kernel, ct=ct, split=split),
        out_shape=[jax.ShapeDtypeStruct((b, n_tiles_eff * TM, D), f32),
                   jax.ShapeDtypeStruct((b, n_tiles_eff * TM * SUBLANES, LANES), f32),
                   jax.ShapeDtypeStruct((b, N_EXPERTS, n_tiles_eff * TM), f32)],
        grid=(b, n_tiles_eff),
        in_specs=x_specs + [tile(ZA_W),
                  pl.BlockSpec((None, BF16_ROWS, ZA_W), lambda i, t: (i, jnp.maximum(t * per - 1, 0), 0)),
                  pl.BlockSpec((None, BF16_ROWS, ZA_W),
                               lambda i, t: (i, jnp.minimum((t + 1) * per, nt // BF16_ROWS - 1), 0)),
                  tile(Q_W), tile(RNN_W), tile(RNN_W), tile(RNN_W), tile(G_G),
                  pl.BlockSpec((None, None, 1, N_MOD * D), lambda i, t: (i, t // ct, 0, 0)),
                  full((3, CONV_W)), full((CONV_W, D)), full((Q_W, D)), full((RNN_W, D)), full((D, D)),
                  full((1, D)), full((N_EXPERTS, D)), full((2 * TM, TM))],
        out_specs=[tile(D), pl.BlockSpec((None, TM * SUBLANES, LANES), lambda i, t: (i, t, 0)),
                   pl.BlockSpec((None, N_EXPERTS, TM), lambda i, t: (i, 0, t))],
        compiler_params=_cparams("arbitrary", "arbitrary"), name="merge",
    )(*x_args, za, za, za, att, hf, hb, zry, zg, modsel, conv_a, wc, wa, wr, wo, norm2, wrt, _row_shift_operator())


ROUTE_QW = 64
ROUTE_TOK_SHIFT = 6
CUM_BLK = 256


def _cumsum_lanes(x, n):
    blk = min(CUM_BLK, n)
    tri = (lax.broadcasted_iota(i32, (blk, blk), 0) <= lax.broadcasted_iota(i32, (blk, blk), 1)).astype(bf16)
    carry = jnp.zeros((x.shape[0], 1), f32)
    outs = []
    for j in range(n // blk):
        c = jnp.dot(x[:, j * blk:(j + 1) * blk].astype(bf16), tri, preferred_element_type=f32) + carry
        outs.append(c)
        carry = c[:, blk - 1:blk]
    return jnp.concatenate(outs, axis=1)


def _route_kernel(aff_ref, idx_ref, val_ref, pos_s, *, n, cap):
    aff = aff_ref[...]
    thr = jnp.zeros((N_EXPERTS, 1), i32)
    for bit in range(30, -1, -1):
        cand = thr | (1 << bit)
        cnt = jnp.sum(jnp.where(aff >= lax.bitcast_convert_type(cand, f32), 1.0, 0.0), axis=1, keepdims=True)
        thr = jnp.where(cnt >= float(cap), cand, thr)
    thr_f = lax.bitcast_convert_type(thr, f32)
    gt = aff > thr_f
    eq = aff == thr_f
    need = float(cap) - jnp.sum(jnp.where(gt, 1.0, 0.0), axis=1, keepdims=True)
    sel = gt | (eq & (_cumsum_lanes(eq.astype(f32), n) <= need))
    pos_s[...] = jnp.where(sel, _cumsum_lanes(sel.astype(f32), n).astype(i32) - 1, -1)

    qw = min(ROUTE_QW, cap)
    shift = qw.bit_length() - 1
    tok = lax.broadcasted_iota(i32, (1, n), 1)
    tok_hi = (tok >> ROUTE_TOK_SHIFT).astype(f32)
    tok_lo = (tok & ((1 << ROUTE_TOK_SHIFT) - 1)).astype(f32)
    hi_iota = lax.broadcasted_iota(i32, (SUBLANES, n), 0)
    lo_iota = lax.broadcasted_iota(i32, (qw, n), 0)

    def body(e, carry):
        pos = pos_s[pl.ds(e, 1), :]
        a0 = aff_ref[pl.ds(e, 1), :]
        a_h = a0.astype(bf16).astype(f32)
        a_m = (a0 - a_h).astype(bf16).astype(f32)
        a_l = ((a0 - a_h) - a_m).astype(bf16).astype(f32)
        in_hi = (pos >> shift) == hi_iota
        lhs = jnp.concatenate([jnp.where(in_hi, r, 0.0) for r in (tok_hi, tok_lo, a_h, a_m, a_l)], axis=0)
        onehot = jnp.where((pos & (qw - 1)) == lo_iota, 1.0, 0.0).astype(bf16)
        res = lax.dot_general(lhs.astype(bf16), onehot, (((1,), (1,)), ((), ())), preferred_element_type=f32)
        r_hi, r_lo, v_h, v_m, v_l = (res[k * SUBLANES:(k + 1) * SUBLANES] for k in range(5))
        idx_ref[e] = (r_hi * float(1 << ROUTE_TOK_SHIFT) + r_lo).astype(i32)[:cap // qw]
        val_ref[e] = ((v_h + v_m) + v_l)[:cap // qw]
        return carry

    lax.fori_loop(0, N_EXPERTS, body, 0)


def _route(aff_t, n, cap, lane_block):
    b = aff_t.shape[0]
    qw = min(ROUTE_QW, cap)
    assert cap % qw == 0 and cap // qw <= SUBLANES and qw & (qw - 1) == 0
    idx, val = pl.pallas_call(
        functools.partial(_route_kernel, n=n, cap=cap),
        out_shape=[jax.ShapeDtypeStruct((b, N_EXPERTS, cap // qw, qw), i32),
                   jax.ShapeDtypeStruct((b, N_EXPERTS, cap // qw, qw), f32)],
        grid=(b,),
        in_specs=[pl.BlockSpec((None, N_EXPERTS, n), lambda i: (i, 0, lane_block))],
        out_specs=[pl.BlockSpec((None, N_EXPERTS, cap // qw, qw), lambda i: (i, 0, 0, 0)),
                   pl.BlockSpec((None, N_EXPERTS, cap // qw, qw), lambda i: (i, 0, 0, 0))],
        scratch_shapes=[pltpu.VMEM((N_EXPERTS, n), i32)],
        compiler_params=_cparams("arbitrary"), name="route",
    )(aff_t)
    return idx.reshape(b, N_EXPERTS, cap), val.reshape(b, N_EXPERTS, cap)


def _gather_kernel(idx_ref, h_ref, tok_ref, tbuf, *, cap):
    base = (pl.program_id(0) * N_EXPERTS + pl.program_id(1)) * cap
    sp = cap + SUBLANES
    for p in range(cap):
        i = idx_ref[base + p]
        tbuf[pl.ds(p, SUBLANES, stride=sp), :] = h_ref[pl.ds(pl.multiple_of(i * SUBLANES, SUBLANES), SUBLANES), :]
    tok_ref[...] = jnp.concatenate([tbuf[pl.ds(c * sp, cap), :] for c in range(D // LANES)], axis=1).astype(bf16)


def _gather(idx, h2t, cap):
    b, rows, _ = h2t.shape
    return pl.pallas_call(
        functools.partial(_gather_kernel, cap=cap),
        out_shape=jax.ShapeDtypeStruct((b, N_EXPERTS, cap, D), bf16),
        grid_spec=pltpu.PrefetchScalarGridSpec(
            num_scalar_prefetch=1, grid=(b, N_EXPERTS),
            in_specs=[pl.BlockSpec((None, rows, LANES), lambda i, e, idx: (i, 0, 0))],
            out_specs=pl.BlockSpec((None, None, cap, D), lambda i, e, idx: (i, e, 0, 0)),
            scratch_shapes=[pltpu.VMEM((SUBLANES * (cap + SUBLANES), LANES), f32)]),
        compiler_params=_cparams("arbitrary", "arbitrary"), name="moe_gather",
    )(idx, h2t)


def _cast3_kernel(a_ref, b_ref, c_ref, ao_ref, bo_ref, co_ref):
    ao_ref[...] = a_ref[...].astype(bf16)
    bo_ref[...] = b_ref[...].astype(bf16)
    co_ref[...] = c_ref[...].astype(bf16)


def _expert0_bf16(wg, wu, wd, layer):
    spec = pl.BlockSpec((None, None, D, EXPERT_FF), lambda i: (layer, 0, 0, 0))
    ospec = pl.BlockSpec((D, EXPERT_FF), lambda i: (0, 0))
    return pl.pallas_call(
        _cast3_kernel, out_shape=[jax.ShapeDtypeStruct((D, EXPERT_FF), bf16)] * 3, grid=(1,),
        in_specs=[spec] * 3, out_specs=[ospec] * 3, compiler_params=_cparams("arbitrary"), name="expert0_cast",
    )(wg, wu, wd)


def _ffn_kernel(tok_ref, g0_ref, u0_ref, d0_ref, gq_ref, uq_ref, dq_ref, o_ref, w_even, w_odd):
    e = pl.program_id(0)
    i = pl.program_id(1)

    @pl.when((e == 0) & (i == 0))
    def _():
        w_even[0] = g0_ref[...]
        w_even[1] = u0_ref[...]
        w_even[2] = d0_ref[...]

    rows = gq_ref.shape[0]
    r0 = pl.multiple_of(i * rows, rows)

    def step(use, fill):
        t = tok_ref[...]
        g = jnp.dot(t, use[0], preferred_element_type=f32)
        for k, q_ref in enumerate((gq_ref, uq_ref, dq_ref)):
            fill[k, pl.ds(r0, rows), :] = q_ref[...].astype(bf16)
        u = jnp.dot(t, use[1], preferred_element_type=f32)
        a = (g * _sigmoid(g) * u).astype(bf16)
        o_ref[...] = jnp.dot(a, use[2], preferred_element_type=f32)

    @pl.when(e % 2 == 0)
    def _():
        step(w_even, w_odd)

    @pl.when(e % 2 == 1)
    def _():
        step(w_odd, w_even)


def _ffn(tok, wg, wu, wd, layer):
    b, _, cap, _ = tok.shape
    assert D == EXPERT_FF and D % b == 0 and (D // b) % BF16_ROWS == 0
    rows = D // b
    w0 = _expert0_bf16(wg, wu, wd, layer)
    w0spec = pl.BlockSpec((D, EXPERT_FF), lambda e, i: (0, 0), pipeline_mode=pl.Buffered(1))
    qspec = pl.BlockSpec((None, None, rows, D), lambda e, i: (layer, jnp.minimum(e + 1, N_EXPERTS - 1), i, 0))
    return pl.pallas_call(
        _ffn_kernel,
        out_shape=jax.ShapeDtypeStruct((b, N_EXPERTS, cap, D), f32),
        grid=(N_EXPERTS, b),
        in_specs=[pl.BlockSpec((None, None, cap, D), lambda e, i: (i, e, 0, 0)),
                  w0spec, w0spec, w0spec, qspec, qspec, qspec],
        out_specs=pl.BlockSpec((None, None, cap, D), lambda e, i: (i, e, 0, 0)),
        scratch_shapes=[pltpu.VMEM((3, D, EXPERT_FF), bf16), pltpu.VMEM((3, D, EXPERT_FF), bf16)],
        compiler_params=_cparams("arbitrary", "arbitrary"), name="moe_ffn",
    )(tok, *w0, wg, wu, wd)


SCATTER_UNROLL = 8


def _scatter_kernel(idx_ref, val_ref, o_ref, acc_ref, tbuf, *, cap):
    e = pl.program_id(1)
    base = (pl.program_id(0) * N_EXPERTS + e) * cap
    sp = cap + SUBLANES

    @pl.when(e == 0)
    def _():
        acc_ref[...] = jnp.zeros_like(acc_ref)

    for c in range(D // LANES):
        tbuf[pl.ds(c * sp, cap), :] = o_ref[:, c * LANES:(c + 1) * LANES]
    u = min(SCATTER_UNROLL, cap)
    for p0 in range(0, cap, u):
        pend = []
        for p in range(p0, p0 + u):
            r0 = pl.multiple_of(idx_ref[base + p] * SUBLANES, SUBLANES)
            pend.append((r0, acc_ref[pl.ds(r0, SUBLANES), :] + tbuf[pl.ds(p, SUBLANES, stride=sp), :] * val_ref[base + p]))
        for r0, v in pend:
            acc_ref[pl.ds(r0, SUBLANES), :] = v


def _scatter(idx, val, o, rows):
    b, _, cap, _ = o.shape
    return pl.pallas_call(
        functools.partial(_scatter_kernel, cap=cap),
        out_shape=jax.ShapeDtypeStruct((b, rows, LANES), f32),
        grid_spec=pltpu.PrefetchScalarGridSpec(
            num_scalar_prefetch=2, grid=(b, N_EXPERTS),
            in_specs=[pl.BlockSpec((None, None, cap, D), lambda i, e, idx, val: (i, e, 0, 0))],
            out_specs=pl.BlockSpec((None, rows, LANES), lambda i, e, idx, val: (i, 0, 0)),
            scratch_shapes=[pltpu.VMEM((SUBLANES * (cap + SUBLANES), LANES), f32)]),
        compiler_params=_cparams("arbitrary", "arbitrary"), name="moe_scatter",
    )(idx, val, o)


def _moe(aff_t, h2t, wg, wu, wd, layer, n_lat, n_ctx):
    idx, val = _route(aff_t, n_lat, CAP_FACTOR * n_lat // N_EXPERTS, 0)
    if n_ctx:
        idx_c, val_c = _route(aff_t, n_ctx, CAP_FACTOR * n_ctx // N_EXPERTS, n_lat // n_ctx)
        idx = jnp.concatenate([idx, idx_c + n_lat], axis=-1)
        val = jnp.concatenate([val, val_c], axis=-1)
    cap = idx.shape[-1]
    idx = idx.reshape(-1)
    val = val.reshape(-1)
    tok = _gather(idx, h2t, cap)
    o = _ffn(tok, wg, wu, wd, layer)
    return _scatter(idx, val, o, h2t.shape[1])


def _final_kernel(x_ref, acc_ref, mod_ref, n_ref, o_ref):
    x = x_ref[...] + mod_ref[:, 5 * D:6 * D] * _tiles_to_rows(acc_ref, TM)
    o_ref[...] = (x * lax.rsqrt(jnp.mean(x * x, axis=-1, keepdims=True) + EPS)) * n_ref[...]


def _final(xs, acc, modsel, norm, n_lat):
    b = xs.shape[0]
    return pl.pallas_call(
        _final_kernel,
        out_shape=jax.ShapeDtypeStruct((b, n_lat, D), f32),
        grid=(b, n_lat // TM),
        in_specs=[pl.BlockSpec((None, TM, D), lambda i, t: (i, t, 0)),
                  pl.BlockSpec((None, TM * SUBLANES, LANES), lambda i, t: (i, t, 0)),
                  pl.BlockSpec((None, None, 1, N_MOD * D), lambda i, t: (i, 0, 0, 0)),
                  pl.BlockSpec((1, D), lambda i, t: (0, 0))],
        out_specs=pl.BlockSpec((None, TM, D), lambda i, t: (i, t, 0)),
        compiler_params=_cparams("arbitrary", "arbitrary"), name="final_norm",
    )(xs, acc, modsel, norm)


def _rope_tables(n_lat, n_ctx):
    pos = jnp.arange(n_lat, dtype=f32)
    rows = jnp.floor(pos / GRID_W)
    cols = pos - rows * GRID_W
    half = HEAD_DIM // 4
    inv = 1.0 / (ROPE_BASE ** (jnp.arange(0, 2 * half, 2, dtype=f32) / (2 * half)))
    lane = jnp.arange(LANES)
    d = lane % HEAD_DIM
    ang = jnp.where((d < 2 * half)[None, :], rows[:, None], cols[:, None]) * inv[d % half][None, :]
    first = ((d % (2 * half)) < half)[None, :]
    cos = jnp.cos(ang)
    sin = jnp.sin(ang)
    s1 = jnp.where(first, -sin, 0.0)
    s2 = jnp.where(first, 0.0, sin)
    pad = lambda t, v: jnp.concatenate([t, jnp.full((n_ctx, LANES), v, f32)], axis=0)
    return pad(cos, 1.0), pad(s1, 0.0), pad(s2, 0.0)


def _in_weights_bf16(w):
    scale = jnp.concatenate([jnp.ones((IN_TOTAL - G_G,), f32), jnp.full((G_G,), 0.5, f32)])
    return (w * scale[None, :]).astype(bf16)


def kernel(x, c, ctx, c_ctx, norm1, norm2, w_mod, b_mod, w_in, conv_a, w_conv_out, attn_sink, w_attn_out, rnn_conv_w,
           rnn_conv_b, rnn_w_a, rnn_b_a, rnn_w_x, rnn_b_x, rnn_lam, w_rnn_out, w_o, w_router, w_e_gate, w_e_up,
           w_e_down, final_norm):
    b, n_lat, d = x.shape
    n_ctx = ctx.shape[1]
    depth = w_mod.shape[0]
    assert d == D and n_ctx == TM and n_lat % TM == 0 and b < SUBLANES
    nt = n_lat + n_ctx
    ct = nt // TM - 1

    cstack = jnp.concatenate([c, c_ctx[None, :], jnp.zeros((SUBLANES - b - 1, D), f32)], axis=0)
    mod = _modulation(cstack, w_mod, b_mod)
    tabs = _rope_tables(n_lat, n_ctx)

    xs = x
    acc = None
    modsel_prev = None
    for l in range(depth):
        ctx_out = l < depth - 1
        ctx_sep = ctx if l == 0 else None
        modsel = jnp.stack([mod[l, :b], jnp.broadcast_to(mod[l, b], (b, N_MOD * D))], axis=1)[:, :, None, :]
        outs = _in_proj(xs, ctx_sep, acc, modsel_prev, modsel, norm1[l][None, :], tabs, _in_weights_bf16(w_in[l]))
        if acc is not None:
            xs, outs = outs[0], outs[1:]
        za, zq, zkv, zrx, zry, zg = outs
        att = _attention(attn_sink[l][None, :], zq, zkv, n_lat, (nt if ctx_out else n_lat) // QB)
        hs = []
        for dr in range(2):
            wg = (0.5 * jnp.concatenate([rnn_w_a[l, dr], rnn_w_x[l, dr]], axis=-1)).astype(bf16)
            hs.append(_rnn(zrx, rnn_conv_w[l, dr], rnn_conv_b[l, dr][None, :], wg, rnn_b_a[l, dr][None, :],
                           rnn_b_x[l, dr][None, :], rnn_lam[l, dr][None, :], rev=bool(dr)))
        xs, h2t, aff_t = _merge(xs, ctx_sep, za, att, hs[0], hs[1], zry, zg, modsel, conv_a[l],
                                w_conv_out[l].astype(bf16),
                                w_attn_out[l].astype(bf16), (0.5 * w_rnn_out[l]).astype(bf16),
                                (0.5 * w_o[l]).astype(bf16),
                                norm2[l][None, :], w_router[l].T.astype(bf16), (ct + 1) if ctx_out else ct)
        acc = _moe(aff_t, h2t, w_e_gate, w_e_up, w_e_down, l, n_lat, n_ctx if ctx_out else 0)
        modsel_prev = modsel
    return _final(xs, acc, modsel_prev, final_norm[None, :], n_lat)
```

```python
import functools

import jax
import jax.numpy as jnp
from jax import lax
from jax.experimental import pallas as pl
from jax.experimental.pallas import tpu as pltpu

f32 = jnp.float32
bf16 = jnp.bfloat16
i32 = jnp.int32

D = 1024
EPS = 1e-6
CONV_W = 512
N_HEADS = 8
N_KV = 2
HEAD_DIM = 64
Q_W = N_HEADS * HEAD_DIM
KV_W = N_KV * HEAD_DIM
WINDOW = 128
GRID_W = 64
ROPE_BASE = 10000.0
NEG_INF = -1e30
RNN_W = 1024
RNN_BLOCKS = 8
RNN_BLK = RNN_W // RNN_BLOCKS
RNN_CONV_K = 4
RG_C = 8.0
N_EXPERTS = 16
EXPERT_FF = 1024
CAP_FACTOR = 2
N_MOD = 6

LANES = 128
SUBLANES = 8
BF16_ROWS = 16
VMEM_LIMIT = 56 * 1024 * 1024

TM = 256
QB = 128
K2_W = 2 * KV_W

G_A = 3 * CONV_W
G_QKV = Q_W + 2 * KV_W
G_RX = RNN_W
G_RY = RNN_W
G_G = 3 * D
IN_COLS = (G_A, G_QKV, G_RX, G_RY, G_G)
IN_TOTAL = sum(IN_COLS)
KV2_W = 2 * K2_W
ZA_W = 2 * CONV_W
OUT_COLS = (ZA_W, Q_W, KV2_W, G_RX, G_RY, G_G)


def _cparams(*sem):
    return pltpu.CompilerParams(dimension_semantics=sem, vmem_limit_bytes=VMEM_LIMIT)


def _sigmoid(x):
    return 0.5 * jnp.tanh(0.5 * x) + 0.5


def _rms_mod(x, g, sc, sh):
    y = x * lax.rsqrt(jnp.mean(x * x, axis=-1, keepdims=True) + EPS)
    return (y * g) * (1.0 + sc) + sh


def _mod_kernel(c_ref, w_ref, b_ref, o_ref):
    c = c_ref[...]
    s = (c * _sigmoid(c)).astype(bf16)
    o_ref[...] = jnp.dot(s, w_ref[...].astype(bf16), preferred_element_type=f32) + b_ref[...]


def _modulation(cstack, w_mod, b_mod):
    depth = w_mod.shape[0]
    tn = 1536
    return pl.pallas_call(
        _mod_kernel,
        out_shape=jax.ShapeDtypeStruct((depth, SUBLANES, N_MOD * D), f32),
        grid=(depth, N_MOD * D // tn),
        in_specs=[
            pl.BlockSpec((SUBLANES, D), lambda l, j: (0, 0)),
            pl.BlockSpec((None, D, tn), lambda l, j: (l, 0, j)),
            pl.BlockSpec((None, 1, tn), lambda l, j: (l, 0, j)),
        ],
        out_specs=pl.BlockSpec((None, SUBLANES, tn), lambda l, j: (l, 0, j)),
        compiler_params=_cparams("arbitrary", "arbitrary"),
        name="modulation",
    )(cstack, w_mod, b_mod.reshape(depth, 1, N_MOD * D))


def _tiles_to_rows(ref, rows):
    return jnp.concatenate([ref[pl.ds(c, rows, stride=SUBLANES), :] for c in range(D // LANES)], axis=1)


def _rows_to_tiles(ref, val, rows):
    per = D // LANES
    for r in range(rows // SUBLANES):
        for c in range(per):
            ref[pl.ds(r * SUBLANES * per + c, SUBLANES, stride=per), :] = val[r * SUBLANES:(r + 1) * SUBLANES,
                                                                              c * LANES:(c + 1) * LANES]


def _rope(z, cos, s1, s2):
    outs = []
    for c in range(z.shape[1] // LANES):
        x = z[:, c * LANES:(c + 1) * LANES]
        outs.append(x * cos + pltpu.roll(x, LANES - 16, axis=1) * s1 + pltpu.roll(x, 16, axis=1) * s2)
    return jnp.concatenate(outs, axis=1)


W_A_COLS = G_A + G_QKV + G_RX


def _proj_a_kernel(*refs, with_moe, ct):
    if with_moe:
        (x_ref, acc_ref, modp_ref), refs = refs[:3], refs[3:]
    else:
        (x_ref, ctx_ref), refs = refs[:2], refs[2:]
    mod_ref, n_ref, cos_ref, s1_ref, s2_ref, w_ref = refs[:6]
    prm, refs = refs[6:12], refs[12:]
    if with_moe:
        xo_ref, refs = refs[0], refs[1:]
    za_ref, zq_ref, zkv_ref, zrx_ref, h_ref, hf_ref, ubuf, obuf, hcar, halo_s = refs
    s = pl.program_id(1)
    chunk = _rnn_chunk(s, ct, False)

    @pl.when(s == 0)
    def _():
        hcar[...] = jnp.zeros_like(hcar)
        halo_s[...] = jnp.zeros_like(halo_s)

    if with_moe:
        x = x_ref[...] + modp_ref[:, 5 * D:6 * D] * _tiles_to_rows(acc_ref, TM)
        xo_ref[...] = x
    else:
        x = jnp.where(chunk == ct, ctx_ref[...], x_ref[...])
    h = _rms_mod(x, n_ref[...], mod_ref[:, D:2 * D], mod_ref[:, 0:D]).astype(bf16)
    h_ref[...] = h
    hal = jnp.where((chunk != 0) & (chunk != ct), halo_s[...], 0.0)
    quarter = G_RX // 4

    def scan_input(p):
        cols = slice(p * quarter, (p + 1) * quarter)
        u = jnp.dot(h, w_ref[:, G_A + G_QKV + p * quarter:G_A + G_QKV + (p + 1) * quarter],
                    preferred_element_type=f32)
        zrx_ref[:, cols] = u.astype(bf16)
        _rnn_fill(u, hal[:, cols], ubuf, False, c0=p * quarter // LANES)
        halo_s[:, cols] = u[TM - BF16_ROWS:, :]

    scan = lambda c: _rnn_tile(c, prm, ubuf, obuf, hcar, False)
    cw = CONV_W
    scan_input(0)
    a_x = jnp.dot(h, w_ref[:, 0:cw], preferred_element_type=f32)
    scan(0)
    scan_input(1)
    scan(1)
    a_b = jnp.dot(h, w_ref[:, cw:2 * cw], preferred_element_type=f32)
    scan(2)
    scan_input(2)
    scan(3)
    a_c = jnp.dot(h, w_ref[:, 2 * cw:G_A], preferred_element_type=f32)
    za_ref[...] = jnp.concatenate([a_c * a_x, a_b], axis=1).astype(bf16)
    scan(4)
    scan_input(3)
    scan(5)
    z = jnp.dot(h, w_ref[:, G_A:G_A + G_QKV], preferred_element_type=f32)
    qk = _rope(z[:, :Q_W + KV_W], cos_ref[...], s1_ref[...], s2_ref[...])
    zq_ref[...] = (qk[:, :Q_W] * (HEAD_DIM ** -0.5)).astype(bf16)
    lo = lax.broadcasted_iota(i32, (TM, LANES), 1) < HEAD_DIM
    dup = []
    for pair in (qk[:, Q_W:], z[:, Q_W + KV_W:]):
        swapped = pltpu.roll(pair, HEAD_DIM, axis=1)
        dup += [jnp.where(lo, pair, swapped), jnp.where(lo, swapped, pair)]
    zkv_ref[...] = jnp.concatenate(dup, axis=1).astype(bf16)
    scan(6)
    scan(7)
    hf_ref[...] = _rnn_states(obuf).astype(bf16)


def _proj_b_kernel(h_ref, w_ref, u_ref, halo_ref, *refs, ct):
    prm, (zry_ref, zg_ref, hb_ref, ubuf, obuf, hcar) = refs[:6], refs[6:]
    s = pl.program_id(1)
    chunk = _rnn_chunk(s, ct, True)

    @pl.when(s == 0)
    def _():
        hcar[...] = jnp.zeros_like(hcar)

    h = h_ref[...]
    hal = jnp.where((chunk != ct - 1) & (chunk != ct), halo_ref[...].astype(f32), 0.0)
    _rnn_fill(u_ref[...].astype(f32), hal, ubuf, True)
    slab = (G_RY + G_G) // RNN_LT
    for c in range(RNN_LT):
        z = jnp.dot(h, w_ref[:, c * slab:(c + 1) * slab], preferred_element_type=f32)
        if (c + 1) * slab <= G_RY:
            t_y = jnp.tanh(z * (0.7978845608028654 + (0.7978845608028654 * 0.044715) * (z * z)))
            zry_ref[:, c * slab:(c + 1) * slab] = (z * t_y + z).astype(bf16)
        else:
            zg_ref[:, c * slab - G_RY:(c + 1) * slab - G_RY] = z.astype(bf16)
        _rnn_tile(c, prm, ubuf, obuf, hcar, True)
    hb_ref[...] = _rnn_states(obuf).astype(bf16)


def _proj_a(xs, ctx, acc, modsel_prev, modsel, norm, tabs, w_a, rnn_prm):
    with_moe = acc is not None
    b = xs.shape[0]
    nt = xs.shape[1] + (0 if with_moe else ctx.shape[1])
    n_tiles = nt // TM
    ct = n_tiles - 1
    ch = lambda s: _rnn_chunk(s, ct, False)
    tile = lambda w: pl.BlockSpec((None, TM, w), lambda i, s: (i, ch(s), 0))
    mod_spec = pl.BlockSpec((None, None, 1, N_MOD * D), lambda i, s: (i, ch(s) // ct, 0, 0))
    tab_spec = pl.BlockSpec((TM, LANES), lambda i, s: (ch(s), 0))
    if with_moe:
        in_specs = [tile(D), pl.BlockSpec((None, TM * SUBLANES, LANES), lambda i, s: (i, ch(s), 0)), mod_spec]
        args = [xs, acc, modsel_prev]
    else:
        in_specs = [pl.BlockSpec((None, TM, D), lambda i, s: (i, jnp.minimum(ch(s), ct - 1), 0)),
                    pl.BlockSpec((None, TM, D), lambda i, s: (i, 0, 0))]
        args = [xs, ctx]
    in_specs += [mod_spec, pl.BlockSpec((1, D), lambda i, s: (0, 0)), tab_spec, tab_spec, tab_spec,
                 pl.BlockSpec((D, W_A_COLS), lambda i, s: (0, 0), pipeline_mode=pl.Buffered(1))]
    in_specs += _rnn_param_specs()
    args += [modsel, norm, *tabs, w_a, *rnn_prm]
    widths = (ZA_W, Q_W, KV2_W, G_RX, D, RNN_W)
    out_shape = [jax.ShapeDtypeStruct((b, nt, w), bf16) for w in widths]
    out_specs = [tile(w) for w in widths]
    if with_moe:
        out_shape = [jax.ShapeDtypeStruct((b, nt, D), f32)] + out_shape
        out_specs = [tile(D)] + out_specs
    return pl.pallas_call(
        functools.partial(_proj_a_kernel, with_moe=with_moe, ct=ct),
        out_shape=out_shape, grid=(b, n_tiles), in_specs=in_specs, out_specs=out_specs,
        scratch_shapes=_RNN_SCRATCH + [pltpu.VMEM((BF16_ROWS, RNN_W), f32)],
        compiler_params=_cparams("arbitrary", "arbitrary"), name="proj_a_rnn_fwd",
    )(*args)


def _proj_b(h, w_b, zrx, rnn_prm):
    b, nt, _ = h.shape
    n_tiles = nt // TM
    ct = n_tiles - 1
    per = TM // BF16_ROWS
    ch = lambda s: _rnn_chunk(s, ct, True)
    tile = lambda w: pl.BlockSpec((None, TM, w), lambda i, s: (i, ch(s), 0))
    halo = pl.BlockSpec((None, BF16_ROWS, RNN_W),
                        lambda i, s: (i, jnp.minimum((ch(s) + 1) * per, nt // BF16_ROWS - 1), 0))
    widths = (G_RY, G_G, RNN_W)
    return pl.pallas_call(
        functools.partial(_proj_b_kernel, ct=ct),
        out_shape=[jax.ShapeDtypeStruct((b, nt, w), bf16) for w in widths],
        grid=(b, n_tiles),
        in_specs=[tile(D), pl.BlockSpec((D, G_RY + G_G), lambda i, s: (0, 0), pipeline_mode=pl.Buffered(1)),
                  tile(RNN_W), halo] + _rnn_param_specs(),
        out_specs=[tile(w) for w in widths],
        scratch_shapes=_RNN_SCRATCH,
        compiler_params=_cparams("arbitrary", "arbitrary"), name="proj_b_rnn_bwd",
    )(h, w_b, zrx, zrx, *rnn_prm)


def _attn_kernel(sink_ref, q_ref, kvp_ref, kvc_ref, kvn_ref, kvx_ref, o_ref, *, n_lat_blocks, n_ctx):
    j = pl.program_id(1)
    group = N_HEADS // N_KV
    nlb = n_lat_blocks
    lo = lax.broadcasted_iota(i32, (QB, LANES), 1) < HEAD_DIM
    diff = lax.broadcasted_iota(i32, (QB, QB), 1) - lax.broadcasted_iota(i32, (QB, QB), 0)
    lat = j < nlb
    ok = (diff >= jnp.where(lat & (j >= 1), 0, QB),
          diff >= jnp.where(lat, -QB, QB),
          -diff >= jnp.where(lat & (j <= nlb - 2), 0, QB))
    zero = jnp.zeros((QB, LANES), bf16)
    scores, values = [], []
    for g in range(N_KV):
        sl = slice(g * LANES, (g + 1) * LANES)
        vsl = slice(K2_W + g * LANES, K2_W + (g + 1) * LANES)
        k = jnp.concatenate([r[:, sl] for r in (kvp_ref, kvc_ref, kvn_ref, kvx_ref)], axis=0)
        v = jnp.concatenate([r[:, vsl] for r in (kvp_ref, kvc_ref, kvn_ref, kvx_ref)], axis=0)
        parts = []
        for pr in range(group // 2):
            qp = q_ref[:, (g * (group // 2) + pr) * LANES:(g * (group // 2) + pr + 1) * LANES]
            parts += [jnp.where(lo, qp, zero), jnp.where(lo, zero, qp)]
        q4 = jnp.concatenate(parts, axis=0)
        scores.append(lax.dot_general(q4, k, (((1,), (1,)), ((), ())), preferred_element_type=f32))
        values.append(v)
    for g in range(N_KV):
        s, v = scores[g], values[g]
        ps, inv = [], []
        for hh in range(group):
            sh = s[hh * QB:(hh + 1) * QB]
            sm = jnp.concatenate([jnp.where(ok[kb], sh[:, kb * QB:(kb + 1) * QB], NEG_INF) for kb in range(3)]
                                 + [sh[:, 3 * QB:]], axis=1)
            sink = sink_ref[0, g * group + hh]
            m = jnp.maximum(jnp.max(sm, axis=1, keepdims=True), sink)
            p = jnp.exp(sm - m)
            inv.append(1.0 / (jnp.sum(p, axis=1, keepdims=True) + jnp.exp(sink - m)))
            ps.append(p.astype(bf16))
        o = jnp.dot(jnp.concatenate(ps, axis=0), v, preferred_element_type=f32) * jnp.concatenate(inv, axis=0)
        for pr in range(group // 2):
            pair = jnp.where(lo, o[(2 * pr) * QB:(2 * pr + 1) * QB], o[(2 * pr + 1) * QB:(2 * pr + 2) * QB])
            c0 = (g * (group // 2) + pr) * LANES
            o_ref[:, c0:c0 + LANES] = pair.astype(bf16)


def _attention(sink, zq, zkv, n_lat, n_qblocks):
    b, nt, _ = zq.shape
    n_ctx = nt - n_lat
    nlb = n_lat // QB
    cidx = n_lat // n_ctx
    win = lambda d: pl.BlockSpec((None, QB, KV2_W), lambda i, j: (i, jnp.clip(j + d, 0, nlb - 1), 0))
    ctx = pl.BlockSpec((None, n_ctx, KV2_W), lambda i, j: (i, cidx, 0))
    return pl.pallas_call(
        functools.partial(_attn_kernel, n_lat_blocks=nlb, n_ctx=n_ctx),
        out_shape=jax.ShapeDtypeStruct((b, n_qblocks * QB, Q_W), bf16),
        grid=(b, n_qblocks),
        in_specs=[pl.BlockSpec(memory_space=pltpu.SMEM),
                  pl.BlockSpec((None, QB, Q_W), lambda i, j: (i, j, 0)),
                  win(-1), win(0), win(1), ctx],
        out_specs=pl.BlockSpec((None, QB, Q_W), lambda i, j: (i, j, 0)),
        compiler_params=_cparams("arbitrary", "arbitrary"), name="attention",
    )(sink, zq, zkv, zkv, zkv, zkv)


def _rnn_chunk(s, ct, rev):
    return jnp.where(s == 0, ct, (ct - s) if rev else (s - 1))


def _block_scan(a, b, h_in, rowi, rev):
    n = SUBLANES
    order = range(n - 1, -1, -1) if rev else range(n)
    hs = [None] * n
    ps = [None] * n
    prev = None
    for j in order:
        if prev is None:
            hs[j], ps[j] = b[j], a[j]
        else:
            hs[j], ps[j] = a[j] * hs[prev] + b[j], a[j] * ps[prev]
        prev = j
    he, pe = hs[prev], ps[prev]
    for sh in (1, 2, 4):
        rs = n - sh if rev else sh
        msk = (rowi < n - sh) if rev else (rowi >= sh)
        he, pe = (jnp.where(msk, he + pe * pltpu.roll(he, rs, axis=0), he),
                  jnp.where(msk, pe * pltpu.roll(pe, rs, axis=0), pe))
    e = he + pe * h_in
    carry = jnp.where((rowi == n - 1) if rev else (rowi == 0), h_in, pltpu.roll(e, n - 1 if rev else 1, axis=0))
    out = [hs[j] + ps[j] * carry for j in range(n)]
    last = e[0:1, :] if rev else e[n - 1:n, :]
    return out, jnp.broadcast_to(last, (n, LANES))


RNN_LT = RNN_W // LANES


def _rnn_fill(u, hal, ubuf, rev, c0=0):
    for k in range(u.shape[1] // LANES):
        sl = slice(k * LANES, (k + 1) * LANES)
        if rev:
            ubuf[c0 + k, 0:TM, :] = u[:, sl]
            ubuf[c0 + k, TM:TM + BF16_ROWS, :] = hal[:, sl]
        else:
            ubuf[c0 + k, 0:BF16_ROWS, :] = hal[:, sl]
            ubuf[c0 + k, BF16_ROWS:BF16_ROWS + TM, :] = u[:, sl]


def _rnn_tile(c, prm, ubuf, obuf, hcar, rev):
    cw_ref, cb_ref, wg_ref, ba_ref, bx_ref, lam_ref = prm
    base = 0 if rev else BF16_ROWS - (RNN_CONV_K - 1)
    rowi = lax.broadcasted_iota(i32, (SUBLANES, LANES), 0)
    blk_rows = SUBLANES * SUBLANES
    n_blk = TM // blk_rows
    sl = slice(c * LANES, (c + 1) * LANES)
    taps = [cw_ref[k:k + 1, sl] for k in range(RNN_CONV_K)]
    xt = []
    for blk in range(n_blk):
        for j in range(SUBLANES):
            r0 = base + blk * blk_rows + j
            acc = cb_ref[:, sl] + ubuf[c, pl.ds(r0, SUBLANES, stride=SUBLANES), :] * taps[0]
            for k in range(1, RNN_CONV_K):
                acc = acc + ubuf[c, pl.ds(r0 + k, SUBLANES, stride=SUBLANES), :] * taps[k]
            xt.append(acc)
    xp = jnp.concatenate(xt, axis=0)
    g = jnp.dot(xp.astype(bf16), wg_ref[c], preferred_element_type=f32)
    t_r = jnp.tanh(g[:, :RNN_BLK] + 0.5 * ba_ref[:, sl])
    t_i = jnp.tanh(g[:, RNN_BLK:] + 0.5 * bx_ref[:, sl])
    c4 = (-0.5 * RG_C) * jnp.logaddexp(-lam_ref[:, sl], 0.0)
    log_a = c4 * t_r + c4
    a = jnp.exp(log_a)
    y = -jnp.tanh(log_a) * (a * a + 1.0)
    xh = 0.5 * xp
    bb = jnp.where(y > 0.0, y * lax.rsqrt(y), 0.0) * (xh * t_i + xh)
    h_in = hcar[:, sl]
    for blk in (range(n_blk - 1, -1, -1) if rev else range(n_blk)):
        rows = lambda j: slice((blk * SUBLANES + j) * SUBLANES, (blk * SUBLANES + j + 1) * SUBLANES)
        hs, h_in = _block_scan([a[rows(j)] for j in range(SUBLANES)], [bb[rows(j)] for j in range(SUBLANES)],
                               h_in, rowi, rev)
        for j in range(SUBLANES):
            obuf[c, pl.ds(blk * blk_rows + j, SUBLANES, stride=SUBLANES), :] = hs[j]
    hcar[:, sl] = h_in


def _rnn_states(obuf):
    return jnp.concatenate([obuf[c] for c in range(RNN_LT)], axis=1)


_RNN_SCRATCH = [pltpu.VMEM((RNN_LT, TM + BF16_ROWS, LANES), f32),
                pltpu.VMEM((RNN_LT, TM, LANES), f32),
                pltpu.VMEM((SUBLANES, RNN_W), f32)]


def _rnn_param_specs():
    full = lambda shape: pl.BlockSpec(shape, lambda i, s: (0,) * len(shape))
    return [full((RNN_CONV_K, RNN_W)), full((1, RNN_W)), full((RNN_BLOCKS, RNN_BLK, 2 * RNN_BLK)),
            full((1, RNN_W)), full((1, RNN_W)), full((1, RNN_W))]


def _merge_kernel(*refs, ct, split):
    (za_ref, zap_ref, zan_ref, att_ref, hf_ref, hb_ref, ry_ref, zg_ref, mod_ref, ca_ref, wc_ref,
     wa_ref, wr_ref, wo_ref, n2_ref, wrt_ref, shift_ref, xo_ref, h2_ref, aff_ref) = refs[2 if split else 1:]
    t = pl.program_id(1)
    if split:
        x_in = jnp.where(t == ct, refs[1][...], refs[0][...])
    else:
        x_in = refs[0][...]
    cw = CONV_W
    cu_b = za_ref[:, 0:cw]
    sh = jnp.dot(shift_ref[...], cu_b, preferred_element_type=f32)
    prev_ok = (t != 0) & (t != ct)
    next_ok = (t != ct - 1) & (t != ct)
    cu_p = jnp.where(prev_ok, zap_ref[BF16_ROWS - 1:BF16_ROWS, 0:cw].astype(f32), 0.0)
    cu_n = jnp.where(next_ok, zan_ref[0:1, 0:cw].astype(f32), 0.0)
    rowi = lax.broadcasted_iota(i32, (TM, cw), 0)
    cu_prev = jnp.where(rowi == 0, cu_p, sh[:TM])
    cu_next = jnp.where(rowi == TM - 1, cu_n, sh[TM:])
    y = cu_prev * ca_ref[0:1, :] + cu_b.astype(f32) * ca_ref[1:2, :] + cu_next * ca_ref[2:3, :]
    cnv = jnp.dot((za_ref[:, cw:2 * cw].astype(f32) * y).astype(bf16), wc_ref[...], preferred_element_type=f32)
    att = jnp.dot(att_ref[...], wa_ref[...], preferred_element_type=f32)
    rec = hf_ref[...].astype(f32) + hb_ref[...].astype(f32)
    rnn = jnp.dot((ry_ref[...].astype(f32) * rec).astype(bf16), wr_ref[...], preferred_element_type=f32)
    mix2 = None
    for k, br in enumerate((cnv, att, rnn)):
        term = jnp.tanh(zg_ref[:, k * D:(k + 1) * D].astype(f32)) * br + br
        mix2 = term if mix2 is None else mix2 + term
    x = x_in + mod_ref[:, 2 * D:3 * D] * jnp.dot(mix2.astype(bf16), wo_ref[...], preferred_element_type=f32)
    xo_ref[...] = x
    h2 = _rms_mod(x, n2_ref[...], mod_ref[:, 4 * D:5 * D], mod_ref[:, 3 * D:4 * D])
    _rows_to_tiles(h2_ref, h2, TM)
    logits = lax.dot_general(wrt_ref[...], h2.astype(bf16), (((1,), (1,)), ((), ())), preferred_element_type=f32)
    e = jnp.exp(logits - jnp.max(logits, axis=0, keepdims=True))
    aff_ref[...] = e / jnp.sum(e, axis=0, keepdims=True)


def _row_shift_operator():
    r = jnp.arange(2 * TM)[:, None]
    c = jnp.arange(TM)[None, :]
    return (c == jnp.where(r < TM, r - 1, r - TM + 1)).astype(bf16)


def _merge(xs, ctx, za, att, hf, hb, zry, zg, modsel, conv_a, wc, wa, wr, wo, norm2, wrt, n_tiles_eff):
    b, nt, _ = za.shape
    ct = nt // TM - 1
    per = TM // BF16_ROWS
    split = ctx is not None
    tile = lambda w: pl.BlockSpec((None, TM, w), lambda i, t: (i, t, 0))
    full = lambda shape: pl.BlockSpec(shape, lambda i, t: (0,) * len(shape), pipeline_mode=pl.Buffered(1))
    if split:
        x_specs = [pl.BlockSpec((None, TM, D), lambda i, t: (i, jnp.minimum(t, ct - 1), 0)),
                   pl.BlockSpec((None, TM, D), lambda i, t: (i, 0, 0))]
        x_args = [xs, ctx]
    else:
        x_specs = [tile(D)]
        x_args = [xs]
    return pl.pallas_call(
        functools.partial(_merge_kernel, ct=ct, split=split),
        out_shape=[jax.ShapeDtypeStruct((b, n_tiles_eff * TM, D), f32),
                   jax.ShapeDtypeStruct((b, n_tiles_eff * TM * SUBLANES, LANES), f32),
                   jax.ShapeDtypeStruct((b, N_EXPERTS, n_tiles_eff * TM), f32)],
        grid=(b, n_tiles_eff),
        in_specs=x_specs + [tile(ZA_W),
                  pl.BlockSpec((None, BF16_ROWS, ZA_W), lambda i, t: (i, jnp.maximum(t * per - 1, 0), 0)),
                  pl.BlockSpec((None, BF16_ROWS, ZA_W),
                               lambda i, t: (i, jnp.minimum((t + 1) * per, nt // BF16_ROWS - 1), 0)),
                  tile(Q_W), tile(RNN_W), tile(RNN_W), tile(RNN_W), tile(G_G),
                  pl.BlockSpec((None, None, 1, N_MOD * D), lambda i, t: (i, t // ct, 0, 0)),
                  full((3, CONV_W)), full((CONV_W, D)), full((Q_W, D)), full((RNN_W, D)), full((D, D)),
                  full((1, D)), full((N_EXPERTS, D)), full((2 * TM, TM))],
        out_specs=[tile(D), pl.BlockSpec((None, TM * SUBLANES, LANES), lambda i, t: (i, t, 0)),
                   pl.BlockSpec((None, N_EXPERTS, TM), lambda i, t: (i, 0, t))],
        compiler_params=_cparams("arbitrary", "arbitrary"), name="merge",
    )(*x_args, za, za, za, att, hf, hb, zry, zg, modsel, conv_a, wc, wa, wr, wo, norm2, wrt, _row_shift_operator())


ROUTE_QW = 64
ROUTE_TOK_SHIFT = 6
CUM_BLK = 256


def _cumsum_lanes(x, n):
    blk = min(CUM_BLK, n)
    tri = (lax.broadcasted_iota(i32, (blk, blk), 0) <= lax.broadcasted_iota(i32, (blk, blk), 1)).astype(bf16)
    carry = jnp.zeros((x.shape[0], 1), f32)
    outs = []
    for j in range(n // blk):
        c = jnp.dot(x[:, j * blk:(j + 1) * blk].astype(bf16), tri, preferred_element_type=f32) + carry
        outs.append(c)
        carry = c[:, blk - 1:blk]
    return jnp.concatenate(outs, axis=1)


def _route_kernel(aff_ref, idx_ref, val_ref, pos_s, *, n, cap):
    aff = aff_ref[...]
    thr = jnp.zeros((N_EXPERTS, 1), i32)
    for bit in range(30, -1, -1):
        cand = thr | (1 << bit)
        cnt = jnp.sum(jnp.where(aff >= lax.bitcast_convert_type(cand, f32), 1.0, 0.0), axis=1, keepdims=True)
        thr = jnp.where(cnt >= float(cap), cand, thr)
    thr_f = lax.bitcast_convert_type(thr, f32)
    gt = aff > thr_f
    eq = aff == thr_f
    need = float(cap) - jnp.sum(jnp.where(gt, 1.0, 0.0), axis=1, keepdims=True)
    sel = gt | (eq & (_cumsum_lanes(eq.astype(f32), n) <= need))
    pos_s[...] = jnp.where(sel, _cumsum_lanes(sel.astype(f32), n).astype(i32) - 1, -1)

    qw = min(ROUTE_QW, cap)
    shift = qw.bit_length() - 1
    tok = lax.broadcasted_iota(i32, (1, n), 1)
    tok_hi = (tok >> ROUTE_TOK_SHIFT).astype(f32)
    tok_lo = (tok & ((1 << ROUTE_TOK_SHIFT) - 1)).astype(f32)
    hi_iota = lax.broadcasted_iota(i32, (SUBLANES, n), 0)
    lo_iota = lax.broadcasted_iota(i32, (qw, n), 0)

    def body(e, carry):
        pos = pos_s[pl.ds(e, 1), :]
        a0 = aff_ref[pl.ds(e, 1), :]
        a_h = a0.astype(bf16).astype(f32)
        a_m = (a0 - a_h).astype(bf16).astype(f32)
        a_l = ((a0 - a_h) - a_m).astype(bf16).astype(f32)
        in_hi = (pos >> shift) == hi_iota
        lhs = jnp.concatenate([jnp.where(in_hi, r, 0.0) for r in (tok_hi, tok_lo, a_h, a_m, a_l)], axis=0)
        onehot = jnp.where((pos & (qw - 1)) == lo_iota, 1.0, 0.0).astype(bf16)
        res = lax.dot_general(lhs.astype(bf16), onehot, (((1,), (1,)), ((), ())), preferred_element_type=f32)
        r_hi, r_lo, v_h, v_m, v_l = (res[k * SUBLANES:(k + 1) * SUBLANES] for k in range(5))
        idx_ref[e] = (r_hi * float(1 << ROUTE_TOK_SHIFT) + r_lo).astype(i32)[:cap // qw]
        val_ref[e] = ((v_h + v_m) + v_l)[:cap // qw]
        return carry

    lax.fori_loop(0, N_EXPERTS, body, 0)


def _route(aff_t, n, cap, lane_block):
    b = aff_t.shape[0]
    qw = min(ROUTE_QW, cap)
    assert cap % qw == 0 and cap // qw <= SUBLANES and qw & (qw - 1) == 0
    idx, val = pl.pallas_call(
        functools.partial(_route_kernel, n=n, cap=cap),
        out_shape=[jax.ShapeDtypeStruct((b, N_EXPERTS, cap // qw, qw), i32),
                   jax.ShapeDtypeStruct((b, N_EXPERTS, cap // qw, qw), f32)],
        grid=(b,),
        in_specs=[pl.BlockSpec((None, N_EXPERTS, n), lambda i: (i, 0, lane_block))],
        out_specs=[pl.BlockSpec((None, N_EXPERTS, cap // qw, qw), lambda i: (i, 0, 0, 0)),
                   pl.BlockSpec((None, N_EXPERTS, cap // qw, qw), lambda i: (i, 0, 0, 0))],
        scratch_shapes=[pltpu.VMEM((N_EXPERTS, n), i32)],
        compiler_params=_cparams("arbitrary"), name="route",
    )(aff_t)
    return idx.reshape(b, N_EXPERTS, cap), val.reshape(b, N_EXPERTS, cap)


def _gather_kernel(idx_ref, h_ref, tok_ref, tbuf, *, cap):
    base = (pl.program_id(0) * N_EXPERTS + pl.program_id(1)) * cap
    sp = cap + SUBLANES
    for p in range(cap):
        i = idx_ref[base + p]
        tbuf[pl.ds(p, SUBLANES, stride=sp), :] = h_ref[pl.ds(pl.multiple_of(i * SUBLANES, SUBLANES), SUBLANES), :]
    tok_ref[...] = jnp.concatenate([tbuf[pl.ds(c * sp, cap), :] for c in range(D // LANES)], axis=1).astype(bf16)


def _gather(idx, h2t, cap):
    b, rows, _ = h2t.shape
    return pl.pallas_call(
        functools.partial(_gather_kernel, cap=cap),
        out_shape=jax.ShapeDtypeStruct((b, N_EXPERTS, cap, D), bf16),
        grid_spec=pltpu.PrefetchScalarGridSpec(
            num_scalar_prefetch=1, grid=(b, N_EXPERTS),
            in_specs=[pl.BlockSpec((None, rows, LANES), lambda i, e, idx: (i, 0, 0))],
            out_specs=pl.BlockSpec((None, None, cap, D), lambda i, e, idx: (i, e, 0, 0)),
            scratch_shapes=[pltpu.VMEM((SUBLANES * (cap + SUBLANES), LANES), f32)]),
        compiler_params=_cparams("arbitrary", "arbitrary"), name="moe_gather",
    )(idx, h2t)


def _cast3_kernel(a_ref, b_ref, c_ref, ao_ref, bo_ref, co_ref):
    ao_ref[...] = a_ref[...].astype(bf16)
    bo_ref[...] = b_ref[...].astype(bf16)
    co_ref[...] = c_ref[...].astype(bf16)


def _expert0_bf16(wg, wu, wd, layer):
    spec = pl.BlockSpec((None, None, D, EXPERT_FF), lambda i: (layer, 0, 0, 0))
    ospec = pl.BlockSpec((D, EXPERT_FF), lambda i: (0, 0))
    return pl.pallas_call(
        _cast3_kernel, out_shape=[jax.ShapeDtypeStruct((D, EXPERT_FF), bf16)] * 3, grid=(1,),
        in_specs=[spec] * 3, out_specs=[ospec] * 3, compiler_params=_cparams("arbitrary"), name="expert0_cast",
    )(wg, wu, wd)


def _ffn_kernel(tok_ref, g0_ref, u0_ref, d0_ref, gq_ref, uq_ref, dq_ref, o_ref, w_even, w_odd):
    e = pl.program_id(0)
    i = pl.program_id(1)

    @pl.when((e == 0) & (i == 0))
    def _():
        w_even[0] = g0_ref[...]
        w_even[1] = u0_ref[...]
        w_even[2] = d0_ref[...]

    rows = gq_ref.shape[0]
    r0 = pl.multiple_of(i * rows, rows)

    def step(use, fill):
        t = tok_ref[...]
        g = jnp.dot(t, use[0], preferred_element_type=f32)
        for k, q_ref in enumerate((gq_ref, uq_ref, dq_ref)):
            fill[k, pl.ds(r0, rows), :] = q_ref[...].astype(bf16)
        u = jnp.dot(t, use[1], preferred_element_type=f32)
        a = (g * _sigmoid(g) * u).astype(bf16)
        o_ref[...] = jnp.dot(a, use[2], preferred_element_type=f32)

    @pl.when(e % 2 == 0)
    def _():
        step(w_even, w_odd)

    @pl.when(e % 2 == 1)
    def _():
        step(w_odd, w_even)


def _ffn(tok, wg, wu, wd, layer):
    b, _, cap, _ = tok.shape
    assert D == EXPERT_FF and D % b == 0 and (D // b) % BF16_ROWS == 0
    rows = D // b
    w0 = _expert0_bf16(wg, wu, wd, layer)
    w0spec = pl.BlockSpec((D, EXPERT_FF), lambda e, i: (0, 0), pipeline_mode=pl.Buffered(1))
    qspec = pl.BlockSpec((None, None, rows, D), lambda e, i: (layer, jnp.minimum(e + 1, N_EXPERTS - 1), i, 0))
    return pl.pallas_call(
        _ffn_kernel,
        out_shape=jax.ShapeDtypeStruct((b, N_EXPERTS, cap, D), f32),
        grid=(N_EXPERTS, b),
        in_specs=[pl.BlockSpec((None, None, cap, D), lambda e, i: (i, e, 0, 0)),
                  w0spec, w0spec, w0spec, qspec, qspec, qspec],
        out_specs=pl.BlockSpec((None, None, cap, D), lambda e, i: (i, e, 0, 0)),
        scratch_shapes=[pltpu.VMEM((3, D, EXPERT_FF), bf16), pltpu.VMEM((3, D, EXPERT_FF), bf16)],
        compiler_params=_cparams("arbitrary", "arbitrary"), name="moe_ffn",
    )(tok, *w0, wg, wu, wd)


SCATTER_UNROLL = 8


def _scatter_kernel(idx_ref, val_ref, o_ref, acc_ref, tbuf, *, cap):
    e = pl.program_id(1)
    base = (pl.program_id(0) * N_EXPERTS + e) * cap
    sp = cap + SUBLANES

    @pl.when(e == 0)
    def _():
        acc_ref[...] = jnp.zeros_like(acc_ref)

    for c in range(D // LANES):
        tbuf[pl.ds(c * sp, cap), :] = o_ref[:, c * LANES:(c + 1) * LANES]
    u = min(SCATTER_UNROLL, cap)
    for p0 in range(0, cap, u):
        pend = []
        for p in range(p0, p0 + u):
            r0 = pl.multiple_of(idx_ref[base + p] * SUBLANES, SUBLANES)
            pend.append((r0, acc_ref[pl.ds(r0, SUBLANES), :] + tbuf[pl.ds(p, SUBLANES, stride=sp), :] * val_ref[base + p]))
        for r0, v in pend:
            acc_ref[pl.ds(r0, SUBLANES), :] = v


def _scatter(idx, val, o, rows):
    b, _, cap, _ = o.shape
    return pl.pallas_call(
        functools.partial(_scatter_kernel, cap=cap),
        out_shape=jax.ShapeDtypeStruct((b, rows, LANES), f32),
        grid_spec=pltpu.PrefetchScalarGridSpec(
            num_scalar_prefetch=2, grid=(b, N_EXPERTS),
            in_specs=[pl.BlockSpec((None, None, cap, D), lambda i, e, idx, val: (i, e, 0, 0))],
            out_specs=pl.BlockSpec((None, rows, LANES), lambda i, e, idx, val: (i, 0, 0)),
            scratch_shapes=[pltpu.VMEM((SUBLANES * (cap + SUBLANES), LANES), f32)]),
        compiler_params=_cparams("arbitrary", "arbitrary"), name="moe_scatter",
    )(idx, val, o)


def _moe(aff_t, h2t, wg, wu, wd, layer, n_lat, n_ctx):
    idx, val = _route(aff_t, n_lat, CAP_FACTOR * n_lat // N_EXPERTS, 0)
    if n_ctx:
        idx_c, val_c = _route(aff_t, n_ctx, CAP_FACTOR * n_ctx // N_EXPERTS, n_lat // n_ctx)
        idx = jnp.concatenate([idx, idx_c + n_lat], axis=-1)
        val = jnp.concatenate([val, val_c], axis=-1)
    cap = idx.shape[-1]
    idx = idx.reshape(-1)
    val = val.reshape(-1)
    tok = _gather(idx, h2t, cap)
    o = _ffn(tok, wg, wu, wd, layer)
    return _scatter(idx, val, o, h2t.shape[1])


def _final_kernel(x_ref, acc_ref, mod_ref, n_ref, o_ref):
    x = x_ref[...] + mod_ref[:, 5 * D:6 * D] * _tiles_to_rows(acc_ref, TM)
    o_ref[...] = (x * lax.rsqrt(jnp.mean(x * x, axis=-1, keepdims=True) + EPS)) * n_ref[...]


def _final(xs, acc, modsel, norm, n_lat):
    b = xs.shape[0]
    return pl.pallas_call(
        _final_kernel,
        out_shape=jax.ShapeDtypeStruct((b, n_lat, D), f32),
        grid=(b, n_lat // TM),
        in_specs=[pl.BlockSpec((None, TM, D), lambda i, t: (i, t, 0)),
                  pl.BlockSpec((None, TM * SUBLANES, LANES), lambda i, t: (i, t, 0)),
                  pl.BlockSpec((None, None, 1, N_MOD * D), lambda i, t: (i, 0, 0, 0)),
                  pl.BlockSpec((1, D), lambda i, t: (0, 0))],
        out_specs=pl.BlockSpec((None, TM, D), lambda i, t: (i, t, 0)),
        compiler_params=_cparams("arbitrary", "arbitrary"), name="final_norm",
    )(xs, acc, modsel, norm)


def _rope_tables(n_lat, n_ctx):
    pos = jnp.arange(n_lat, dtype=f32)
    rows = jnp.floor(pos / GRID_W)
    cols = pos - rows * GRID_W
    half = HEAD_DIM // 4
    inv = 1.0 / (ROPE_BASE ** (jnp.arange(0, 2 * half, 2, dtype=f32) / (2 * half)))
    lane = jnp.arange(LANES)
    d = lane % HEAD_DIM
    ang = jnp.where((d < 2 * half)[None, :], rows[:, None], cols[:, None]) * inv[d % half][None, :]
    first = ((d % (2 * half)) < half)[None, :]
    cos = jnp.cos(ang)
    sin = jnp.sin(ang)
    s1 = jnp.where(first, -sin, 0.0)
    s2 = jnp.where(first, 0.0, sin)
    pad = lambda t, v: jnp.concatenate([t, jnp.full((n_ctx, LANES), v, f32)], axis=0)
    return pad(cos, 1.0), pad(s1, 0.0), pad(s2, 0.0)


def _in_weights_bf16(w):
    scale = jnp.concatenate([jnp.ones((IN_TOTAL - G_G,), f32), jnp.full((G_G,), 0.5, f32)])
    return (w * scale[None, :]).astype(bf16)


def kernel(x, c, ctx, c_ctx, norm1, norm2, w_mod, b_mod, w_in, conv_a, w_conv_out, attn_sink, w_attn_out, rnn_conv_w,
           rnn_conv_b, rnn_w_a, rnn_b_a, rnn_w_x, rnn_b_x, rnn_lam, w_rnn_out, w_o, w_router, w_e_gate, w_e_up,
           w_e_down, final_norm):
    b, n_lat, d = x.shape
    n_ctx = ctx.shape[1]
    depth = w_mod.shape[0]
    assert d == D and n_ctx == TM and n_lat % TM == 0 and b < SUBLANES
    nt = n_lat + n_ctx
    ct = nt // TM - 1

    cstack = jnp.concatenate([c, c_ctx[None, :], jnp.zeros((SUBLANES - b - 1, D), f32)], axis=0)
    mod = _modulation(cstack, w_mod, b_mod)
    tabs = _rope_tables(n_lat, n_ctx)

    xs = x
    acc = None
    modsel_prev = None
    for l in range(depth):
        ctx_out = l < depth - 1
        ctx_sep = ctx if l == 0 else None
        modsel = jnp.stack([mod[l, :b], jnp.broadcast_to(mod[l, b], (b, N_MOD * D))], axis=1)[:, :, None, :]
        rnn_prm = []
        for dr in range(2):
            wg = (0.5 * jnp.concatenate([rnn_w_a[l, dr], rnn_w_x[l, dr]], axis=-1)).astype(bf16)
            rnn_prm.append((rnn_conv_w[l, dr], rnn_conv_b[l, dr][None, :], wg, rnn_b_a[l, dr][None, :],
                            rnn_b_x[l, dr][None, :], rnn_lam[l, dr][None, :]))
        w_all = _in_weights_bf16(w_in[l])
        outs = _proj_a(xs, ctx_sep, acc, modsel_prev, modsel, norm1[l][None, :], tabs, w_all[:, :W_A_COLS],
                       rnn_prm[0])
        if acc is not None:
            xs, outs = outs[0], outs[1:]
        za, zq, zkv, zrx, hn, hf = outs
        zry, zg, hb = _proj_b(hn, w_all[:, W_A_COLS:], zrx, rnn_prm[1])
        att = _attention(attn_sink[l][None, :], zq, zkv, n_lat, (nt if ctx_out else n_lat) // QB)
        xs, h2t, aff_t = _merge(xs, ctx_sep, za, att, hf, hb, zry, zg, modsel, conv_a[l],
                                w_conv_out[l].astype(bf16),
                                w_attn_out[l].astype(bf16), (0.5 * w_rnn_out[l]).astype(bf16),
                                (0.5 * w_o[l]).astype(bf16),
                                norm2[l][None, :], w_router[l].T.astype(bf16), (ct + 1) if ctx_out else ct)
        acc = _moe(aff_t, h2t, w_e_gate, w_e_up, w_e_down, l, n_lat, n_ctx if ctx_out else 0)
        modsel_prev = modsel
    return _final(xs, acc, modsel_prev, final_norm[None, :], n_lat)
```

```python
import functools

import jax
import jax.numpy as jnp
from jax import lax
from jax.experimental import pallas as pl
from jax.experimental.pallas import tpu as pltpu

f32 = jnp.float32
bf16 = jnp.bfloat16
i32 = jnp.int32

D = 1024
EPS = 1e-6
CONV_W = 512
N_HEADS = 8
N_KV = 2
HEAD_DIM = 64
Q_W = N_HEADS * HEAD_DIM
KV_W = N_KV * HEAD_DIM
WINDOW = 128
GRID_W = 64
ROPE_BASE = 10000.0
NEG_INF = -1e30
RNN_W = 1024
RNN_BLOCKS = 8
RNN_BLK = RNN_W // RNN_BLOCKS
RNN_CONV_K = 4
RG_C = 8.0
N_EXPERTS = 16
EXPERT_FF = 1024
CAP_FACTOR = 2
N_MOD = 6

LANES = 128
SUBLANES = 8
BF16_ROWS = 16
VMEM_LIMIT = 56 * 1024 * 1024

TM = 256
QB = 128
K2_W = 2 * KV_W

G_A = 3 * CONV_W
G_QKV = Q_W + 2 * KV_W
G_RX = RNN_W
G_RY = RNN_W
G_G = 3 * D
IN_COLS = (G_A, G_QKV, G_RX, G_RY, G_G)
IN_TOTAL = sum(IN_COLS)
KV2_W = 2 * K2_W
ZA_W = 2 * CONV_W
OUT_COLS = (ZA_W, Q_W, KV2_W, G_RX, G_RY, G_G)


def _cparams(*sem):
    return pltpu.CompilerParams(dimension_semantics=sem, vmem_limit_bytes=VMEM_LIMIT)


def _sigmoid(x):
    return 0.5 * jnp.tanh(0.5 * x) + 0.5


def _rms_mod(x, g, sc, sh):
    y = x * lax.rsqrt(jnp.mean(x * x, axis=-1, keepdims=True) + EPS)
    return (y * g) * (1.0 + sc) + sh


def _mod_kernel(c_ref, w_ref, b_ref, o_ref):
    c = c_ref[...]
    s = (c * _sigmoid(c)).astype(bf16)
    o_ref[...] = jnp.dot(s, w_ref[...].astype(bf16), preferred_element_type=f32) + b_ref[...]


def _modulation(cstack, w_mod, b_mod):
    depth = w_mod.shape[0]
    tn = 1536
    return pl.pallas_call(
        _mod_kernel,
        out_shape=jax.ShapeDtypeStruct((depth, SUBLANES, N_MOD * D), f32),
        grid=(depth, N_MOD * D // tn),
        in_specs=[
            pl.BlockSpec((SUBLANES, D), lambda l, j: (0, 0)),
            pl.BlockSpec((None, D, tn), lambda l, j: (l, 0, j)),
            pl.BlockSpec((None, 1, tn), lambda l, j: (l, 0, j)),
        ],
        out_specs=pl.BlockSpec((None, SUBLANES, tn), lambda l, j: (l, 0, j)),
        compiler_params=_cparams("arbitrary", "arbitrary"),
        name="modulation",
    )(cstack, w_mod, b_mod.reshape(depth, 1, N_MOD * D))


def _tiles_to_rows(ref, rows):
    return jnp.concatenate([ref[pl.ds(c, rows, stride=SUBLANES), :] for c in range(D // LANES)], axis=1)


def _rows_to_tiles(ref, val, rows):
    per = D // LANES
    for r in range(rows // SUBLANES):
        for c in range(per):
            ref[pl.ds(r * SUBLANES * per + c, SUBLANES, stride=per), :] = val[r * SUBLANES:(r + 1) * SUBLANES,
                                                                              c * LANES:(c + 1) * LANES]


def _rope(z, cos, s1, s2):
    outs = []
    for c in range(z.shape[1] // LANES):
        x = z[:, c * LANES:(c + 1) * LANES]
        outs.append(x * cos + pltpu.roll(x, LANES - 16, axis=1) * s1 + pltpu.roll(x, 16, axis=1) * s2)
    return jnp.concatenate(outs, axis=1)


W_A_COLS = G_A + G_QKV + G_RX


def _proj_a_kernel(*refs, with_moe, ct):
    if with_moe:
        (x_ref, acc_ref, modp_ref), refs = refs[:3], refs[3:]
    else:
        (x_ref, ctx_ref), refs = refs[:2], refs[2:]
    mod_ref, n_ref, cos_ref, s1_ref, s2_ref, w_ref = refs[:6]
    prm, refs = refs[6:12], refs[12:]
    if with_moe:
        xo_ref, refs = refs[0], refs[1:]
    za_ref, zq_ref, zkv_ref, zrx_ref, h_ref, hf_ref, ubuf, obuf, hcar, halo_s = refs
    s = pl.program_id(1)
    chunk = _rnn_chunk(s, ct, False)

    @pl.when(s == 0)
    def _():
        hcar[...] = jnp.zeros_like(hcar)
        halo_s[...] = jnp.zeros_like(halo_s)

    if with_moe:
        x = x_ref[...] + modp_ref[:, 5 * D:6 * D] * _tiles_to_rows(acc_ref, TM)
        xo_ref[...] = x
    else:
        x = jnp.where(chunk == ct, ctx_ref[...], x_ref[...])
    h = _rms_mod(x, n_ref[...], mod_ref[:, D:2 * D], mod_ref[:, 0:D]).astype(bf16)
    h_ref[...] = h
    hal = jnp.where((chunk != 0) & (chunk != ct), halo_s[...], 0.0)
    quarter = G_RX // 4

    def scan_input(p):
        cols = slice(p * quarter, (p + 1) * quarter)
        u = jnp.dot(h, w_ref[:, G_A + G_QKV + p * quarter:G_A + G_QKV + (p + 1) * quarter],
                    preferred_element_type=f32)
        zrx_ref[:, cols] = u.astype(bf16)
        _rnn_fill(u, hal[:, cols], ubuf, False, c0=p * quarter // LANES)
        halo_s[:, cols] = u[TM - BF16_ROWS:, :]

    scan = lambda c: _rnn_tile(c, prm, ubuf, obuf, hcar, False)
    cw = CONV_W
    scan_input(0)
    a_x = jnp.dot(h, w_ref[:, 0:cw], preferred_element_type=f32)
    scan(0)
    scan_input(1)
    scan(1)
    a_b = jnp.dot(h, w_ref[:, cw:2 * cw], preferred_element_type=f32)
    scan(2)
    scan_input(2)
    scan(3)
    a_c = jnp.dot(h, w_ref[:, 2 * cw:G_A], preferred_element_type=f32)
    za_ref[...] = jnp.concatenate([a_c * a_x, a_b], axis=1).astype(bf16)
    scan(4)
    scan_input(3)
    scan(5)
    z = jnp.dot(h, w_ref[:, G_A:G_A + G_QKV], preferred_element_type=f32)
    qk = _rope(z[:, :Q_W + KV_W], cos_ref[...], s1_ref[...], s2_ref[...])
    zq_ref[...] = (qk[:, :Q_W] * (HEAD_DIM ** -0.5)).astype(bf16)
    lo = lax.broadcasted_iota(i32, (TM, LANES), 1) < HEAD_DIM
    dup = []
    for pair in (qk[:, Q_W:], z[:, Q_W + KV_W:]):
        swapped = pltpu.roll(pair, HEAD_DIM, axis=1)
        dup += [jnp.where(lo, pair, swapped), jnp.where(lo, swapped, pair)]
    zkv_ref[...] = jnp.concatenate(dup, axis=1).astype(bf16)
    scan(6)
    scan(7)
    hf_ref[...] = _rnn_states(obuf).astype(bf16)


def _proj_b_kernel(h_ref, w_ref, u_ref, halo_ref, *refs, ct):
    prm, (zry_ref, zg_ref, hb_ref, ubuf, obuf, hcar) = refs[:6], refs[6:]
    s = pl.program_id(1)
    chunk = _rnn_chunk(s, ct, True)

    @pl.when(s == 0)
    def _():
        hcar[...] = jnp.zeros_like(hcar)

    h = h_ref[...]
    hal = jnp.where((chunk != ct - 1) & (chunk != ct), halo_ref[...].astype(f32), 0.0)
    _rnn_fill(u_ref[...].astype(f32), hal, ubuf, True)
    slab = (G_RY + G_G) // RNN_LT
    for c in range(RNN_LT):
        z = jnp.dot(h, w_ref[:, c * slab:(c + 1) * slab], preferred_element_type=f32)
        if (c + 1) * slab <= G_RY:
            t_y = jnp.tanh(z * (0.7978845608028654 + (0.7978845608028654 * 0.044715) * (z * z)))
            zry_ref[:, c * slab:(c + 1) * slab] = (z * t_y + z).astype(bf16)
        else:
            zg_ref[:, c * slab - G_RY:(c + 1) * slab - G_RY] = z.astype(bf16)
        _rnn_tile(c, prm, ubuf, obuf, hcar, True)
    hb_ref[...] = _rnn_states(obuf).astype(bf16)


def _proj_a(xs, ctx, acc, modsel_prev, modsel, norm, tabs, w_a, rnn_prm):
    with_moe = acc is not None
    b = xs.shape[0]
    nt = xs.shape[1] + (0 if with_moe else ctx.shape[1])
    n_tiles = nt // TM
    ct = n_tiles - 1
    ch = lambda s: _rnn_chunk(s, ct, False)
    tile = lambda w: pl.BlockSpec((None, TM, w), lambda i, s: (i, ch(s), 0))
    mod_spec = pl.BlockSpec((None, None, 1, N_MOD * D), lambda i, s: (i, ch(s) // ct, 0, 0))
    tab_spec = pl.BlockSpec((TM, LANES), lambda i, s: (ch(s), 0))
    if with_moe:
        in_specs = [tile(D), pl.BlockSpec((None, TM * SUBLANES, LANES), lambda i, s: (i, ch(s), 0)), mod_spec]
        args = [xs, acc, modsel_prev]
    else:
        in_specs = [pl.BlockSpec((None, TM, D), lambda i, s: (i, jnp.minimum(ch(s), ct - 1), 0)),
                    pl.BlockSpec((None, TM, D), lambda i, s: (i, 0, 0))]
        args = [xs, ctx]
    in_specs += [mod_spec, pl.BlockSpec((1, D), lambda i, s: (0, 0)), tab_spec, tab_spec, tab_spec,
                 pl.BlockSpec((D, W_A_COLS), lambda i, s: (0, 0), pipeline_mode=pl.Buffered(1))]
    in_specs += _rnn_param_specs()
    args += [modsel, norm, *tabs, w_a, *rnn_prm]
    widths = (ZA_W, Q_W, KV2_W, G_RX, D, RNN_W)
    out_shape = [jax.ShapeDtypeStruct((b, nt, w), bf16) for w in widths]
    out_specs = [tile(w) for w in widths]
    if with_moe:
        out_shape = [jax.ShapeDtypeStruct((b, nt, D), f32)] + out_shape
        out_specs = [tile(D)] + out_specs
    return pl.pallas_call(
        functools.partial(_proj_a_kernel, with_moe=with_moe, ct=ct),
        out_shape=out_shape, grid=(b, n_tiles), in_specs=in_specs, out_specs=out_specs,
        scratch_shapes=_RNN_SCRATCH + [pltpu.VMEM((BF16_ROWS, RNN_W), f32)],
        compiler_params=_cparams("arbitrary", "arbitrary"), name="proj_a_rnn_fwd",
    )(*args)


def _proj_b(h, w_b, zrx, rnn_prm):
    b, nt, _ = h.shape
    n_tiles = nt // TM
    ct = n_tiles - 1
    per = TM // BF16_ROWS
    ch = lambda s: _rnn_chunk(s, ct, True)
    tile = lambda w: pl.BlockSpec((None, TM, w), lambda i, s: (i, ch(s), 0))
    halo = pl.BlockSpec((None, BF16_ROWS, RNN_W),
                        lambda i, s: (i, jnp.minimum((ch(s) + 1) * per, nt // BF16_ROWS - 1), 0))
    widths = (G_RY, G_G, RNN_W)
    return pl.pallas_call(
        functools.partial(_proj_b_kernel, ct=ct),
        out_shape=[jax.ShapeDtypeStruct((b, nt, w), bf16) for w in widths],
        grid=(b, n_tiles),
        in_specs=[tile(D), pl.BlockSpec((D, G_RY + G_G), lambda i, s: (0, 0), pipeline_mode=pl.Buffered(1)),
                  tile(RNN_W), halo] + _rnn_param_specs(),
        out_specs=[tile(w) for w in widths],
        scratch_shapes=_RNN_SCRATCH,
        compiler_params=_cparams("arbitrary", "arbitrary"), name="proj_b_rnn_bwd",
    )(h, w_b, zrx, zrx, *rnn_prm)


ATT_QBLOCKS = 2


def _attn_kernel(sink_ref, q_ref, kvp_ref, kvc_ref, kvn_ref, kvx_ref, o_ref, *, n_lat_blocks, n_ctx):
    group = N_HEADS // N_KV
    nlb = n_lat_blocks
    lo = lax.broadcasted_iota(i32, (QB, LANES), 1) < HEAD_DIM
    diff = lax.broadcasted_iota(i32, (QB, QB), 1) - lax.broadcasted_iota(i32, (QB, QB), 0)
    zero = jnp.zeros((QB, LANES), bf16)
    work = []
    for sub in range(ATT_QBLOCKS):
        j = pl.program_id(1) * ATT_QBLOCKS + sub
        lat = j < nlb
        ok = (diff >= jnp.where(lat & (j >= 1), 0, QB),
              diff >= jnp.where(lat, -QB, QB),
              -diff >= jnp.where(lat & (j <= nlb - 2), 0, QB))
        rows = slice(sub * QB, (sub + 1) * QB)
        for g in range(N_KV):
            ksl = slice(g * LANES, (g + 1) * LANES)
            vsl = slice(K2_W + g * LANES, K2_W + (g + 1) * LANES)
            window = (kvp_ref, kvc_ref) if sub == 0 else (kvc_ref, kvn_ref)
            k = jnp.concatenate([r[:, ksl] for r in window] + [kvx_ref[:, ksl]], axis=0)
            v = jnp.concatenate([r[:, vsl] for r in window] + [kvx_ref[:, vsl]], axis=0)
            parts = []
            for pr in range(group // 2):
                qp = q_ref[rows, (g * (group // 2) + pr) * LANES:(g * (group // 2) + pr + 1) * LANES]
                parts += [jnp.where(lo, qp, zero), jnp.where(lo, zero, qp)]
            q4 = jnp.concatenate(parts, axis=0)
            s = lax.dot_general(q4, k, (((1,), (1,)), ((), ())), preferred_element_type=f32)
            work.append((rows, g, ok, s, v))
    for rows, g, ok, s, v in work:
        ps, inv = [], []
        for hh in range(group):
            sh = s[hh * QB:(hh + 1) * QB]
            sm = jnp.concatenate([jnp.where(ok[kb], sh[:, kb * QB:(kb + 1) * QB], NEG_INF) for kb in range(3)]
                                 + [sh[:, 3 * QB:]], axis=1)
            sink = sink_ref[0, g * group + hh]
            m = jnp.maximum(jnp.max(sm, axis=1, keepdims=True), sink)
            p = jnp.exp(sm - m)
            inv.append(1.0 / (jnp.sum(p, axis=1, keepdims=True) + jnp.exp(sink - m)))
            ps.append(p.astype(bf16))
        o = jnp.dot(jnp.concatenate(ps, axis=0), v, preferred_element_type=f32) * jnp.concatenate(inv, axis=0)
        for pr in range(group // 2):
            pair = jnp.where(lo, o[(2 * pr) * QB:(2 * pr + 1) * QB], o[(2 * pr + 1) * QB:(2 * pr + 2) * QB])
            c0 = (g * (group // 2) + pr) * LANES
            o_ref[rows, c0:c0 + LANES] = pair.astype(bf16)


def _attention(sink, zq, zkv, n_lat, n_qblocks):
    b, nt, _ = zq.shape
    n_ctx = nt - n_lat
    nlb = n_lat // QB
    cidx = n_lat // n_ctx
    nq = ATT_QBLOCKS
    assert nq == 2 and n_qblocks % nq == 0 and nlb % nq == 0
    edge = lambda d: pl.BlockSpec((None, QB, KV2_W), lambda i, j: (i, jnp.clip(nq * j + d, 0, nlb - 1), 0))
    ctx = pl.BlockSpec((None, n_ctx, KV2_W), lambda i, j: (i, cidx, 0))
    return pl.pallas_call(
        functools.partial(_attn_kernel, n_lat_blocks=nlb, n_ctx=n_ctx),
        out_shape=jax.ShapeDtypeStruct((b, n_qblocks * QB, Q_W), bf16),
        grid=(b, n_qblocks // nq),
        in_specs=[pl.BlockSpec(memory_space=pltpu.SMEM),
                  pl.BlockSpec((None, nq * QB, Q_W), lambda i, j: (i, j, 0)),
                  edge(-1), pl.BlockSpec((None, nq * QB, KV2_W), lambda i, j: (i, j, 0)), edge(nq), ctx],
        out_specs=pl.BlockSpec((None, nq * QB, Q_W), lambda i, j: (i, j, 0)),
        compiler_params=_cparams("arbitrary", "arbitrary"), name="attention",
    )(sink, zq, zkv, zkv, zkv, zkv)


def _rnn_chunk(s, ct, rev):
    return jnp.where(s == 0, ct, (ct - s) if rev else (s - 1))


def _block_scan(a, b, h_in, rowi, rev):
    n = SUBLANES
    order = range(n - 1, -1, -1) if rev else range(n)
    hs = [None] * n
    ps = [None] * n
    prev = None
    for j in order:
        if prev is None:
            hs[j], ps[j] = b[j], a[j]
        else:
            hs[j], ps[j] = a[j] * hs[prev] + b[j], a[j] * ps[prev]
        prev = j
    he, pe = hs[prev], ps[prev]
    for sh in (1, 2, 4):
        rs = n - sh if rev else sh
        msk = (rowi < n - sh) if rev else (rowi >= sh)
        he, pe = (jnp.where(msk, he + pe * pltpu.roll(he, rs, axis=0), he),
                  jnp.where(msk, pe * pltpu.roll(pe, rs, axis=0), pe))
    e = he + pe * h_in
    carry = jnp.where((rowi == n - 1) if rev else (rowi == 0), h_in, pltpu.roll(e, n - 1 if rev else 1, axis=0))
    out = [hs[j] + ps[j] * carry for j in range(n)]
    last = e[0:1, :] if rev else e[n - 1:n, :]
    return out, jnp.broadcast_to(last, (n, LANES))


RNN_LT = RNN_W // LANES


def _rnn_fill(u, hal, ubuf, rev, c0=0):
    for k in range(u.shape[1] // LANES):
        sl = slice(k * LANES, (k + 1) * LANES)
        if rev:
            ubuf[c0 + k, 0:TM, :] = u[:, sl]
            ubuf[c0 + k, TM:TM + BF16_ROWS, :] = hal[:, sl]
        else:
            ubuf[c0 + k, 0:BF16_ROWS, :] = hal[:, sl]
            ubuf[c0 + k, BF16_ROWS:BF16_ROWS + TM, :] = u[:, sl]


def _rnn_tile(c, prm, ubuf, obuf, hcar, rev):
    cw_ref, cb_ref, wg_ref, ba_ref, bx_ref, lam_ref = prm
    base = 0 if rev else BF16_ROWS - (RNN_CONV_K - 1)
    rowi = lax.broadcasted_iota(i32, (SUBLANES, LANES), 0)
    blk_rows = SUBLANES * SUBLANES
    n_blk = TM // blk_rows
    sl = slice(c * LANES, (c + 1) * LANES)
    taps = [cw_ref[k:k + 1, sl] for k in range(RNN_CONV_K)]
    xt = []
    for blk in range(n_blk):
        for j in range(SUBLANES):
            r0 = base + blk * blk_rows + j
            acc = cb_ref[:, sl] + ubuf[c, pl.ds(r0, SUBLANES, stride=SUBLANES), :] * taps[0]
            for k in range(1, RNN_CONV_K):
                acc = acc + ubuf[c, pl.ds(r0 + k, SUBLANES, stride=SUBLANES), :] * taps[k]
            xt.append(acc)
    xp = jnp.concatenate(xt, axis=0)
    g = jnp.dot(xp.astype(bf16), wg_ref[c], preferred_element_type=f32)
    t_r = jnp.tanh(g[:, :RNN_BLK] + 0.5 * ba_ref[:, sl])
    t_i = jnp.tanh(g[:, RNN_BLK:] + 0.5 * bx_ref[:, sl])
    c4 = (-0.5 * RG_C) * jnp.logaddexp(-lam_ref[:, sl], 0.0)
    log_a = c4 * t_r + c4
    a = jnp.exp(log_a)
    y = -jnp.tanh(log_a) * (a * a + 1.0)
    xh = 0.5 * xp
    bb = jnp.where(y > 0.0, y * lax.rsqrt(y), 0.0) * (xh * t_i + xh)
    h_in = hcar[:, sl]
    for blk in (range(n_blk - 1, -1, -1) if rev else range(n_blk)):
        rows = lambda j: slice((blk * SUBLANES + j) * SUBLANES, (blk * SUBLANES + j + 1) * SUBLANES)
        hs, h_in = _block_scan([a[rows(j)] for j in range(SUBLANES)], [bb[rows(j)] for j in range(SUBLANES)],
                               h_in, rowi, rev)
        for j in range(SUBLANES):
            obuf[c, pl.ds(blk * blk_rows + j, SUBLANES, stride=SUBLANES), :] = hs[j]
    hcar[:, sl] = h_in


def _rnn_states(obuf):
    return jnp.concatenate([obuf[c] for c in range(RNN_LT)], axis=1)


_RNN_SCRATCH = [pltpu.VMEM((RNN_LT, TM + BF16_ROWS, LANES), f32),
                pltpu.VMEM((RNN_LT, TM, LANES), f32),
                pltpu.VMEM((SUBLANES, RNN_W), f32)]


def _rnn_param_specs():
    full = lambda shape: pl.BlockSpec(shape, lambda i, s: (0,) * len(shape))
    return [full((RNN_CONV_K, RNN_W)), full((1, RNN_W)), full((RNN_BLOCKS, RNN_BLK, 2 * RNN_BLK)),
            full((1, RNN_W)), full((1, RNN_W)), full((1, RNN_W))]


def _merge_kernel(*refs, ct, split):
    (za_ref, zap_ref, zan_ref, att_ref, hf_ref, hb_ref, ry_ref, zg_ref, mod_ref, ca_ref, wc_ref,
     wa_ref, wr_ref, wo_ref, n2_ref, wrt_ref, shift_ref, xo_ref, h2_ref, aff_ref) = refs[2 if split else 1:]
    t = pl.program_id(1)
    if split:
        x_in = jnp.where(t == ct, refs[1][...], refs[0][...])
    else:
        x_in = refs[0][...]
    cw = CONV_W
    cu_b = za_ref[:, 0:cw]
    sh = jnp.dot(shift_ref[...], cu_b, preferred_element_type=f32)
    prev_ok = (t != 0) & (t != ct)
    next_ok = (t != ct - 1) & (t != ct)
    cu_p = jnp.where(prev_ok, zap_ref[BF16_ROWS - 1:BF16_ROWS, 0:cw].astype(f32), 0.0)
    cu_n = jnp.where(next_ok, zan_ref[0:1, 0:cw].astype(f32), 0.0)
    rowi = lax.broadcasted_iota(i32, (TM, cw), 0)
    cu_prev = jnp.where(rowi == 0, cu_p, sh[:TM])
    cu_next = jnp.where(rowi == TM - 1, cu_n, sh[TM:])
    y = cu_prev * ca_ref[0:1, :] + cu_b.astype(f32) * ca_ref[1:2, :] + cu_next * ca_ref[2:3, :]
    cnv = jnp.dot((za_ref[:, cw:2 * cw].astype(f32) * y).astype(bf16), wc_ref[...], preferred_element_type=f32)
    att = jnp.dot(att_ref[...], wa_ref[...], preferred_element_type=f32)
    rec = hf_ref[...].astype(f32) + hb_ref[...].astype(f32)
    rnn = jnp.dot((ry_ref[...].astype(f32) * rec).astype(bf16), wr_ref[...], preferred_element_type=f32)
    mix2 = None
    for k, br in enumerate((cnv, att, rnn)):
        term = jnp.tanh(zg_ref[:, k * D:(k + 1) * D].astype(f32)) * br + br
        mix2 = term if mix2 is None else mix2 + term
    x = x_in + mod_ref[:, 2 * D:3 * D] * jnp.dot(mix2.astype(bf16), wo_ref[...], preferred_element_type=f32)
    xo_ref[...] = x
    h2 = _rms_mod(x, n2_ref[...], mod_ref[:, 4 * D:5 * D], mod_ref[:, 3 * D:4 * D])
    _rows_to_tiles(h2_ref, h2, TM)
    logits = lax.dot_general(wrt_ref[...], h2.astype(bf16), (((1,), (1,)), ((), ())), preferred_element_type=f32)
    e = jnp.exp(logits - jnp.max(logits, axis=0, keepdims=True))
    aff_ref[...] = e / jnp.sum(e, axis=0, keepdims=True)


def _row_shift_operator():
    r = jnp.arange(2 * TM)[:, None]
    c = jnp.arange(TM)[None, :]
    return (c == jnp.where(r < TM, r - 1, r - TM + 1)).astype(bf16)


def _merge(xs, ctx, za, att, hf, hb, zry, zg, modsel, conv_a, wc, wa, wr, wo, norm2, wrt, n_tiles_eff):
    b, nt, _ = za.shape
    ct = nt // TM - 1
    per = TM // BF16_ROWS
    split = ctx is not None
    tile = lambda w: pl.BlockSpec((None, TM, w), lambda i, t: (i, t, 0))
    full = lambda shape: pl.BlockSpec(shape, lambda i, t: (0,) * len(shape), pipeline_mode=pl.Buffered(1))
    if split:
        x_specs = [pl.BlockSpec((None, TM, D), lambda i, t: (i, jnp.minimum(t, ct - 1), 0)),
                   pl.BlockSpec((None, TM, D), lambda i, t: (i, 0, 0))]
        x_args = [xs, ctx]
    else:
        x_specs = [tile(D)]
        x_args = [xs]
    return pl.pallas_call(
        functools.partial(_merge_kernel, ct=ct, split=split),
        out_shape=[jax.ShapeDtypeStruct((b, n_tiles_eff * TM, D), f32),
                   jax.ShapeDtypeStruct((b, n_tiles_eff * TM * SUBLANES, LANES), f32),
                   jax.ShapeDtypeStruct((b, N_EXPERTS, n_tiles_eff * TM), f32)],
        grid=(b, n_tiles_eff),
        in_specs=x_specs + [tile(ZA_W),
                  pl.BlockSpec((None, BF16_ROWS, ZA_W), lambda i, t: (i, jnp.maximum(t * per - 1, 0), 0)),
                  pl.BlockSpec((None, BF16_ROWS, ZA_W),
                               lambda i, t: (i, jnp.minimum((t + 1) * per, nt // BF16_ROWS - 1), 0)),
                  tile(Q_W), tile(RNN_W), tile(RNN_W), tile(RNN_W), tile(G_G),
                  pl.BlockSpec((None, None, 1, N_MOD * D), lambda i, t: (i, t // ct, 0, 0)),
                  full((3, CONV_W)), full((CONV_W, D)), full((Q_W, D)), full((RNN_W, D)), full((D, D)),
                  full((1, D)), full((N_EXPERTS, D)), full((2 * TM, TM))],
        out_specs=[tile(D), pl.BlockSpec((None, TM * SUBLANES, LANES), lambda i, t: (i, t, 0)),
                   pl.BlockSpec((None, N_EXPERTS, TM), lambda i, t: (i, 0, t))],
        compiler_params=_cparams("arbitrary", "arbitrary"), name="merge",
    )(*x_args, za, za, za, att, hf, hb, zry, zg, modsel, conv_a, wc, wa, wr, wo, norm2, wrt, _row_shift_operator())


ROUTE_QW = 64
ROUTE_TOK_SHIFT = 6
CUM_BLK = 256


def _cumsum_lanes(x, n):
    blk = min(CUM_BLK, n)
    tri = (lax.broadcasted_iota(i32, (blk, blk), 0) <= lax.broadcasted_iota(i32, (blk, blk), 1)).astype(bf16)
    carry = jnp.zeros((x.shape[0], 1), f32)
    outs = []
    for j in range(n // blk):
        c = jnp.dot(x[:, j * blk:(j + 1) * blk].astype(bf16), tri, preferred_element_type=f32) + carry
        outs.append(c)
        carry = c[:, blk - 1:blk]
    return jnp.concatenate(outs, axis=1)


def _route_kernel(aff_ref, idx_ref, val_ref, pos_s, *, n, cap):
    aff = aff_ref[...]
    n_rows = aff.shape[0]
    thr = jnp.zeros((n_rows, 1), i32)
    for bit in range(30, -1, -1):
        cand = thr | (1 << bit)
        cnt = jnp.sum(jnp.where(aff >= lax.bitcast_convert_type(cand, f32), 1.0, 0.0), axis=1, keepdims=True)
        thr = jnp.where(cnt >= float(cap), cand, thr)
    thr_f = lax.bitcast_convert_type(thr, f32)
    gt = aff > thr_f
    eq = aff == thr_f
    need = float(cap) - jnp.sum(jnp.where(gt, 1.0, 0.0), axis=1, keepdims=True)
    sel = gt | (eq & (_cumsum_lanes(eq.astype(f32), n) <= need))
    pos_s[...] = jnp.where(sel, _cumsum_lanes(sel.astype(f32), n).astype(i32) - 1, -1)

    qw = min(ROUTE_QW, cap)
    shift = qw.bit_length() - 1
    tok = lax.broadcasted_iota(i32, (1, n), 1)
    tok_hi = (tok >> ROUTE_TOK_SHIFT).astype(f32)
    tok_lo = (tok & ((1 << ROUTE_TOK_SHIFT) - 1)).astype(f32)
    hi_iota = lax.broadcasted_iota(i32, (SUBLANES, n), 0)
    lo_iota = lax.broadcasted_iota(i32, (qw, n), 0)

    def body(e, carry):
        pos = pos_s[pl.ds(e, 1), :]
        a0 = aff_ref[pl.ds(e, 1), :]
        a_h = a0.astype(bf16).astype(f32)
        a_m = (a0 - a_h).astype(bf16).astype(f32)
        a_l = ((a0 - a_h) - a_m).astype(bf16).astype(f32)
        in_hi = (pos >> shift) == hi_iota
        lhs = jnp.concatenate([jnp.where(in_hi, r, 0.0) for r in (tok_hi, tok_lo, a_h, a_m, a_l)], axis=0)
        onehot = jnp.where((pos & (qw - 1)) == lo_iota, 1.0, 0.0).astype(bf16)
        res = lax.dot_general(lhs.astype(bf16), onehot, (((1,), (1,)), ((), ())), preferred_element_type=f32)
        r_hi, r_lo, v_h, v_m, v_l = (res[k * SUBLANES:(k + 1) * SUBLANES] for k in range(5))
        idx_ref[e] = (r_hi * float(1 << ROUTE_TOK_SHIFT) + r_lo).astype(i32)[:cap // qw]
        val_ref[e] = ((v_h + v_m) + v_l)[:cap // qw]
        return carry

    lax.fori_loop(0, n_rows, body, 0)


def _route(aff_t, n, cap, lane_block):
    b = aff_t.shape[0]
    qw = min(ROUTE_QW, cap)
    assert cap % qw == 0 and cap // qw <= SUBLANES and qw & (qw - 1) == 0
    rows = b * N_EXPERTS
    idx, val = pl.pallas_call(
        functools.partial(_route_kernel, n=n, cap=cap),
        out_shape=[jax.ShapeDtypeStruct((rows, cap // qw, qw), i32),
                   jax.ShapeDtypeStruct((rows, cap // qw, qw), f32)],
        grid=(1,),
        in_specs=[pl.BlockSpec((rows, n), lambda i: (0, lane_block))],
        out_specs=[pl.BlockSpec((rows, cap // qw, qw), lambda i: (0, 0, 0)),
                   pl.BlockSpec((rows, cap // qw, qw), lambda i: (0, 0, 0))],
        scratch_shapes=[pltpu.VMEM((rows, n), i32)],
        compiler_params=_cparams("arbitrary"), name="route",
    )(aff_t.reshape(rows, aff_t.shape[2]))
    return idx.reshape(b, N_EXPERTS, cap), val.reshape(b, N_EXPERTS, cap)


def _gather_kernel(idx_ref, h_ref, tok_ref, tbuf, *, cap):
    base = (pl.program_id(0) * N_EXPERTS + pl.program_id(1)) * cap
    sp = cap + SUBLANES
    for p in range(cap):
        i = idx_ref[base + p]
        tbuf[pl.ds(p, SUBLANES, stride=sp), :] = h_ref[pl.ds(pl.multiple_of(i * SUBLANES, SUBLANES), SUBLANES), :]
    tok_ref[...] = jnp.concatenate([tbuf[pl.ds(c * sp, cap), :] for c in range(D // LANES)], axis=1).astype(bf16)


def _gather(idx, h2t, cap):
    b, rows, _ = h2t.shape
    return pl.pallas_call(
        functools.partial(_gather_kernel, cap=cap),
        out_shape=jax.ShapeDtypeStruct((b, N_EXPERTS, cap, D), bf16),
        grid_spec=pltpu.PrefetchScalarGridSpec(
            num_scalar_prefetch=1, grid=(b, N_EXPERTS),
            in_specs=[pl.BlockSpec((None, rows, LANES), lambda i, e, idx: (i, 0, 0))],
            out_specs=pl.BlockSpec((None, None, cap, D), lambda i, e, idx: (i, e, 0, 0)),
            scratch_shapes=[pltpu.VMEM((SUBLANES * (cap + SUBLANES), LANES), f32)]),
        compiler_params=_cparams("arbitrary", "arbitrary"), name="moe_gather",
    )(idx, h2t)


def _cast3_kernel(a_ref, b_ref, c_ref, ao_ref, bo_ref, co_ref):
    ao_ref[...] = a_ref[...].astype(bf16)
    bo_ref[...] = b_ref[...].astype(bf16)
    co_ref[...] = c_ref[...].astype(bf16)


def _expert0_bf16(wg, wu, wd, layer):
    spec = pl.BlockSpec((None, None, D, EXPERT_FF), lambda i: (layer, 0, 0, 0))
    ospec = pl.BlockSpec((D, EXPERT_FF), lambda i: (0, 0))
    return pl.pallas_call(
        _cast3_kernel, out_shape=[jax.ShapeDtypeStruct((D, EXPERT_FF), bf16)] * 3, grid=(1,),
        in_specs=[spec] * 3, out_specs=[ospec] * 3, compiler_params=_cparams("arbitrary"), name="expert0_cast",
    )(wg, wu, wd)


def _ffn_kernel(tok_ref, g0_ref, u0_ref, d0_ref, gq_ref, uq_ref, dq_ref, o_ref, w_even, w_odd):
    e = pl.program_id(0)
    i = pl.program_id(1)

    @pl.when((e == 0) & (i == 0))
    def _():
        w_even[0] = g0_ref[...]
        w_even[1] = u0_ref[...]
        w_even[2] = d0_ref[...]

    rows = gq_ref.shape[0]
    r0 = pl.multiple_of(i * rows, rows)

    def step(use, fill):
        t = tok_ref[...]
        g = jnp.dot(t, use[0], preferred_element_type=f32)
        for k, q_ref in enumerate((gq_ref, uq_ref, dq_ref)):
            fill[k, pl.ds(r0, rows), :] = q_ref[...].astype(bf16)
        u = jnp.dot(t, use[1], preferred_element_type=f32)
        a = (g * _sigmoid(g) * u).astype(bf16)
        o_ref[...] = jnp.dot(a, use[2], preferred_element_type=f32)

    @pl.when(e % 2 == 0)
    def _():
        step(w_even, w_odd)

    @pl.when(e % 2 == 1)
    def _():
        step(w_odd, w_even)


def _ffn(tok, wg, wu, wd, layer):
    b, _, cap, _ = tok.shape
    assert D == EXPERT_FF and D % b == 0 and (D // b) % BF16_ROWS == 0
    rows = D // b
    w0 = _expert0_bf16(wg, wu, wd, layer)
    w0spec = pl.BlockSpec((D, EXPERT_FF), lambda e, i: (0, 0), pipeline_mode=pl.Buffered(1))
    qspec = pl.BlockSpec((None, None, rows, D), lambda e, i: (layer, jnp.minimum(e + 1, N_EXPERTS - 1), i, 0))
    return pl.pallas_call(
        _ffn_kernel,
        out_shape=jax.ShapeDtypeStruct((b, N_EXPERTS, cap, D), f32),
        grid=(N_EXPERTS, b),
        in_specs=[pl.BlockSpec((None, None, cap, D), lambda e, i: (i, e, 0, 0)),
                  w0spec, w0spec, w0spec, qspec, qspec, qspec],
        out_specs=pl.BlockSpec((None, None, cap, D), lambda e, i: (i, e, 0, 0)),
        scratch_shapes=[pltpu.VMEM((3, D, EXPERT_FF), bf16), pltpu.VMEM((3, D, EXPERT_FF), bf16)],
        compiler_params=_cparams("arbitrary", "arbitrary"), name="moe_ffn",
    )(tok, *w0, wg, wu, wd)


SCATTER_UNROLL = 8


def _scatter_kernel(idx_ref, val_ref, o_ref, acc_ref, tbuf, *, cap):
    e = pl.program_id(1)
    base = (pl.program_id(0) * N_EXPERTS + e) * cap
    sp = cap + SUBLANES

    @pl.when(e == 0)
    def _():
        acc_ref[...] = jnp.zeros_like(acc_ref)

    for c in range(D // LANES):
        tbuf[pl.ds(c * sp, cap), :] = o_ref[:, c * LANES:(c + 1) * LANES]
    u = min(SCATTER_UNROLL, cap)
    for p0 in range(0, cap, u):
        pend = []
        for p in range(p0, p0 + u):
            r0 = pl.multiple_of(idx_ref[base + p] * SUBLANES, SUBLANES)
            pend.append((r0, acc_ref[pl.ds(r0, SUBLANES), :] + tbuf[pl.ds(p, SUBLANES, stride=sp), :] * val_ref[base + p]))
        for r0, v in pend:
            acc_ref[pl.ds(r0, SUBLANES), :] = v


def _scatter(idx, val, o, rows):
    b, _, cap, _ = o.shape
    return pl.pallas_call(
        functools.partial(_scatter_kernel, cap=cap),
        out_shape=jax.ShapeDtypeStruct((b, rows, LANES), f32),
        grid_spec=pltpu.PrefetchScalarGridSpec(
            num_scalar_prefetch=2, grid=(b, N_EXPERTS),
            in_specs=[pl.BlockSpec((None, None, cap, D), lambda i, e, idx, val: (i, e, 0, 0))],
            out_specs=pl.BlockSpec((None, rows, LANES), lambda i, e, idx, val: (i, 0, 0)),
            scratch_shapes=[pltpu.VMEM((SUBLANES * (cap + SUBLANES), LANES), f32)]),
        compiler_params=_cparams("arbitrary", "arbitrary"), name="moe_scatter",
    )(idx, val, o)


def _moe(aff_t, h2t, wg, wu, wd, layer, n_lat, n_ctx):
    idx, val = _route(aff_t, n_lat, CAP_FACTOR * n_lat // N_EXPERTS, 0)
    if n_ctx:
        idx_c, val_c = _route(aff_t, n_ctx, CAP_FACTOR * n_ctx // N_EXPERTS, n_lat // n_ctx)
        idx = jnp.concatenate([idx, idx_c + n_lat], axis=-1)
        val = jnp.concatenate([val, val_c], axis=-1)
    cap = idx.shape[-1]
    idx = idx.reshape(-1)
    val = val.reshape(-1)
    tok = _gather(idx, h2t, cap)
    o = _ffn(tok, wg, wu, wd, layer)
    return _scatter(idx, val, o, h2t.shape[1])


def _final_kernel(x_ref, acc_ref, mod_ref, n_ref, o_ref):
    x = x_ref[...] + mod_ref[:, 5 * D:6 * D] * _tiles_to_rows(acc_ref, TM)
    o_ref[...] = (x * lax.rsqrt(jnp.mean(x * x, axis=-1, keepdims=True) + EPS)) * n_ref[...]


def _final(xs, acc, modsel, norm, n_lat):
    b = xs.shape[0]
    return pl.pallas_call(
        _final_kernel,
        out_shape=jax.ShapeDtypeStruct((b, n_lat, D), f32),
        grid=(b, n_lat // TM),
        in_specs=[pl.BlockSpec((None, TM, D), lambda i, t: (i, t, 0)),
                  pl.BlockSpec((None, TM * SUBLANES, LANES), lambda i, t: (i, t, 0)),
                  pl.BlockSpec((None, None, 1, N_MOD * D), lambda i, t: (i, 0, 0, 0)),
                  pl.BlockSpec((1, D), lambda i, t: (0, 0))],
        out_specs=pl.BlockSpec((None, TM, D), lambda i, t: (i, t, 0)),
        compiler_params=_cparams("arbitrary", "arbitrary"), name="final_norm",
    )(xs, acc, modsel, norm)


def _rope_tables(n_lat, n_ctx):
    pos = jnp.arange(n_lat, dtype=f32)
    rows = jnp.floor(pos / GRID_W)
    cols = pos - rows * GRID_W
    half = HEAD_DIM // 4
    inv = 1.0 / (ROPE_BASE ** (jnp.arange(0, 2 * half, 2, dtype=f32) / (2 * half)))
    lane = jnp.arange(LANES)
    d = lane % HEAD_DIM
    ang = jnp.where((d < 2 * half)[None, :], rows[:, None], cols[:, None]) * inv[d % half][None, :]
    first = ((d % (2 * half)) < half)[None, :]
    cos = jnp.cos(ang)
    sin = jnp.sin(ang)
    s1 = jnp.where(first, -sin, 0.0)
    s2 = jnp.where(first, 0.0, sin)
    pad = lambda t, v: jnp.concatenate([t, jnp.full((n_ctx, LANES), v, f32)], axis=0)
    return pad(cos, 1.0), pad(s1, 0.0), pad(s2, 0.0)


def _in_weights_bf16(w):
    scale = jnp.concatenate([jnp.ones((IN_TOTAL - G_G,), f32), jnp.full((G_G,), 0.5, f32)])
    return (w * scale[None, :]).astype(bf16)


def kernel(x, c, ctx, c_ctx, norm1, norm2, w_mod, b_mod, w_in, conv_a, w_conv_out, attn_sink, w_attn_out, rnn_conv_w,
           rnn_conv_b, rnn_w_a, rnn_b_a, rnn_w_x, rnn_b_x, rnn_lam, w_rnn_out, w_o, w_router, w_e_gate, w_e_up,
           w_e_down, final_norm):
    b, n_lat, d = x.shape
    n_ctx = ctx.shape[1]
    depth = w_mod.shape[0]
    assert d == D and n_ctx == TM and n_lat % TM == 0 and b < SUBLANES
    nt = n_lat + n_ctx
    ct = nt // TM - 1

    cstack = jnp.concatenate([c, c_ctx[None, :], jnp.zeros((SUBLANES - b - 1, D), f32)], axis=0)
    mod = _modulation(cstack, w_mod, b_mod)
    tabs = _rope_tables(n_lat, n_ctx)

    xs = x
    acc = None
    modsel_prev = None
    for l in range(depth):
        ctx_out = l < depth - 1
        ctx_sep = ctx if l == 0 else None
        modsel = jnp.stack([mod[l, :b], jnp.broadcast_to(mod[l, b], (b, N_MOD * D))], axis=1)[:, :, None, :]
        rnn_prm = []
        for dr in range(2):
            wg = (0.5 * jnp.concatenate([rnn_w_a[l, dr], rnn_w_x[l, dr]], axis=-1)).astype(bf16)
            rnn_prm.append((rnn_conv_w[l, dr], rnn_conv_b[l, dr][None, :], wg, rnn_b_a[l, dr][None, :],
                            rnn_b_x[l, dr][None, :], rnn_lam[l, dr][None, :]))
        w_all = _in_weights_bf16(w_in[l])
        outs = _proj_a(xs, ctx_sep, acc, modsel_prev, modsel, norm1[l][None, :], tabs, w_all[:, :W_A_COLS],
                       rnn_prm[0])
        if acc is not None:
            xs, outs = outs[0], outs[1:]
        za, zq, zkv, zrx, hn, hf = outs
        zry, zg, hb = _proj_b(hn, w_all[:, W_A_COLS:], zrx, rnn_prm[1])
        att = _attention(attn_sink[l][None, :], zq, zkv, n_lat, (nt if ctx_out else n_lat) // QB)
        xs, h2t, aff_t = _merge(xs, ctx_sep, za, att, hf, hb, zry, zg, modsel, conv_a[l],
                                w_conv_out[l].astype(bf16),
                                w_attn_out[l].astype(bf16), (0.5 * w_rnn_out[l]).astype(bf16),
                                (0.5 * w_o[l]).astype(bf16),
                                norm2[l][None, :], w_router[l].T.astype(bf16), (ct + 1) if ctx_out else ct)
        acc = _moe(aff_t, h2t, w_e_gate, w_e_up, w_e_down, l, n_lat, n_ctx if ctx_out else 0)
        modsel_prev = modsel
    return _final(xs, acc, modsel_prev, final_norm[None, :], n_lat)
```

```python
import functools

import jax
import jax.numpy as jnp
from jax import lax
from jax.experimental import pallas as pl
from jax.experimental.pallas import tpu as pltpu

f32 = jnp.float32
bf16 = jnp.bfloat16
i32 = jnp.int32

D = 1024
EPS = 1e-6
CONV_W = 512
N_HEADS = 8
N_KV = 2
HEAD_DIM = 64
Q_W = N_HEADS * HEAD_DIM
KV_W = N_KV * HEAD_DIM
WINDOW = 128
GRID_W = 64
ROPE_BASE = 10000.0
NEG_INF = -1e30
RNN_W = 1024
RNN_BLOCKS = 8
RNN_BLK = RNN_W // RNN_BLOCKS
RNN_CONV_K = 4
RG_C = 8.0
N_EXPERTS = 16
EXPERT_FF = 1024
CAP_FACTOR = 2
N_MOD = 6

LANES = 128
SUBLANES = 8
BF16_ROWS = 16
VMEM_LIMIT = 56 * 1024 * 1024

TM = 256
QB = 128
K2_W = 2 * KV_W

G_A = 3 * CONV_W
G_QKV = Q_W + 2 * KV_W
G_RX = RNN_W
G_RY = RNN_W
G_G = 3 * D
IN_COLS = (G_A, G_QKV, G_RX, G_RY, G_G)
IN_TOTAL = sum(IN_COLS)
KV2_W = 2 * K2_W
ZA_W = 2 * CONV_W
OUT_COLS = (ZA_W, Q_W, KV2_W, G_RX, G_RY, G_G)


def _cparams(*sem):
    return pltpu.CompilerParams(dimension_semantics=sem, vmem_limit_bytes=VMEM_LIMIT)


def _sigmoid(x):
    return 0.5 * jnp.tanh(0.5 * x) + 0.5


def _rms_mod(x, g, sc, sh):
    y = x * lax.rsqrt(jnp.mean(x * x, axis=-1, keepdims=True) + EPS)
    return (y * g) * (1.0 + sc) + sh


def _mod_kernel(c_ref, w_ref, b_ref, o_ref):
    c = c_ref[...]
    s = (c * _sigmoid(c)).astype(bf16)
    o_ref[...] = jnp.dot(s, w_ref[...].astype(bf16), preferred_element_type=f32) + b_ref[...]


def _modulation(cstack, w_mod, b_mod):
    depth = w_mod.shape[0]
    tn = 1536
    return pl.pallas_call(
        _mod_kernel,
        out_shape=jax.ShapeDtypeStruct((depth, SUBLANES, N_MOD * D), f32),
        grid=(depth, N_MOD * D // tn),
        in_specs=[
            pl.BlockSpec((SUBLANES, D), lambda l, j: (0, 0)),
            pl.BlockSpec((None, D, tn), lambda l, j: (l, 0, j)),
            pl.BlockSpec((None, 1, tn), lambda l, j: (l, 0, j)),
        ],
        out_specs=pl.BlockSpec((None, SUBLANES, tn), lambda l, j: (l, 0, j)),
        compiler_params=_cparams("arbitrary", "arbitrary"),
        name="modulation",
    )(cstack, w_mod, b_mod.reshape(depth, 1, N_MOD * D))


def _tiles_to_rows(ref, rows):
    return jnp.concatenate([ref[pl.ds(c, rows, stride=SUBLANES), :] for c in range(D // LANES)], axis=1)


def _rows_to_tiles(ref, val, rows):
    per = D // LANES
    for r in range(rows // SUBLANES):
        for c in range(per):
            ref[pl.ds(r * SUBLANES * per + c, SUBLANES, stride=per), :] = val[r * SUBLANES:(r + 1) * SUBLANES,
                                                                              c * LANES:(c + 1) * LANES]


def _rope(z, cos, s1, s2):
    outs = []
    for c in range(z.shape[1] // LANES):
        x = z[:, c * LANES:(c + 1) * LANES]
        outs.append(x * cos + pltpu.roll(x, LANES - 16, axis=1) * s1 + pltpu.roll(x, 16, axis=1) * s2)
    return jnp.concatenate(outs, axis=1)


W_A_COLS = G_A + G_QKV + G_RX


def _proj_a_kernel(*refs, with_moe, ct):
    if with_moe:
        (x_ref, acc_ref, modp_ref), refs = refs[:3], refs[3:]
    else:
        (x_ref, ctx_ref), refs = refs[:2], refs[2:]
    mod_ref, n_ref, cos_ref, s1_ref, s2_ref, w_ref = refs[:6]
    prm, refs = refs[6:12], refs[12:]
    if with_moe:
        xo_ref, refs = refs[0], refs[1:]
    za_ref, zq_ref, zkv_ref, zrx_ref, h_ref, hf_ref, ubuf, obuf, hcar, halo_s = refs
    s = pl.program_id(1)
    chunk = _rnn_chunk(s, ct, False)

    @pl.when(s == 0)
    def _():
        hcar[...] = jnp.zeros_like(hcar)
        halo_s[...] = jnp.zeros_like(halo_s)

    if with_moe:
        x = x_ref[...] + modp_ref[:, 5 * D:6 * D] * _tiles_to_rows(acc_ref, TM)
        xo_ref[...] = x
    else:
        x = jnp.where(chunk == ct, ctx_ref[...], x_ref[...])
    h = _rms_mod(x, n_ref[...], mod_ref[:, D:2 * D], mod_ref[:, 0:D]).astype(bf16)
    h_ref[...] = h
    hal = jnp.where((chunk != 0) & (chunk != ct), halo_s[...], 0.0)
    quarter = G_RX // 4

    def scan_input(p):
        cols = slice(p * quarter, (p + 1) * quarter)
        u = jnp.dot(h, w_ref[:, G_A + G_QKV + p * quarter:G_A + G_QKV + (p + 1) * quarter],
                    preferred_element_type=f32)
        zrx_ref[:, cols] = u.astype(bf16)
        _rnn_fill(u, hal[:, cols], ubuf, False, c0=p * quarter // LANES)
        halo_s[:, cols] = u[TM - BF16_ROWS:, :]

    scan = lambda c: _rnn_tile(c, prm, ubuf, obuf, hcar, False)
    cw = CONV_W
    scan_input(0)
    a_x = jnp.dot(h, w_ref[:, 0:cw], preferred_element_type=f32)
    scan(0)
    scan_input(1)
    scan(1)
    a_b = jnp.dot(h, w_ref[:, cw:2 * cw], preferred_element_type=f32)
    scan(2)
    scan_input(2)
    scan(3)
    a_c = jnp.dot(h, w_ref[:, 2 * cw:G_A], preferred_element_type=f32)
    za_ref[...] = jnp.concatenate([a_c * a_x, a_b], axis=1).astype(bf16)
    scan(4)
    scan_input(3)
    scan(5)
    z = jnp.dot(h, w_ref[:, G_A:G_A + G_QKV], preferred_element_type=f32)
    qk = _rope(z[:, :Q_W + KV_W], cos_ref[...], s1_ref[...], s2_ref[...])
    zq_ref[...] = (qk[:, :Q_W] * (HEAD_DIM ** -0.5)).astype(bf16)
    lo = lax.broadcasted_iota(i32, (TM, LANES), 1) < HEAD_DIM
    dup = []
    for pair in (qk[:, Q_W:], z[:, Q_W + KV_W:]):
        swapped = pltpu.roll(pair, HEAD_DIM, axis=1)
        dup += [jnp.where(lo, pair, swapped), jnp.where(lo, swapped, pair)]
    zkv_ref[...] = jnp.concatenate(dup, axis=1).astype(bf16)
    scan(6)
    scan(7)
    hf_ref[...] = _rnn_states(obuf).astype(bf16)


def _proj_b_kernel(h_ref, w_ref, u_ref, halo_ref, *refs, ct):
    prm, (zry_ref, zg_ref, hb_ref, ubuf, obuf, hcar) = refs[:6], refs[6:]
    s = pl.program_id(1)
    chunk = _rnn_chunk(s, ct, True)

    @pl.when(s == 0)
    def _():
        hcar[...] = jnp.zeros_like(hcar)

    h = h_ref[...]
    hal = jnp.where((chunk != ct - 1) & (chunk != ct), halo_ref[...].astype(f32), 0.0)
    _rnn_fill(u_ref[...].astype(f32), hal, ubuf, True)
    slab = (G_RY + G_G) // RNN_LT
    for c in range(RNN_LT):
        z = jnp.dot(h, w_ref[:, c * slab:(c + 1) * slab], preferred_element_type=f32)
        if (c + 1) * slab <= G_RY:
            t_y = jnp.tanh(z * (0.7978845608028654 + (0.7978845608028654 * 0.044715) * (z * z)))
            zry_ref[:, c * slab:(c + 1) * slab] = (z * t_y + z).astype(bf16)
        else:
            zg_ref[:, c * slab - G_RY:(c + 1) * slab - G_RY] = z.astype(bf16)
        _rnn_tile(c, prm, ubuf, obuf, hcar, True)
    hb_ref[...] = _rnn_states(obuf).astype(bf16)


def _proj_a(xs, ctx, acc, modsel_prev, modsel, norm, tabs, w_a, rnn_prm):
    with_moe = acc is not None
    b = xs.shape[0]
    nt = xs.shape[1] + (0 if with_moe else ctx.shape[1])
    n_tiles = nt // TM
    ct = n_tiles - 1
    ch = lambda s: _rnn_chunk(s, ct, False)
    tile = lambda w: pl.BlockSpec((None, TM, w), lambda i, s: (i, ch(s), 0))
    mod_spec = pl.BlockSpec((None, None, 1, N_MOD * D), lambda i, s: (i, ch(s) // ct, 0, 0))
    tab_spec = pl.BlockSpec((TM, LANES), lambda i, s: (ch(s), 0))
    if with_moe:
        in_specs = [tile(D), pl.BlockSpec((None, TM * SUBLANES, LANES), lambda i, s: (i, ch(s), 0)), mod_spec]
        args = [xs, acc, modsel_prev]
    else:
        in_specs = [pl.BlockSpec((None, TM, D), lambda i, s: (i, jnp.minimum(ch(s), ct - 1), 0)),
                    pl.BlockSpec((None, TM, D), lambda i, s: (i, 0, 0))]
        args = [xs, ctx]
    in_specs += [mod_spec, pl.BlockSpec((1, D), lambda i, s: (0, 0)), tab_spec, tab_spec, tab_spec,
                 pl.BlockSpec((D, W_A_COLS), lambda i, s: (0, 0), pipeline_mode=pl.Buffered(1))]
    in_specs += _rnn_param_specs()
    args += [modsel, norm, *tabs, w_a, *rnn_prm]
    widths = (ZA_W, Q_W, KV2_W, G_RX, D, RNN_W)
    out_shape = [jax.ShapeDtypeStruct((b, nt, w), bf16) for w in widths]
    out_specs = [tile(w) for w in widths]
    if with_moe:
        out_shape = [jax.ShapeDtypeStruct((b, nt, D), f32)] + out_shape
        out_specs = [tile(D)] + out_specs
    return pl.pallas_call(
        functools.partial(_proj_a_kernel, with_moe=with_moe, ct=ct),
        out_shape=out_shape, grid=(b, n_tiles), in_specs=in_specs, out_specs=out_specs,
        scratch_shapes=_RNN_SCRATCH + [pltpu.VMEM((BF16_ROWS, RNN_W), f32)],
        compiler_params=_cparams("arbitrary", "arbitrary"), name="proj_a_rnn_fwd",
    )(*args)


def _proj_b(h, w_b, zrx, rnn_prm):
    b, nt, _ = h.shape
    n_tiles = nt // TM
    ct = n_tiles - 1
    per = TM // BF16_ROWS
    ch = lambda s: _rnn_chunk(s, ct, True)
    tile = lambda w: pl.BlockSpec((None, TM, w), lambda i, s: (i, ch(s), 0))
    halo = pl.BlockSpec((None, BF16_ROWS, RNN_W),
                        lambda i, s: (i, jnp.minimum((ch(s) + 1) * per, nt // BF16_ROWS - 1), 0))
    widths = (G_RY, G_G, RNN_W)
    return pl.pallas_call(
        functools.partial(_proj_b_kernel, ct=ct),
        out_shape=[jax.ShapeDtypeStruct((b, nt, w), bf16) for w in widths],
        grid=(b, n_tiles),
        in_specs=[tile(D), pl.BlockSpec((D, G_RY + G_G), lambda i, s: (0, 0), pipeline_mode=pl.Buffered(1)),
                  tile(RNN_W), halo] + _rnn_param_specs(),
        out_specs=[tile(w) for w in widths],
        scratch_shapes=_RNN_SCRATCH,
        compiler_params=_cparams("arbitrary", "arbitrary"), name="proj_b_rnn_bwd",
    )(h, w_b, zrx, zrx, *rnn_prm)


ATT_QBLOCKS = 2


def _attn_kernel(sink_ref, q_ref, kvp_ref, kvc_ref, kvn_ref, kvx_ref, o_ref, *, n_lat_blocks, n_ctx):
    group = N_HEADS // N_KV
    nlb = n_lat_blocks
    lo = lax.broadcasted_iota(i32, (QB, LANES), 1) < HEAD_DIM
    diff = lax.broadcasted_iota(i32, (QB, QB), 1) - lax.broadcasted_iota(i32, (QB, QB), 0)
    zero = jnp.zeros((QB, LANES), bf16)
    work = []
    for sub in range(ATT_QBLOCKS):
        j = pl.program_id(1) * ATT_QBLOCKS + sub
        lat = j < nlb
        ok = (diff >= jnp.where(lat & (j >= 1), 0, QB),
              diff >= jnp.where(lat, -QB, QB),
              -diff >= jnp.where(lat & (j <= nlb - 2), 0, QB))
        rows = slice(sub * QB, (sub + 1) * QB)
        for g in range(N_KV):
            ksl = slice(g * LANES, (g + 1) * LANES)
            vsl = slice(K2_W + g * LANES, K2_W + (g + 1) * LANES)
            window = (kvp_ref, kvc_ref) if sub == 0 else (kvc_ref, kvn_ref)
            k = jnp.concatenate([r[:, ksl] for r in window] + [kvx_ref[:, ksl]], axis=0)
            v = jnp.concatenate([r[:, vsl] for r in window] + [kvx_ref[:, vsl]], axis=0)
            parts = []
            for pr in range(group // 2):
                qp = q_ref[rows, (g * (group // 2) + pr) * LANES:(g * (group // 2) + pr + 1) * LANES]
                parts += [jnp.where(lo, qp, zero), jnp.where(lo, zero, qp)]
            q4 = jnp.concatenate(parts, axis=0)
            s = lax.dot_general(q4, k, (((1,), (1,)), ((), ())), preferred_element_type=f32)
            work.append((rows, g, ok, s, v))
    for rows, g, ok, s, v in work:
        ps, inv = [], []
        for hh in range(group):
            sh = s[hh * QB:(hh + 1) * QB]
            sm = jnp.concatenate([jnp.where(ok[kb], sh[:, kb * QB:(kb + 1) * QB], NEG_INF) for kb in range(3)]
                                 + [sh[:, 3 * QB:]], axis=1)
            sink = sink_ref[0, g * group + hh]
            m = jnp.maximum(jnp.max(sm, axis=1, keepdims=True), sink)
            p = jnp.exp(sm - m)
            inv.append(1.0 / (jnp.sum(p, axis=1, keepdims=True) + jnp.exp(sink - m)))
            ps.append(p.astype(bf16))
        o = jnp.dot(jnp.concatenate(ps, axis=0), v, preferred_element_type=f32) * jnp.concatenate(inv, axis=0)
        for pr in range(group // 2):
            pair = jnp.where(lo, o[(2 * pr) * QB:(2 * pr + 1) * QB], o[(2 * pr + 1) * QB:(2 * pr + 2) * QB])
            c0 = (g * (group // 2) + pr) * LANES
            o_ref[rows, c0:c0 + LANES] = pair.astype(bf16)


def _attention(sink, zq, zkv, n_lat, n_qblocks):
    b, nt, _ = zq.shape
    n_ctx = nt - n_lat
    nlb = n_lat // QB
    cidx = n_lat // n_ctx
    nq = ATT_QBLOCKS
    assert nq == 2 and n_qblocks % nq == 0 and nlb % nq == 0
    edge = lambda d: pl.BlockSpec((None, QB, KV2_W), lambda i, j: (i, jnp.clip(nq * j + d, 0, nlb - 1), 0))
    ctx = pl.BlockSpec((None, n_ctx, KV2_W), lambda i, j: (i, cidx, 0))
    return pl.pallas_call(
        functools.partial(_attn_kernel, n_lat_blocks=nlb, n_ctx=n_ctx),
        out_shape=jax.ShapeDtypeStruct((b, n_qblocks * QB, Q_W), bf16),
        grid=(b, n_qblocks // nq),
        in_specs=[pl.BlockSpec(memory_space=pltpu.SMEM),
                  pl.BlockSpec((None, nq * QB, Q_W), lambda i, j: (i, j, 0)),
                  edge(-1), pl.BlockSpec((None, nq * QB, KV2_W), lambda i, j: (i, j, 0)), edge(nq), ctx],
        out_specs=pl.BlockSpec((None, nq * QB, Q_W), lambda i, j: (i, j, 0)),
        compiler_params=_cparams("arbitrary", "arbitrary"), name="attention",
    )(sink, zq, zkv, zkv, zkv, zkv)


def _rnn_chunk(s, ct, rev):
    return jnp.where(s == 0, ct, (ct - s) if rev else (s - 1))


def _block_scan(a, b, h_in, rowi, rev):
    n = SUBLANES
    order = range(n - 1, -1, -1) if rev else range(n)
    hs = [None] * n
    ps = [None] * n
    prev = None
    for j in order:
        if prev is None:
            hs[j], ps[j] = b[j], a[j]
        else:
            hs[j], ps[j] = a[j] * hs[prev] + b[j], a[j] * ps[prev]
        prev = j
    he, pe = hs[prev], ps[prev]
    for sh in (1, 2, 4):
        rs = n - sh if rev else sh
        msk = (rowi < n - sh) if rev else (rowi >= sh)
        he, pe = (jnp.where(msk, he + pe * pltpu.roll(he, rs, axis=0), he),
                  jnp.where(msk, pe * pltpu.roll(pe, rs, axis=0), pe))
    e = he + pe * h_in
    carry = jnp.where((rowi == n - 1) if rev else (rowi == 0), h_in, pltpu.roll(e, n - 1 if rev else 1, axis=0))
    out = [hs[j] + ps[j] * carry for j in range(n)]
    last = e[0:1, :] if rev else e[n - 1:n, :]
    return out, jnp.broadcast_to(last, (n, LANES))


RNN_LT = RNN_W // LANES


def _rnn_fill(u, hal, ubuf, rev, c0=0):
    for k in range(u.shape[1] // LANES):
        sl = slice(k * LANES, (k + 1) * LANES)
        if rev:
            ubuf[c0 + k, 0:TM, :] = u[:, sl]
            ubuf[c0 + k, TM:TM + BF16_ROWS, :] = hal[:, sl]
        else:
            ubuf[c0 + k, 0:BF16_ROWS, :] = hal[:, sl]
            ubuf[c0 + k, BF16_ROWS:BF16_ROWS + TM, :] = u[:, sl]


def _rnn_tile(c, prm, ubuf, obuf, hcar, rev):
    cw_ref, cb_ref, wg_ref, ba_ref, bx_ref, lam_ref = prm
    base = 0 if rev else BF16_ROWS - (RNN_CONV_K - 1)
    rowi = lax.broadcasted_iota(i32, (SUBLANES, LANES), 0)
    blk_rows = SUBLANES * SUBLANES
    n_blk = TM // blk_rows
    sl = slice(c * LANES, (c + 1) * LANES)
    taps = [cw_ref[k:k + 1, sl] for k in range(RNN_CONV_K)]
    xt = []
    for blk in range(n_blk):
        for j in range(SUBLANES):
            r0 = base + blk * blk_rows + j
            acc = cb_ref[:, sl] + ubuf[c, pl.ds(r0, SUBLANES, stride=SUBLANES), :] * taps[0]
            for k in range(1, RNN_CONV_K):
                acc = acc + ubuf[c, pl.ds(r0 + k, SUBLANES, stride=SUBLANES), :] * taps[k]
            xt.append(acc)
    xp = jnp.concatenate(xt, axis=0)
    g = jnp.dot(xp.astype(bf16), wg_ref[c], preferred_element_type=f32)
    t_r = jnp.tanh(g[:, :RNN_BLK] + 0.5 * ba_ref[:, sl])
    t_i = jnp.tanh(g[:, RNN_BLK:] + 0.5 * bx_ref[:, sl])
    c4 = (-0.5 * RG_C) * jnp.logaddexp(-lam_ref[:, sl], 0.0)
    log_a = c4 * t_r + c4
    a = jnp.exp(log_a)
    y = -jnp.tanh(log_a) * (a * a + 1.0)
    xh = 0.5 * xp
    bb = jnp.where(y > 0.0, y * lax.rsqrt(y), 0.0) * (xh * t_i + xh)
    h_in = hcar[:, sl]
    for blk in (range(n_blk - 1, -1, -1) if rev else range(n_blk)):
        rows = lambda j: slice((blk * SUBLANES + j) * SUBLANES, (blk * SUBLANES + j + 1) * SUBLANES)
        hs, h_in = _block_scan([a[rows(j)] for j in range(SUBLANES)], [bb[rows(j)] for j in range(SUBLANES)],
                               h_in, rowi, rev)
        for j in range(SUBLANES):
            obuf[c, pl.ds(blk * blk_rows + j, SUBLANES, stride=SUBLANES), :] = hs[j]
    hcar[:, sl] = h_in


def _rnn_states(obuf):
    return jnp.concatenate([obuf[c] for c in range(RNN_LT)], axis=1)


_RNN_SCRATCH = [pltpu.VMEM((RNN_LT, TM + BF16_ROWS, LANES), f32),
                pltpu.VMEM((RNN_LT, TM, LANES), f32),
                pltpu.VMEM((SUBLANES, RNN_W), f32)]


def _rnn_param_specs():
    full = lambda shape: pl.BlockSpec(shape, lambda i, s: (0,) * len(shape))
    return [full((RNN_CONV_K, RNN_W)), full((1, RNN_W)), full((RNN_BLOCKS, RNN_BLK, 2 * RNN_BLK)),
            full((1, RNN_W)), full((1, RNN_W)), full((1, RNN_W))]


def _merge_kernel(*refs, ct, split):
    (za_ref, zap_ref, zan_ref, att_ref, hf_ref, hb_ref, ry_ref, zg_ref, mod_ref, ca_ref, wc_ref,
     wa_ref, wr_ref, wo_ref, n2_ref, wrt_ref, shift_ref, xo_ref, h2_ref, aff_ref) = refs[2 if split else 1:]
    t = pl.program_id(1)
    if split:
        x_in = jnp.where(t == ct, refs[1][...], refs[0][...])
    else:
        x_in = refs[0][...]
    cw = CONV_W
    cu_b = za_ref[:, 0:cw]
    sh = jnp.dot(shift_ref[...], cu_b, preferred_element_type=f32)
    prev_ok = (t != 0) & (t != ct)
    next_ok = (t != ct - 1) & (t != ct)
    cu_p = jnp.where(prev_ok, zap_ref[BF16_ROWS - 1:BF16_ROWS, 0:cw].astype(f32), 0.0)
    cu_n = jnp.where(next_ok, zan_ref[0:1, 0:cw].astype(f32), 0.0)
    rowi = lax.broadcasted_iota(i32, (TM, cw), 0)
    cu_prev = jnp.where(rowi == 0, cu_p, sh[:TM])
    cu_next = jnp.where(rowi == TM - 1, cu_n, sh[TM:])
    y = cu_prev * ca_ref[0:1, :] + cu_b.astype(f32) * ca_ref[1:2, :] + cu_next * ca_ref[2:3, :]
    cnv = jnp.dot((za_ref[:, cw:2 * cw].astype(f32) * y).astype(bf16), wc_ref[...], preferred_element_type=f32)
    att = jnp.dot(att_ref[...], wa_ref[...], preferred_element_type=f32)
    rec = hf_ref[...].astype(f32) + hb_ref[...].astype(f32)
    rnn = jnp.dot((ry_ref[...].astype(f32) * rec).astype(bf16), wr_ref[...], preferred_element_type=f32)
    mix2 = None
    for k, br in enumerate((cnv, att, rnn)):
        term = jnp.tanh(zg_ref[:, k * D:(k + 1) * D].astype(f32)) * br + br
        mix2 = term if mix2 is None else mix2 + term
    x = x_in + mod_ref[:, 2 * D:3 * D] * jnp.dot(mix2.astype(bf16), wo_ref[...], preferred_element_type=f32)
    xo_ref[...] = x
    h2 = _rms_mod(x, n2_ref[...], mod_ref[:, 4 * D:5 * D], mod_ref[:, 3 * D:4 * D])
    _rows_to_tiles(h2_ref, h2, TM)
    logits = lax.dot_general(wrt_ref[...], h2.astype(bf16), (((1,), (1,)), ((), ())), preferred_element_type=f32)
    e = jnp.exp(logits - jnp.max(logits, axis=0, keepdims=True))
    aff_ref[...] = e / jnp.sum(e, axis=0, keepdims=True)


def _row_shift_operator():
    r = jnp.arange(2 * TM)[:, None]
    c = jnp.arange(TM)[None, :]
    return (c == jnp.where(r < TM, r - 1, r - TM + 1)).astype(bf16)


def _merge(xs, ctx, za, att, hf, hb, zry, zg, modsel, conv_a, wc, wa, wr, wo, norm2, wrt, n_tiles_eff):
    b, nt, _ = za.shape
    ct = nt // TM - 1
    per = TM // BF16_ROWS
    split = ctx is not None
    tile = lambda w: pl.BlockSpec((None, TM, w), lambda i, t: (i, t, 0))
    full = lambda shape: pl.BlockSpec(shape, lambda i, t: (0,) * len(shape), pipeline_mode=pl.Buffered(1))
    if split:
        x_specs = [pl.BlockSpec((None, TM, D), lambda i, t: (i, jnp.minimum(t, ct - 1), 0)),
                   pl.BlockSpec((None, TM, D), lambda i, t: (i, 0, 0))]
        x_args = [xs, ctx]
    else:
        x_specs = [tile(D)]
        x_args = [xs]
    return pl.pallas_call(
        functools.partial(_merge_kernel, ct=ct, split=split),
        out_shape=[jax.ShapeDtypeStruct((b, n_tiles_eff * TM, D), f32),
                   jax.ShapeDtypeStruct((b, n_tiles_eff * TM * SUBLANES, LANES), f32),
                   jax.ShapeDtypeStruct((b, N_EXPERTS, n_tiles_eff * TM), f32)],
        grid=(b, n_tiles_eff),
        in_specs=x_specs + [tile(ZA_W),
                  pl.BlockSpec((None, BF16_ROWS, ZA_W), lambda i, t: (i, jnp.maximum(t * per - 1, 0), 0)),
                  pl.BlockSpec((None, BF16_ROWS, ZA_W),
                               lambda i, t: (i, jnp.minimum((t + 1) * per, nt // BF16_ROWS - 1), 0)),
                  tile(Q_W), tile(RNN_W), tile(RNN_W), tile(RNN_W), tile(G_G),
                  pl.BlockSpec((None, None, 1, N_MOD * D), lambda i, t: (i, t // ct, 0, 0)),
                  full((3, CONV_W)), full((CONV_W, D)), full((Q_W, D)), full((RNN_W, D)), full((D, D)),
                  full((1, D)), full((N_EXPERTS, D)), full((2 * TM, TM))],
        out_specs=[tile(D), pl.BlockSpec((None, TM * SUBLANES, LANES), lambda i, t: (i, t, 0)),
                   pl.BlockSpec((None, N_EXPERTS, TM), lambda i, t: (i, 0, t))],
        compiler_params=_cparams("arbitrary", "arbitrary"), name="merge",
    )(*x_args, za, za, za, att, hf, hb, zry, zg, modsel, conv_a, wc, wa, wr, wo, norm2, wrt, _row_shift_operator())


ROUTE_QW = 64
ROUTE_TOK_SHIFT = 6
CUM_BLK = 256


def _cumsum_lanes(x, n):
    blk = min(CUM_BLK, n)
    tri = (lax.broadcasted_iota(i32, (blk, blk), 0) <= lax.broadcasted_iota(i32, (blk, blk), 1)).astype(bf16)
    carry = jnp.zeros((x.shape[0], 1), f32)
    outs = []
    for j in range(n // blk):
        c = jnp.dot(x[:, j * blk:(j + 1) * blk].astype(bf16), tri, preferred_element_type=f32) + carry
        outs.append(c)
        carry = c[:, blk - 1:blk]
    return jnp.concatenate(outs, axis=1)


def _route_kernel(aff_ref, idx_ref, val_ref, pos_s, *, n, cap):
    aff = aff_ref[...]
    n_rows = aff.shape[0]
    thr = jnp.zeros((n_rows, 1), i32)
    for bit in range(30, -1, -1):
        cand = thr | (1 << bit)
        cnt = jnp.sum(jnp.where(aff >= lax.bitcast_convert_type(cand, f32), 1.0, 0.0), axis=1, keepdims=True)
        thr = jnp.where(cnt >= float(cap), cand, thr)
    thr_f = lax.bitcast_convert_type(thr, f32)
    gt = aff > thr_f
    eq = aff == thr_f
    need = float(cap) - jnp.sum(jnp.where(gt, 1.0, 0.0), axis=1, keepdims=True)
    sel = gt | (eq & (_cumsum_lanes(eq.astype(f32), n) <= need))
    pos_s[...] = jnp.where(sel, _cumsum_lanes(sel.astype(f32), n).astype(i32) - 1, -1)

    qw = min(ROUTE_QW, cap)
    shift = qw.bit_length() - 1
    tok = lax.broadcasted_iota(i32, (1, n), 1)
    tok_hi = (tok >> ROUTE_TOK_SHIFT).astype(f32)
    tok_lo = (tok & ((1 << ROUTE_TOK_SHIFT) - 1)).astype(f32)
    hi_iota = lax.broadcasted_iota(i32, (SUBLANES, n), 0)
    lo_iota = lax.broadcasted_iota(i32, (qw, n), 0)

    def body(e, carry):
        pos = pos_s[pl.ds(e, 1), :]
        a0 = aff_ref[pl.ds(e, 1), :]
        a_h = a0.astype(bf16).astype(f32)
        a_m = (a0 - a_h).astype(bf16).astype(f32)
        a_l = ((a0 - a_h) - a_m).astype(bf16).astype(f32)
        in_hi = (pos >> shift) == hi_iota
        lhs = jnp.concatenate([jnp.where(in_hi, r, 0.0) for r in (tok_hi, tok_lo, a_h, a_m, a_l)], axis=0)
        onehot = jnp.where((pos & (qw - 1)) == lo_iota, 1.0, 0.0).astype(bf16)
        res = lax.dot_general(lhs.astype(bf16), onehot, (((1,), (1,)), ((), ())), preferred_element_type=f32)
        r_hi, r_lo, v_h, v_m, v_l = (res[k * SUBLANES:(k + 1) * SUBLANES] for k in range(5))
        idx_ref[e] = (r_hi * float(1 << ROUTE_TOK_SHIFT) + r_lo).astype(i32)[:cap // qw]
        val_ref[e] = ((v_h + v_m) + v_l)[:cap // qw]
        return carry

    lax.fori_loop(0, n_rows, body, 0)


def _route(aff_t, n, cap, lane_block):
    b = aff_t.shape[0]
    qw = min(ROUTE_QW, cap)
    assert cap % qw == 0 and cap // qw <= SUBLANES and qw & (qw - 1) == 0
    rows = b * N_EXPERTS
    idx, val = pl.pallas_call(
        functools.partial(_route_kernel, n=n, cap=cap),
        out_shape=[jax.ShapeDtypeStruct((rows, cap // qw, qw), i32),
                   jax.ShapeDtypeStruct((rows, cap // qw, qw), f32)],
        grid=(1,),
        in_specs=[pl.BlockSpec((rows, n), lambda i: (0, lane_block))],
        out_specs=[pl.BlockSpec((rows, cap // qw, qw), lambda i: (0, 0, 0)),
                   pl.BlockSpec((rows, cap // qw, qw), lambda i: (0, 0, 0))],
        scratch_shapes=[pltpu.VMEM((rows, n), i32)],
        compiler_params=_cparams("arbitrary"), name="route",
    )(aff_t.reshape(rows, aff_t.shape[2]))
    return idx.reshape(b, N_EXPERTS, cap), val.reshape(b, N_EXPERTS, cap)


MOE_PAIR = 2


def _gather_kernel(idx_ref, h_ref, tok_ref, tbuf, *, cap):
    sp = cap + SUBLANES
    for k in range(MOE_PAIR):
        base = (pl.program_id(0) * N_EXPERTS + pl.program_id(1) * MOE_PAIR + k) * cap
        for p in range(cap):
            i = idx_ref[base + p]
            tbuf[pl.ds(p, SUBLANES, stride=sp), :] = h_ref[pl.ds(pl.multiple_of(i * SUBLANES, SUBLANES), SUBLANES), :]
        tok_ref[k] = jnp.concatenate([tbuf[pl.ds(c * sp, cap), :] for c in range(D // LANES)], axis=1).astype(bf16)


def _gather(idx, h2t, cap):
    b, rows, _ = h2t.shape
    return pl.pallas_call(
        functools.partial(_gather_kernel, cap=cap),
        out_shape=jax.ShapeDtypeStruct((b, N_EXPERTS, cap, D), bf16),
        grid_spec=pltpu.PrefetchScalarGridSpec(
            num_scalar_prefetch=1, grid=(b, N_EXPERTS // MOE_PAIR),
            in_specs=[pl.BlockSpec((None, rows, LANES), lambda i, e, idx: (i, 0, 0))],
            out_specs=pl.BlockSpec((None, MOE_PAIR, cap, D), lambda i, e, idx: (i, e, 0, 0)),
            scratch_shapes=[pltpu.VMEM((SUBLANES * (cap + SUBLANES), LANES), f32)]),
        compiler_params=_cparams("arbitrary", "arbitrary"), name="moe_gather",
    )(idx, h2t)


def _cast3_kernel(a_ref, b_ref, c_ref, ao_ref, bo_ref, co_ref):
    ao_ref[...] = a_ref[...].astype(bf16)
    bo_ref[...] = b_ref[...].astype(bf16)
    co_ref[...] = c_ref[...].astype(bf16)


def _expert0_bf16(wg, wu, wd, layer):
    spec = pl.BlockSpec((None, None, D, EXPERT_FF), lambda i: (layer, 0, 0, 0))
    ospec = pl.BlockSpec((D, EXPERT_FF), lambda i: (0, 0))
    return pl.pallas_call(
        _cast3_kernel, out_shape=[jax.ShapeDtypeStruct((D, EXPERT_FF), bf16)] * 3, grid=(1,),
        in_specs=[spec] * 3, out_specs=[ospec] * 3, compiler_params=_cparams("arbitrary"), name="expert0_cast",
    )(wg, wu, wd)


def _ffn_kernel(tok_ref, g0_ref, u0_ref, d0_ref, gq_ref, uq_ref, dq_ref, o_ref, w_even, w_odd):
    e = pl.program_id(0)
    i = pl.program_id(1)

    @pl.when((e == 0) & (i == 0))
    def _():
        w_even[0] = g0_ref[...]
        w_even[1] = u0_ref[...]
        w_even[2] = d0_ref[...]

    rows = gq_ref.shape[0]
    r0 = pl.multiple_of(i * rows, rows)

    def step(use, fill):
        n_s, cap, _ = tok_ref.shape
        t = tok_ref[...].reshape(n_s * cap, D)
        g = jnp.dot(t, use[0], preferred_element_type=f32)
        for k, q_ref in enumerate((gq_ref, uq_ref, dq_ref)):
            fill[k, pl.ds(r0, rows), :] = q_ref[...].astype(bf16)
        u = jnp.dot(t, use[1], preferred_element_type=f32)
        a = (g * _sigmoid(g) * u).astype(bf16)
        o_ref[...] = jnp.dot(a, use[2], preferred_element_type=f32).reshape(n_s, cap, D)

    @pl.when(e % 2 == 0)
    def _():
        step(w_even, w_odd)

    @pl.when(e % 2 == 1)
    def _():
        step(w_odd, w_even)


def _ffn(tok, wg, wu, wd, layer):
    b, _, cap, _ = tok.shape
    n_s = 1
    steps = b // n_s
    assert D == EXPERT_FF and D % steps == 0 and (D // steps) % BF16_ROWS == 0
    rows = D // steps
    w0 = _expert0_bf16(wg, wu, wd, layer)
    w0spec = pl.BlockSpec((D, EXPERT_FF), lambda e, i: (0, 0), pipeline_mode=pl.Buffered(1))
    qspec = pl.BlockSpec((None, None, rows, D), lambda e, i: (layer, jnp.minimum(e + 1, N_EXPERTS - 1), i, 0))
    return pl.pallas_call(
        _ffn_kernel,
        out_shape=jax.ShapeDtypeStruct((b, N_EXPERTS, cap, D), f32),
        grid=(N_EXPERTS, steps),
        in_specs=[pl.BlockSpec((n_s, None, cap, D), lambda e, i: (i, e, 0, 0)),
                  w0spec, w0spec, w0spec, qspec, qspec, qspec],
        out_specs=pl.BlockSpec((n_s, None, cap, D), lambda e, i: (i, e, 0, 0)),
        scratch_shapes=[pltpu.VMEM((3, D, EXPERT_FF), bf16), pltpu.VMEM((3, D, EXPERT_FF), bf16)],
        compiler_params=_cparams("arbitrary", "arbitrary"), name="moe_ffn",
    )(tok, *w0, wg, wu, wd)


SCATTER_UNROLL = 8


def _scatter_kernel(idx_ref, val_ref, o_ref, acc_ref, tbuf, *, cap):
    e = pl.program_id(1)
    sp = cap + SUBLANES

    @pl.when(e == 0)
    def _():
        acc_ref[...] = jnp.zeros_like(acc_ref)

    for k in range(MOE_PAIR):
        base = (pl.program_id(0) * N_EXPERTS + e * MOE_PAIR + k) * cap
        for c in range(D // LANES):
            tbuf[pl.ds(c * sp, cap), :] = o_ref[k, :, c * LANES:(c + 1) * LANES]
        u = min(SCATTER_UNROLL, cap)
        for p0 in range(0, cap, u):
            pend = []
            for p in range(p0, p0 + u):
                r0 = pl.multiple_of(idx_ref[base + p] * SUBLANES, SUBLANES)
                pend.append((r0, acc_ref[pl.ds(r0, SUBLANES), :]
                             + tbuf[pl.ds(p, SUBLANES, stride=sp), :] * val_ref[base + p]))
            for r0, v in pend:
                acc_ref[pl.ds(r0, SUBLANES), :] = v


def _scatter(idx, val, o, rows):
    b, _, cap, _ = o.shape
    return pl.pallas_call(
        functools.partial(_scatter_kernel, cap=cap),
        out_shape=jax.ShapeDtypeStruct((b, rows, LANES), f32),
        grid_spec=pltpu.PrefetchScalarGridSpec(
            num_scalar_prefetch=2, grid=(b, N_EXPERTS // MOE_PAIR),
            in_specs=[pl.BlockSpec((None, MOE_PAIR, cap, D), lambda i, e, idx, val: (i, e, 0, 0))],
            out_specs=pl.BlockSpec((None, rows, LANES), lambda i, e, idx, val: (i, 0, 0)),
            scratch_shapes=[pltpu.VMEM((SUBLANES * (cap + SUBLANES), LANES), f32)]),
        compiler_params=_cparams("arbitrary", "arbitrary"), name="moe_scatter",
    )(idx, val, o)


def _moe(aff_t, h2t, wg, wu, wd, layer, n_lat, n_ctx):
    idx, val = _route(aff_t, n_lat, CAP_FACTOR * n_lat // N_EXPERTS, 0)
    if n_ctx:
        idx_c, val_c = _route(aff_t, n_ctx, CAP_FACTOR * n_ctx // N_EXPERTS, n_lat // n_ctx)
        idx = jnp.concatenate([idx, idx_c + n_lat], axis=-1)
        val = jnp.concatenate([val, val_c], axis=-1)
    cap = idx.shape[-1]
    idx = idx.reshape(-1)
    val = val.reshape(-1)
    tok = _gather(idx, h2t, cap)
    o = _ffn(tok, wg, wu, wd, layer)
    return _scatter(idx, val, o, h2t.shape[1])


def _final_kernel(x_ref, acc_ref, mod_ref, n_ref, o_ref):
    x = x_ref[...] + mod_ref[:, 5 * D:6 * D] * _tiles_to_rows(acc_ref, x_ref.shape[0])
    o_ref[...] = (x * lax.rsqrt(jnp.mean(x * x, axis=-1, keepdims=True) + EPS)) * n_ref[...]


def _final(xs, acc, modsel, norm, n_lat):
    b = xs.shape[0]
    tm = 2 * TM if n_lat % (2 * TM) == 0 else TM
    return pl.pallas_call(
        _final_kernel,
        out_shape=jax.ShapeDtypeStruct((b, n_lat, D), f32),
        grid=(b, n_lat // tm),
        in_specs=[pl.BlockSpec((None, tm, D), lambda i, t: (i, t, 0)),
                  pl.BlockSpec((None, tm * SUBLANES, LANES), lambda i, t: (i, t, 0)),
                  pl.BlockSpec((None, None, 1, N_MOD * D), lambda i, t: (i, 0, 0, 0)),
                  pl.BlockSpec((1, D), lambda i, t: (0, 0))],
        out_specs=pl.BlockSpec((None, tm, D), lambda i, t: (i, t, 0)),
        compiler_params=_cparams("arbitrary", "arbitrary"), name="final_norm",
    )(xs, acc, modsel, norm)


def _rope_tables(n_lat, n_ctx):
    pos = jnp.arange(n_lat, dtype=f32)
    rows = jnp.floor(pos / GRID_W)
    cols = pos - rows * GRID_W
    half = HEAD_DIM // 4
    inv = 1.0 / (ROPE_BASE ** (jnp.arange(0, 2 * half, 2, dtype=f32) / (2 * half)))
    lane = jnp.arange(LANES)
    d = lane % HEAD_DIM
    ang = jnp.where((d < 2 * half)[None, :], rows[:, None], cols[:, None]) * inv[d % half][None, :]
    first = ((d % (2 * half)) < half)[None, :]
    cos = jnp.cos(ang)
    sin = jnp.sin(ang)
    s1 = jnp.where(first, -sin, 0.0)
    s2 = jnp.where(first, 0.0, sin)
    pad = lambda t, v: jnp.concatenate([t, jnp.full((n_ctx, LANES), v, f32)], axis=0)
    return pad(cos, 1.0), pad(s1, 0.0), pad(s2, 0.0)


def _in_weights_bf16(w):
    scale = jnp.concatenate([jnp.ones((IN_TOTAL - G_G,), f32), jnp.full((G_G,), 0.5, f32)])
    return (w * scale[None, :]).astype(bf16)


def kernel(x, c, ctx, c_ctx, norm1, norm2, w_mod, b_mod, w_in, conv_a, w_conv_out, attn_sink, w_attn_out, rnn_conv_w,
           rnn_conv_b, rnn_w_a, rnn_b_a, rnn_w_x, rnn_b_x, rnn_lam, w_rnn_out, w_o, w_router, w_e_gate, w_e_up,
           w_e_down, final_norm):
    b, n_lat, d = x.shape
    n_ctx = ctx.shape[1]
    depth = w_mod.shape[0]
    assert d == D and n_ctx == TM and n_lat % TM == 0 and b < SUBLANES
    nt = n_lat + n_ctx
    ct = nt // TM - 1

    cstack = jnp.concatenate([c, c_ctx[None, :], jnp.zeros((SUBLANES - b - 1, D), f32)], axis=0)
    mod = _modulation(cstack, w_mod, b_mod)
    tabs = _rope_tables(n_lat, n_ctx)

    xs = x
    acc = None
    modsel_prev = None
    for l in range(depth):
        ctx_out = l < depth - 1
        ctx_sep = ctx if l == 0 else None
        modsel = jnp.stack([mod[l, :b], jnp.broadcast_to(mod[l, b], (b, N_MOD * D))], axis=1)[:, :, None, :]
        rnn_prm = []
        for dr in range(2):
            wg = (0.5 * jnp.concatenate([rnn_w_a[l, dr], rnn_w_x[l, dr]], axis=-1)).astype(bf16)
            rnn_prm.append((rnn_conv_w[l, dr], rnn_conv_b[l, dr][None, :], wg, rnn_b_a[l, dr][None, :],
                            rnn_b_x[l, dr][None, :], rnn_lam[l, dr][None, :]))
        w_all = _in_weights_bf16(w_in[l])
        outs = _proj_a(xs, ctx_sep, acc, modsel_prev, modsel, norm1[l][None, :], tabs, w_all[:, :W_A_COLS],
                       rnn_prm[0])
        if acc is not None:
            xs, outs = outs[0], outs[1:]
        za, zq, zkv, zrx, hn, hf = outs
        zry, zg, hb = _proj_b(hn, w_all[:, W_A_COLS:], zrx, rnn_prm[1])
        att = _attention(attn_sink[l][None, :], zq, zkv, n_lat, (nt if ctx_out else n_lat) // QB)
        xs, h2t, aff_t = _merge(xs, ctx_sep, za, att, hf, hb, zry, zg, modsel, conv_a[l],
                                w_conv_out[l].astype(bf16),
                                w_attn_out[l].astype(bf16), (0.5 * w_rnn_out[l]).astype(bf16),
                                (0.5 * w_o[l]).astype(bf16),
                                norm2[l][None, :], w_router[l].T.astype(bf16), (ct + 1) if ctx_out else ct)
        acc = _moe(aff_t, h2t, w_e_gate, w_e_up, w_e_down, l, n_lat, n_ctx if ctx_out else 0)
        modsel_prev = modsel
    return _final(xs, acc, modsel_prev, final_norm[None, :], n_lat)
```

```python
import functools

import jax
import jax.numpy as jnp
from jax import lax
from jax.experimental import pallas as pl
from jax.experimental.pallas import tpu as pltpu

f32 = jnp.float32
bf16 = jnp.bfloat16
i32 = jnp.int32

D = 1024
EPS = 1e-6
CONV_W = 512
N_HEADS = 8
N_KV = 2
HEAD_DIM = 64
Q_W = N_HEADS * HEAD_DIM
KV_W = N_KV * HEAD_DIM
WINDOW = 128
GRID_W = 64
ROPE_BASE = 10000.0
NEG_INF = -1e30
RNN_W = 1024
RNN_BLOCKS = 8
RNN_BLK = RNN_W // RNN_BLOCKS
RNN_CONV_K = 4
RG_C = 8.0
N_EXPERTS = 16
EXPERT_FF = 1024
CAP_FACTOR = 2
N_MOD = 6

LANES = 128
SUBLANES = 8
BF16_ROWS = 16
VMEM_LIMIT = 56 * 1024 * 1024

TM = 256
QB = 128
K2_W = 2 * KV_W

G_A = 3 * CONV_W
G_QKV = Q_W + 2 * KV_W
G_RX = RNN_W
G_RY = RNN_W
G_G = 3 * D
IN_COLS = (G_A, G_QKV, G_RX, G_RY, G_G)
IN_TOTAL = sum(IN_COLS)
KV2_W = 2 * K2_W
ZA_W = 2 * CONV_W
OUT_COLS = (ZA_W, Q_W, KV2_W, G_RX, G_RY, G_G)


def _cparams(*sem):
    return pltpu.CompilerParams(dimension_semantics=sem, vmem_limit_bytes=VMEM_LIMIT)


def _sigmoid(x):
    return 0.5 * jnp.tanh(0.5 * x) + 0.5


def _rms_mod(x, g, sc, sh):
    y = x * lax.rsqrt(jnp.mean(x * x, axis=-1, keepdims=True) + EPS)
    return (y * g) * (1.0 + sc) + sh


def _mod_kernel(c_ref, w_ref, b_ref, o_ref):
    c = c_ref[...]
    s = (c * _sigmoid(c)).astype(bf16)
    o_ref[...] = jnp.dot(s, w_ref[...].astype(bf16), preferred_element_type=f32) + b_ref[...]


def _modulation(cstack, w_mod, b_mod):
    depth = w_mod.shape[0]
    tn = 1536
    return pl.pallas_call(
        _mod_kernel,
        out_shape=jax.ShapeDtypeStruct((depth, SUBLANES, N_MOD * D), f32),
        grid=(depth, N_MOD * D // tn),
        in_specs=[
            pl.BlockSpec((SUBLANES, D), lambda l, j: (0, 0)),
            pl.BlockSpec((None, D, tn), lambda l, j: (l, 0, j)),
            pl.BlockSpec((None, 1, tn), lambda l, j: (l, 0, j)),
        ],
        out_specs=pl.BlockSpec((None, SUBLANES, tn), lambda l, j: (l, 0, j)),
        compiler_params=_cparams("arbitrary", "arbitrary"),
        name="modulation",
    )(cstack, w_mod, b_mod.reshape(depth, 1, N_MOD * D))


def _tiles_to_rows(ref, rows):
    return jnp.concatenate([ref[pl.ds(c, rows, stride=SUBLANES), :] for c in range(D // LANES)], axis=1)


def _rows_to_tiles(ref, val, rows):
    per = D // LANES
    for r in range(rows // SUBLANES):
        for c in range(per):
            ref[pl.ds(r * SUBLANES * per + c, SUBLANES, stride=per), :] = val[r * SUBLANES:(r + 1) * SUBLANES,
                                                                              c * LANES:(c + 1) * LANES]


def _rope(z, cos, s1, s2):
    outs = []
    for c in range(z.shape[1] // LANES):
        x = z[:, c * LANES:(c + 1) * LANES]
        outs.append(x * cos + pltpu.roll(x, LANES - 16, axis=1) * s1 + pltpu.roll(x, 16, axis=1) * s2)
    return jnp.concatenate(outs, axis=1)


W_A_COLS = G_A + G_QKV + G_RX


def _proj_a_kernel(*refs, with_moe, ct):
    if with_moe:
        (x_ref, acc_ref, modp_ref), refs = refs[:3], refs[3:]
    else:
        (x_ref, ctx_ref), refs = refs[:2], refs[2:]
    mod_ref, n_ref, cos_ref, s1_ref, s2_ref, w_ref = refs[:6]
    prm, refs = refs[6:12], refs[12:]
    if with_moe:
        xo_ref, refs = refs[0], refs[1:]
    za_ref, zq_ref, zkv_ref, zrx_ref, h_ref, hf_ref, ubuf, obuf, hcar, halo_s = refs
    s = pl.program_id(1)
    chunk = _rnn_chunk(s, ct, False)

    @pl.when(s == 0)
    def _():
        hcar[...] = jnp.zeros_like(hcar)
        halo_s[...] = jnp.zeros_like(halo_s)

    if with_moe:
        x = x_ref[...] + modp_ref[:, 5 * D:6 * D] * _tiles_to_rows(acc_ref, TM)
        xo_ref[...] = x
    else:
        x = jnp.where(chunk == ct, ctx_ref[...], x_ref[...])
    h = _rms_mod(x, n_ref[...], mod_ref[:, D:2 * D], mod_ref[:, 0:D]).astype(bf16)
    h_ref[...] = h
    hal = jnp.where((chunk != 0) & (chunk != ct), halo_s[...], 0.0)
    quarter = G_RX // 4

    def scan_input(p):
        cols = slice(p * quarter, (p + 1) * quarter)
        u = jnp.dot(h, w_ref[:, G_A + G_QKV + p * quarter:G_A + G_QKV + (p + 1) * quarter],
                    preferred_element_type=f32)
        zrx_ref[:, cols] = u.astype(bf16)
        _rnn_fill(u, hal[:, cols], ubuf, False, c0=p * quarter // LANES)
        halo_s[:, cols] = u[TM - BF16_ROWS:, :]

    scan = lambda c: _rnn_tile(c, prm, ubuf, obuf, hcar, False)
    cw = CONV_W
    scan_input(0)
    a_x = jnp.dot(h, w_ref[:, 0:cw], preferred_element_type=f32)
    scan(0)
    scan_input(1)
    scan(1)
    a_b = jnp.dot(h, w_ref[:, cw:2 * cw], preferred_element_type=f32)
    scan(2)
    scan_input(2)
    scan(3)
    a_c = jnp.dot(h, w_ref[:, 2 * cw:G_A], preferred_element_type=f32)
    za_ref[...] = jnp.concatenate([a_c * a_x, a_b], axis=1).astype(bf16)
    scan(4)
    scan_input(3)
    scan(5)
    z = jnp.dot(h, w_ref[:, G_A:G_A + G_QKV], preferred_element_type=f32)
    qk = _rope(z[:, :Q_W + KV_W], cos_ref[...], s1_ref[...], s2_ref[...])
    zq_ref[...] = (qk[:, :Q_W] * (HEAD_DIM ** -0.5)).astype(bf16)
    lo = lax.broadcasted_iota(i32, (TM, LANES), 1) < HEAD_DIM
    dup = []
    for pair in (qk[:, Q_W:], z[:, Q_W + KV_W:]):
        swapped = pltpu.roll(pair, HEAD_DIM, axis=1)
        dup += [jnp.where(lo, pair, swapped), jnp.where(lo, swapped, pair)]
    zkv_ref[...] = jnp.concatenate(dup, axis=1).astype(bf16)
    scan(6)
    scan(7)
    hf_ref[...] = _rnn_states(obuf).astype(bf16)


def _proj_b_kernel(h_ref, w_ref, u_ref, halo_ref, *refs, ct):
    prm, (zry_ref, zg_ref, hb_ref, ubuf, obuf, hcar) = refs[:6], refs[6:]
    s = pl.program_id(1)
    chunk = _rnn_chunk(s, ct, True)

    @pl.when(s == 0)
    def _():
        hcar[...] = jnp.zeros_like(hcar)

    h = h_ref[...]
    hal = jnp.where((chunk != ct - 1) & (chunk != ct), halo_ref[...].astype(f32), 0.0)
    _rnn_fill(u_ref[...].astype(f32), hal, ubuf, True)
    slab = (G_RY + G_G) // RNN_LT
    for c in range(RNN_LT):
        z = jnp.dot(h, w_ref[:, c * slab:(c + 1) * slab], preferred_element_type=f32)
        if (c + 1) * slab <= G_RY:
            t_y = jnp.tanh(z * (0.7978845608028654 + (0.7978845608028654 * 0.044715) * (z * z)))
            zry_ref[:, c * slab:(c + 1) * slab] = (z * t_y + z).astype(bf16)
        else:
            zg_ref[:, c * slab - G_RY:(c + 1) * slab - G_RY] = z.astype(bf16)
        _rnn_tile(c, prm, ubuf, obuf, hcar, True)
    hb_ref[...] = _rnn_states(obuf).astype(bf16)


def _proj_a(xs, ctx, acc, modsel_prev, modsel, norm, tabs, w_a, rnn_prm):
    with_moe = acc is not None
    b = xs.shape[0]
    nt = xs.shape[1] + (0 if with_moe else ctx.shape[1])
    n_tiles = nt // TM
    ct = n_tiles - 1
    ch = lambda s: _rnn_chunk(s, ct, False)
    tile = lambda w: pl.BlockSpec((None, TM, w), lambda i, s: (i, ch(s), 0))
    mod_spec = pl.BlockSpec((None, None, 1, N_MOD * D), lambda i, s: (i, ch(s) // ct, 0, 0))
    tab_spec = pl.BlockSpec((TM, LANES), lambda i, s: (ch(s), 0))
    if with_moe:
        in_specs = [tile(D), pl.BlockSpec((None, TM * SUBLANES, LANES), lambda i, s: (i, ch(s), 0)), mod_spec]
        args = [xs, acc, modsel_prev]
    else:
        in_specs = [pl.BlockSpec((None, TM, D), lambda i, s: (i, jnp.minimum(ch(s), ct - 1), 0)),
                    pl.BlockSpec((None, TM, D), lambda i, s: (i, 0, 0))]
        args = [xs, ctx]
    in_specs += [mod_spec, pl.BlockSpec((1, D), lambda i, s: (0, 0)), tab_spec, tab_spec, tab_spec,
                 pl.BlockSpec((D, W_A_COLS), lambda i, s: (0, 0), pipeline_mode=pl.Buffered(1))]
    in_specs += _rnn_param_specs()
    args += [modsel, norm, *tabs, w_a, *rnn_prm]
    widths = (ZA_W, Q_W, KV2_W, G_RX, D, RNN_W)
    out_shape = [jax.ShapeDtypeStruct((b, nt, w), bf16) for w in widths]
    out_specs = [tile(w) for w in widths]
    if with_moe:
        out_shape = [jax.ShapeDtypeStruct((b, nt, D), f32)] + out_shape
        out_specs = [tile(D)] + out_specs
    return pl.pallas_call(
        functools.partial(_proj_a_kernel, with_moe=with_moe, ct=ct),
        out_shape=out_shape, grid=(b, n_tiles), in_specs=in_specs, out_specs=out_specs,
        scratch_shapes=_RNN_SCRATCH + [pltpu.VMEM((BF16_ROWS, RNN_W), f32)],
        compiler_params=_cparams("arbitrary", "arbitrary"), name="proj_a_rnn_fwd",
    )(*args)


def _proj_b(h, w_b, zrx, rnn_prm):
    b, nt, _ = h.shape
    n_tiles = nt // TM
    ct = n_tiles - 1
    per = TM // BF16_ROWS
    ch = lambda s: _rnn_chunk(s, ct, True)
    tile = lambda w: pl.BlockSpec((None, TM, w), lambda i, s: (i, ch(s), 0))
    halo = pl.BlockSpec((None, BF16_ROWS, RNN_W),
                        lambda i, s: (i, jnp.minimum((ch(s) + 1) * per, nt // BF16_ROWS - 1), 0))
    widths = (G_RY, G_G, RNN_W)
    return pl.pallas_call(
        functools.partial(_proj_b_kernel, ct=ct),
        out_shape=[jax.ShapeDtypeStruct((b, nt, w), bf16) for w in widths],
        grid=(b, n_tiles),
        in_specs=[tile(D), pl.BlockSpec((D, G_RY + G_G), lambda i, s: (0, 0), pipeline_mode=pl.Buffered(1)),
                  tile(RNN_W), halo] + _rnn_param_specs(),
        out_specs=[tile(w) for w in widths],
        scratch_shapes=_RNN_SCRATCH,
        compiler_params=_cparams("arbitrary", "arbitrary"), name="proj_b_rnn_bwd",
    )(h, w_b, zrx, zrx, *rnn_prm)


ATT_QBLOCKS = 2


def _attn_kernel(sink_ref, q_ref, kvp_ref, kvc_ref, kvn_ref, kvx_ref, o_ref, *, n_lat_blocks, n_ctx):
    group = N_HEADS // N_KV
    nlb = n_lat_blocks
    lo = lax.broadcasted_iota(i32, (QB, LANES), 1) < HEAD_DIM
    diff = lax.broadcasted_iota(i32, (QB, QB), 1) - lax.broadcasted_iota(i32, (QB, QB), 0)
    zero = jnp.zeros((QB, LANES), bf16)
    def score(sub, g):
        window = (kvp_ref, kvc_ref) if sub == 0 else (kvc_ref, kvn_ref)
        ksl = slice(g * LANES, (g + 1) * LANES)
        k = jnp.concatenate([r[:, ksl] for r in window] + [kvx_ref[:, ksl]], axis=0)
        parts = []
        for pr in range(group // 2):
            qp = q_ref[sub * QB:(sub + 1) * QB, (g * (group // 2) + pr) * LANES:(g * (group // 2) + pr + 1) * LANES]
            parts += [jnp.where(lo, qp, zero), jnp.where(lo, zero, qp)]
        return lax.dot_general(jnp.concatenate(parts, axis=0), k, (((1,), (1,)), ((), ())),
                               preferred_element_type=f32)

    def finish(sub, g, s):
        j = pl.program_id(1) * ATT_QBLOCKS + sub
        lat = j < nlb
        ok = (diff >= jnp.where(lat & (j >= 1), 0, QB),
              diff >= jnp.where(lat, -QB, QB),
              -diff >= jnp.where(lat & (j <= nlb - 2), 0, QB))
        rows = slice(sub * QB, (sub + 1) * QB)
        window = (kvp_ref, kvc_ref) if sub == 0 else (kvc_ref, kvn_ref)
        vsl = slice(K2_W + g * LANES, K2_W + (g + 1) * LANES)
        v = jnp.concatenate([r[:, vsl] for r in window] + [kvx_ref[:, vsl]], axis=0)
        ps, inv = [], []
        for hh in range(group):
            sh = s[hh * QB:(hh + 1) * QB]
            sm = jnp.concatenate([jnp.where(ok[kb], sh[:, kb * QB:(kb + 1) * QB], NEG_INF) for kb in range(3)]
                                 + [sh[:, 3 * QB:]], axis=1)
            sink = sink_ref[0, g * group + hh]
            m = jnp.maximum(jnp.max(sm, axis=1, keepdims=True), sink)
            p = jnp.exp(sm - m)
            inv.append(1.0 / (jnp.sum(p, axis=1, keepdims=True) + jnp.exp(sink - m)))
            ps.append(p.astype(bf16))
        o = jnp.dot(jnp.concatenate(ps, axis=0), v, preferred_element_type=f32) * jnp.concatenate(inv, axis=0)
        for pr in range(group // 2):
            pair = jnp.where(lo, o[(2 * pr) * QB:(2 * pr + 1) * QB], o[(2 * pr + 1) * QB:(2 * pr + 2) * QB])
            c0 = (g * (group // 2) + pr) * LANES
            o_ref[rows, c0:c0 + LANES] = pair.astype(bf16)

    units = [(sub, g) for sub in range(ATT_QBLOCKS) for g in range(N_KV)]
    pending = score(*units[0])
    for n, unit in enumerate(units):
        ahead = score(*units[n + 1]) if n + 1 < len(units) else None
        finish(*unit, pending)
        pending = ahead


def _attention(sink, zq, zkv, n_lat, n_qblocks):
    b, nt, _ = zq.shape
    n_ctx = nt - n_lat
    nlb = n_lat // QB
    cidx = n_lat // n_ctx
    nq = ATT_QBLOCKS
    assert nq == 2 and n_qblocks % nq == 0 and nlb % nq == 0
    edge = lambda d: pl.BlockSpec((None, QB, KV2_W), lambda i, j: (i, jnp.clip(nq * j + d, 0, nlb - 1), 0))
    ctx = pl.BlockSpec((None, n_ctx, KV2_W), lambda i, j: (i, cidx, 0))
    return pl.pallas_call(
        functools.partial(_attn_kernel, n_lat_blocks=nlb, n_ctx=n_ctx),
        out_shape=jax.ShapeDtypeStruct((b, n_qblocks * QB, Q_W), bf16),
        grid=(b, n_qblocks // nq),
        in_specs=[pl.BlockSpec(memory_space=pltpu.SMEM),
                  pl.BlockSpec((None, nq * QB, Q_W), lambda i, j: (i, j, 0)),
                  edge(-1), pl.BlockSpec((None, nq * QB, KV2_W), lambda i, j: (i, j, 0)), edge(nq), ctx],
        out_specs=pl.BlockSpec((None, nq * QB, Q_W), lambda i, j: (i, j, 0)),
        compiler_params=_cparams("arbitrary", "arbitrary"), name="attention",
    )(sink, zq, zkv, zkv, zkv, zkv)


def _rnn_chunk(s, ct, rev):
    return jnp.where(s == 0, ct, (ct - s) if rev else (s - 1))


def _block_scan(a, b, h_in, rowi, rev):
    n = SUBLANES
    order = range(n - 1, -1, -1) if rev else range(n)
    hs = [None] * n
    ps = [None] * n
    prev = None
    for j in order:
        if prev is None:
            hs[j], ps[j] = b[j], a[j]
        else:
            hs[j], ps[j] = a[j] * hs[prev] + b[j], a[j] * ps[prev]
        prev = j
    he, pe = hs[prev], ps[prev]
    for sh in (1, 2, 4):
        rs = n - sh if rev else sh
        msk = (rowi < n - sh) if rev else (rowi >= sh)
        he, pe = (jnp.where(msk, he + pe * pltpu.roll(he, rs, axis=0), he),
                  jnp.where(msk, pe * pltpu.roll(pe, rs, axis=0), pe))
    e = he + pe * h_in
    carry = jnp.where((rowi == n - 1) if rev else (rowi == 0), h_in, pltpu.roll(e, n - 1 if rev else 1, axis=0))
    out = [hs[j] + ps[j] * carry for j in range(n)]
    last = e[0:1, :] if rev else e[n - 1:n, :]
    return out, jnp.broadcast_to(last, (n, LANES))


RNN_LT = RNN_W // LANES


def _rnn_fill(u, hal, ubuf, rev, c0=0):
    for k in range(u.shape[1] // LANES):
        sl = slice(k * LANES, (k + 1) * LANES)
        if rev:
            ubuf[c0 + k, 0:TM, :] = u[:, sl]
            ubuf[c0 + k, TM:TM + BF16_ROWS, :] = hal[:, sl]
        else:
            ubuf[c0 + k, 0:BF16_ROWS, :] = hal[:, sl]
            ubuf[c0 + k, BF16_ROWS:BF16_ROWS + TM, :] = u[:, sl]


def _rnn_tile(c, prm, ubuf, obuf, hcar, rev):
    cw_ref, cb_ref, wg_ref, ba_ref, bx_ref, lam_ref = prm
    base = 0 if rev else BF16_ROWS - (RNN_CONV_K - 1)
    rowi = lax.broadcasted_iota(i32, (SUBLANES, LANES), 0)
    blk_rows = SUBLANES * SUBLANES
    n_blk = TM // blk_rows
    sl = slice(c * LANES, (c + 1) * LANES)
    taps = [cw_ref[k:k + 1, sl] for k in range(RNN_CONV_K)]
    xt = []
    for blk in range(n_blk):
        for j in range(SUBLANES):
            r0 = base + blk * blk_rows + j
            acc = cb_ref[:, sl] + ubuf[c, pl.ds(r0, SUBLANES, stride=SUBLANES), :] * taps[0]
            for k in range(1, RNN_CONV_K):
                acc = acc + ubuf[c, pl.ds(r0 + k, SUBLANES, stride=SUBLANES), :] * taps[k]
            xt.append(acc)
    xp = jnp.concatenate(xt, axis=0)
    g = jnp.dot(xp.astype(bf16), wg_ref[c], preferred_element_type=f32)
    t_r = jnp.tanh(g[:, :RNN_BLK] + 0.5 * ba_ref[:, sl])
    t_i = jnp.tanh(g[:, RNN_BLK:] + 0.5 * bx_ref[:, sl])
    c4 = (-0.5 * RG_C) * jnp.logaddexp(-lam_ref[:, sl], 0.0)
    log_a = c4 * t_r + c4
    a = jnp.exp(log_a)
    y = -jnp.tanh(log_a) * (a * a + 1.0)
    xh = 0.5 * xp
    bb = jnp.where(y > 0.0, y * lax.rsqrt(y), 0.0) * (xh * t_i + xh)
    h_in = hcar[:, sl]
    for blk in (range(n_blk - 1, -1, -1) if rev else range(n_blk)):
        rows = lambda j: slice((blk * SUBLANES + j) * SUBLANES, (blk * SUBLANES + j + 1) * SUBLANES)
        hs, h_in = _block_scan([a[rows(j)] for j in range(SUBLANES)], [bb[rows(j)] for j in range(SUBLANES)],
                               h_in, rowi, rev)
        for j in range(SUBLANES):
            obuf[c, pl.ds(blk * blk_rows + j, SUBLANES, stride=SUBLANES), :] = hs[j]
    hcar[:, sl] = h_in


def _rnn_states(obuf):
    return jnp.concatenate([obuf[c] for c in range(RNN_LT)], axis=1)


_RNN_SCRATCH = [pltpu.VMEM((RNN_LT, TM + BF16_ROWS, LANES), f32),
                pltpu.VMEM((RNN_LT, TM, LANES), f32),
                pltpu.VMEM((SUBLANES, RNN_W), f32)]


def _rnn_param_specs():
    full = lambda shape: pl.BlockSpec(shape, lambda i, s: (0,) * len(shape))
    return [full((RNN_CONV_K, RNN_W)), full((1, RNN_W)), full((RNN_BLOCKS, RNN_BLK, 2 * RNN_BLK)),
            full((1, RNN_W)), full((1, RNN_W)), full((1, RNN_W))]


def _merge_kernel(*refs, ct, split):
    (za_ref, zap_ref, zan_ref, att_ref, hf_ref, hb_ref, ry_ref, zg_ref, mod_ref, ca_ref, wc_ref,
     wa_ref, wr_ref, wo_ref, n2_ref, wrt_ref, shift_ref, xo_ref, h2_ref, aff_ref) = refs[2 if split else 1:]
    t = pl.program_id(1)
    if split:
        x_in = jnp.where(t == ct, refs[1][...], refs[0][...])
    else:
        x_in = refs[0][...]
    cw = CONV_W
    cu_b = za_ref[:, 0:cw]
    sh = jnp.dot(shift_ref[...], cu_b, preferred_element_type=f32)
    prev_ok = (t != 0) & (t != ct)
    next_ok = (t != ct - 1) & (t != ct)
    cu_p = jnp.where(prev_ok, zap_ref[BF16_ROWS - 1:BF16_ROWS, 0:cw].astype(f32), 0.0)
    cu_n = jnp.where(next_ok, zan_ref[0:1, 0:cw].astype(f32), 0.0)
    rowi = lax.broadcasted_iota(i32, (TM, cw), 0)
    cu_prev = jnp.where(rowi == 0, cu_p, sh[:TM])
    cu_next = jnp.where(rowi == TM - 1, cu_n, sh[TM:])
    y = cu_prev * ca_ref[0:1, :] + cu_b.astype(f32) * ca_ref[1:2, :] + cu_next * ca_ref[2:3, :]
    cnv = jnp.dot((za_ref[:, cw:2 * cw].astype(f32) * y).astype(bf16), wc_ref[...], preferred_element_type=f32)
    att = jnp.dot(att_ref[...], wa_ref[...], preferred_element_type=f32)
    rec = hf_ref[...].astype(f32) + hb_ref[...].astype(f32)
    rnn = jnp.dot((ry_ref[...].astype(f32) * rec).astype(bf16), wr_ref[...], preferred_element_type=f32)
    mix2 = None
    for k, br in enumerate((cnv, att, rnn)):
        term = jnp.tanh(zg_ref[:, k * D:(k + 1) * D].astype(f32)) * br + br
        mix2 = term if mix2 is None else mix2 + term
    x = x_in + mod_ref[:, 2 * D:3 * D] * jnp.dot(mix2.astype(bf16), wo_ref[...], preferred_element_type=f32)
    xo_ref[...] = x
    h2 = _rms_mod(x, n2_ref[...], mod_ref[:, 4 * D:5 * D], mod_ref[:, 3 * D:4 * D])
    _rows_to_tiles(h2_ref, h2, TM)
    logits = lax.dot_general(wrt_ref[...], h2.astype(bf16), (((1,), (1,)), ((), ())), preferred_element_type=f32)
    e = jnp.exp(logits - jnp.max(logits, axis=0, keepdims=True))
    aff_ref[...] = e / jnp.sum(e, axis=0, keepdims=True)


def _row_shift_operator():
    r = jnp.arange(2 * TM)[:, None]
    c = jnp.arange(TM)[None, :]
    return (c == jnp.where(r < TM, r - 1, r - TM + 1)).astype(bf16)


def _merge(xs, ctx, za, att, hf, hb, zry, zg, modsel, conv_a, wc, wa, wr, wo, norm2, wrt, n_tiles_eff):
    b, nt, _ = za.shape
    ct = nt // TM - 1
    per = TM // BF16_ROWS
    split = ctx is not None
    tile = lambda w: pl.BlockSpec((None, TM, w), lambda i, t: (i, t, 0))
    full = lambda shape: pl.BlockSpec(shape, lambda i, t: (0,) * len(shape), pipeline_mode=pl.Buffered(1))
    if split:
        x_specs = [pl.BlockSpec((None, TM, D), lambda i, t: (i, jnp.minimum(t, ct - 1), 0)),
                   pl.BlockSpec((None, TM, D), lambda i, t: (i, 0, 0))]
        x_args = [xs, ctx]
    else:
        x_specs = [tile(D)]
        x_args = [xs]
    return pl.pallas_call(
        functools.partial(_merge_kernel, ct=ct, split=split),
        out_shape=[jax.ShapeDtypeStruct((b, n_tiles_eff * TM, D), f32),
                   jax.ShapeDtypeStruct((b, n_tiles_eff * TM * SUBLANES, LANES), f32),
                   jax.ShapeDtypeStruct((b, N_EXPERTS, n_tiles_eff * TM), f32)],
        grid=(b, n_tiles_eff),
        in_specs=x_specs + [tile(ZA_W),
                  pl.BlockSpec((None, BF16_ROWS, ZA_W), lambda i, t: (i, jnp.maximum(t * per - 1, 0), 0)),
                  pl.BlockSpec((None, BF16_ROWS, ZA_W),
                               lambda i, t: (i, jnp.minimum((t + 1) * per, nt // BF16_ROWS - 1), 0)),
                  tile(Q_W), tile(RNN_W), tile(RNN_W), tile(RNN_W), tile(G_G),
                  pl.BlockSpec((None, None, 1, N_MOD * D), lambda i, t: (i, t // ct, 0, 0)),
                  full((3, CONV_W)), full((CONV_W, D)), full((Q_W, D)), full((RNN_W, D)), full((D, D)),
                  full((1, D)), full((N_EXPERTS, D)), full((2 * TM, TM))],
        out_specs=[tile(D), pl.BlockSpec((None, TM * SUBLANES, LANES), lambda i, t: (i, t, 0)),
                   pl.BlockSpec((None, N_EXPERTS, TM), lambda i, t: (i, 0, t))],
        compiler_params=_cparams("arbitrary", "arbitrary"), name="merge",
    )(*x_args, za, za, za, att, hf, hb, zry, zg, modsel, conv_a, wc, wa, wr, wo, norm2, wrt, _row_shift_operator())


ROUTE_QW = 64
ROUTE_TOK_SHIFT = 6
CUM_BLK = 256


def _cumsum_lanes(x, n):
    blk = min(CUM_BLK, n)
    tri = (lax.broadcasted_iota(i32, (blk, blk), 0) <= lax.broadcasted_iota(i32, (blk, blk), 1)).astype(bf16)
    carry = jnp.zeros((x.shape[0], 1), f32)
    outs = []
    for j in range(n // blk):
        c = jnp.dot(x[:, j * blk:(j + 1) * blk].astype(bf16), tri, preferred_element_type=f32) + carry
        outs.append(c)
        carry = c[:, blk - 1:blk]
    return jnp.concatenate(outs, axis=1)


def _route_kernel(aff_ref, idx_ref, val_ref, pos_s, *, n, cap):
    aff = aff_ref[...]
    n_rows = aff.shape[0]
    thr = jnp.zeros((n_rows, 1), i32)
    for bit in range(30, -1, -1):
        cand = thr | (1 << bit)
        cnt = jnp.sum(jnp.where(aff >= lax.bitcast_convert_type(cand, f32), 1.0, 0.0), axis=1, keepdims=True)
        thr = jnp.where(cnt >= float(cap), cand, thr)
    thr_f = lax.bitcast_convert_type(thr, f32)
    gt = aff > thr_f
    eq = aff == thr_f
    need = float(cap) - jnp.sum(jnp.where(gt, 1.0, 0.0), axis=1, keepdims=True)
    sel = gt | (eq & (_cumsum_lanes(eq.astype(f32), n) <= need))
    pos_s[...] = jnp.where(sel, _cumsum_lanes(sel.astype(f32), n).astype(i32) - 1, -1)

    qw = min(ROUTE_QW, cap)
    shift = qw.bit_length() - 1
    tok = lax.broadcasted_iota(i32, (1, n), 1)
    tok_hi = (tok >> ROUTE_TOK_SHIFT).astype(f32)
    tok_lo = (tok & ((1 << ROUTE_TOK_SHIFT) - 1)).astype(f32)
    hi_iota = lax.broadcasted_iota(i32, (SUBLANES, n), 0)
    lo_iota = lax.broadcasted_iota(i32, (qw, n), 0)

    def body(e, carry):
        pos = pos_s[pl.ds(e, 1), :]
        a0 = aff_ref[pl.ds(e, 1), :]
        a_h = a0.astype(bf16).astype(f32)
        a_m = (a0 - a_h).astype(bf16).astype(f32)
        a_l = ((a0 - a_h) - a_m).astype(bf16).astype(f32)
        in_hi = (pos >> shift) == hi_iota
        lhs = jnp.concatenate([jnp.where(in_hi, r, 0.0) for r in (tok_hi, tok_lo, a_h, a_m, a_l)], axis=0)
        onehot = jnp.where((pos & (qw - 1)) == lo_iota, 1.0, 0.0).astype(bf16)
        res = lax.dot_general(lhs.astype(bf16), onehot, (((1,), (1,)), ((), ())), preferred_element_type=f32)
        r_hi, r_lo, v_h, v_m, v_l = (res[k * SUBLANES:(k + 1) * SUBLANES] for k in range(5))
        idx_ref[e] = (r_hi * float(1 << ROUTE_TOK_SHIFT) + r_lo).astype(i32)[:cap // qw]
        val_ref[e] = ((v_h + v_m) + v_l)[:cap // qw]
        return carry

    lax.fori_loop(0, n_rows, body, 0)


def _route(aff_t, n, cap, lane_block):
    b = aff_t.shape[0]
    qw = min(ROUTE_QW, cap)
    assert cap % qw == 0 and cap // qw <= SUBLANES and qw & (qw - 1) == 0
    rows = b * N_EXPERTS
    idx, val = pl.pallas_call(
        functools.partial(_route_kernel, n=n, cap=cap),
        out_shape=[jax.ShapeDtypeStruct((rows, cap // qw, qw), i32),
                   jax.ShapeDtypeStruct((rows, cap // qw, qw), f32)],
        grid=(1,),
        in_specs=[pl.BlockSpec((rows, n), lambda i: (0, lane_block))],
        out_specs=[pl.BlockSpec((rows, cap // qw, qw), lambda i: (0, 0, 0)),
                   pl.BlockSpec((rows, cap // qw, qw), lambda i: (0, 0, 0))],
        scratch_shapes=[pltpu.VMEM((rows, n), i32)],
        compiler_params=_cparams("arbitrary"), name="route",
    )(aff_t.reshape(rows, aff_t.shape[2]))
    return idx.reshape(b, N_EXPERTS, cap), val.reshape(b, N_EXPERTS, cap)


MOE_PAIR = 2


def _gather_kernel(idx_ref, h_ref, tok_ref, tbuf, *, cap):
    sp = cap + SUBLANES
    for k in range(MOE_PAIR):
        base = (pl.program_id(0) * N_EXPERTS + pl.program_id(1) * MOE_PAIR + k) * cap
        for p in range(cap):
            i = idx_ref[base + p]
            tbuf[pl.ds(p, SUBLANES, stride=sp), :] = h_ref[pl.ds(pl.multiple_of(i * SUBLANES, SUBLANES), SUBLANES), :]
        tok_ref[k] = jnp.concatenate([tbuf[pl.ds(c * sp, cap), :] for c in range(D // LANES)], axis=1).astype(bf16)


def _gather(idx, h2t, cap):
    b, rows, _ = h2t.shape
    return pl.pallas_call(
        functools.partial(_gather_kernel, cap=cap),
        out_shape=jax.ShapeDtypeStruct((b, N_EXPERTS, cap, D), bf16),
        grid_spec=pltpu.PrefetchScalarGridSpec(
            num_scalar_prefetch=1, grid=(b, N_EXPERTS // MOE_PAIR),
            in_specs=[pl.BlockSpec((None, rows, LANES), lambda i, e, idx: (i, 0, 0))],
            out_specs=pl.BlockSpec((None, MOE_PAIR, cap, D), lambda i, e, idx: (i, e, 0, 0)),
            scratch_shapes=[pltpu.VMEM((SUBLANES * (cap + SUBLANES), LANES), f32)]),
        compiler_params=_cparams("arbitrary", "arbitrary"), name="moe_gather",
    )(idx, h2t)


def _cast3_kernel(a_ref, b_ref, c_ref, ao_ref, bo_ref, co_ref):
    ao_ref[...] = a_ref[...].astype(bf16)
    bo_ref[...] = b_ref[...].astype(bf16)
    co_ref[...] = c_ref[...].astype(bf16)


def _expert0_bf16(wg, wu, wd, layer):
    spec = pl.BlockSpec((None, None, D, EXPERT_FF), lambda i: (layer, 0, 0, 0))
    ospec = pl.BlockSpec((D, EXPERT_FF), lambda i: (0, 0))
    return pl.pallas_call(
        _cast3_kernel, out_shape=[jax.ShapeDtypeStruct((D, EXPERT_FF), bf16)] * 3, grid=(1,),
        in_specs=[spec] * 3, out_specs=[ospec] * 3, compiler_params=_cparams("arbitrary"), name="expert0_cast",
    )(wg, wu, wd)


def _ffn_kernel(tok_ref, g0_ref, u0_ref, d0_ref, gq_ref, uq_ref, dq_ref, o_ref, w_even, w_odd):
    e = pl.program_id(0)
    i = pl.program_id(1)

    @pl.when((e == 0) & (i == 0))
    def _():
        w_even[0] = g0_ref[...]
        w_even[1] = u0_ref[...]
        w_even[2] = d0_ref[...]

    rows = gq_ref.shape[0]
    r0 = pl.multiple_of(i * rows, rows)

    def step(use, fill):
        n_s, cap, _ = tok_ref.shape
        t = tok_ref[...].reshape(n_s * cap, D)
        g = jnp.dot(t, use[0], preferred_element_type=f32)
        for k, q_ref in enumerate((gq_ref, uq_ref, dq_ref)):
            fill[k, pl.ds(r0, rows), :] = q_ref[...].astype(bf16)
        u = jnp.dot(t, use[1], preferred_element_type=f32)
        a = (g * _sigmoid(g) * u).astype(bf16)
        o_ref[...] = jnp.dot(a, use[2], preferred_element_type=f32).reshape(n_s, cap, D)

    @pl.when(e % 2 == 0)
    def _():
        step(w_even, w_odd)

    @pl.when(e % 2 == 1)
    def _():
        step(w_odd, w_even)


def _ffn(tok, wg, wu, wd, layer):
    b, _, cap, _ = tok.shape
    n_s = 1
    steps = b // n_s
    assert D == EXPERT_FF and D % steps == 0 and (D // steps) % BF16_ROWS == 0
    rows = D // steps
    w0 = _expert0_bf16(wg, wu, wd, layer)
    w0spec = pl.BlockSpec((D, EXPERT_FF), lambda e, i: (0, 0), pipeline_mode=pl.Buffered(1))
    qspec = pl.BlockSpec((None, None, rows, D), lambda e, i: (layer, jnp.minimum(e + 1, N_EXPERTS - 1), i, 0))
    return pl.pallas_call(
        _ffn_kernel,
        out_shape=jax.ShapeDtypeStruct((b, N_EXPERTS, cap, D), f32),
        grid=(N_EXPERTS, steps),
        in_specs=[pl.BlockSpec((n_s, None, cap, D), lambda e, i: (i, e, 0, 0)),
                  w0spec, w0spec, w0spec, qspec, qspec, qspec],
        out_specs=pl.BlockSpec((n_s, None, cap, D), lambda e, i: (i, e, 0, 0)),
        scratch_shapes=[pltpu.VMEM((3, D, EXPERT_FF), bf16), pltpu.VMEM((3, D, EXPERT_FF), bf16)],
        compiler_params=_cparams("arbitrary", "arbitrary"), name="moe_ffn",
    )(tok, *w0, wg, wu, wd)


SCATTER_UNROLL = 8


def _scatter_kernel(idx_ref, val_ref, o_ref, acc_ref, tbuf, *, cap):
    e = pl.program_id(1)
    sp = cap + SUBLANES

    @pl.when(e == 0)
    def _():
        acc_ref[...] = jnp.zeros_like(acc_ref)

    for k in range(MOE_PAIR):
        base = (pl.program_id(0) * N_EXPERTS + e * MOE_PAIR + k) * cap
        for c in range(D // LANES):
            tbuf[pl.ds(c * sp, cap), :] = o_ref[k, :, c * LANES:(c + 1) * LANES]
        u = min(SCATTER_UNROLL, cap)
        for p0 in range(0, cap, u):
            pend = []
            for p in range(p0, p0 + u):
                r0 = pl.multiple_of(idx_ref[base + p] * SUBLANES, SUBLANES)
                pend.append((r0, acc_ref[pl.ds(r0, SUBLANES), :]
                             + tbuf[pl.ds(p, SUBLANES, stride=sp), :] * val_ref[base + p]))
            for r0, v in pend:
                acc_ref[pl.ds(r0, SUBLANES), :] = v


def _scatter(idx, val, o, rows):
    b, _, cap, _ = o.shape
    return pl.pallas_call(
        functools.partial(_scatter_kernel, cap=cap),
        out_shape=jax.ShapeDtypeStruct((b, rows, LANES), f32),
        grid_spec=pltpu.PrefetchScalarGridSpec(
            num_scalar_prefetch=2, grid=(b, N_EXPERTS // MOE_PAIR),
            in_specs=[pl.BlockSpec((None, MOE_PAIR, cap, D), lambda i, e, idx, val: (i, e, 0, 0))],
            out_specs=pl.BlockSpec((None, rows, LANES), lambda i, e, idx, val: (i, 0, 0)),
            scratch_shapes=[pltpu.VMEM((SUBLANES * (cap + SUBLANES), LANES), f32)]),
        compiler_params=_cparams("arbitrary", "arbitrary"), name="moe_scatter",
    )(idx, val, o)


def _moe(aff_t, h2t, wg, wu, wd, layer, n_lat, n_ctx):
    idx, val = _route(aff_t, n_lat, CAP_FACTOR * n_lat // N_EXPERTS, 0)
    if n_ctx:
        idx_c, val_c = _route(aff_t, n_ctx, CAP_FACTOR * n_ctx // N_EXPERTS, n_lat // n_ctx)
        idx = jnp.concatenate([idx, idx_c + n_lat], axis=-1)
        val = jnp.concatenate([val, val_c], axis=-1)
    cap = idx.shape[-1]
    idx = idx.reshape(-1)
    val = val.reshape(-1)
    tok = _gather(idx, h2t, cap)
    o = _ffn(tok, wg, wu, wd, layer)
    return _scatter(idx, val, o, h2t.shape[1])


def _final_kernel(x_ref, acc_ref, mod_ref, n_ref, o_ref):
    x = x_ref[...] + mod_ref[:, 5 * D:6 * D] * _tiles_to_rows(acc_ref, x_ref.shape[0])
    o_ref[...] = (x * lax.rsqrt(jnp.mean(x * x, axis=-1, keepdims=True) + EPS)) * n_ref[...]


def _final(xs, acc, modsel, norm, n_lat):
    b = xs.shape[0]
    tm = 2 * TM if n_lat % (2 * TM) == 0 else TM
    return pl.pallas_call(
        _final_kernel,
        out_shape=jax.ShapeDtypeStruct((b, n_lat, D), f32),
        grid=(b, n_lat // tm),
        in_specs=[pl.BlockSpec((None, tm, D), lambda i, t: (i, t, 0)),
                  pl.BlockSpec((None, tm * SUBLANES, LANES), lambda i, t: (i, t, 0)),
                  pl.BlockSpec((None, None, 1, N_MOD * D), lambda i, t: (i, 0, 0, 0)),
                  pl.BlockSpec((1, D), lambda i, t: (0, 0))],
        out_specs=pl.BlockSpec((None, tm, D), lambda i, t: (i, t, 0)),
        compiler_params=_cparams("arbitrary", "arbitrary"), name="final_norm",
    )(xs, acc, modsel, norm)


def _rope_tables(n_lat, n_ctx):
    pos = jnp.arange(n_lat, dtype=f32)
    rows = jnp.floor(pos / GRID_W)
    cols = pos - rows * GRID_W
    half = HEAD_DIM // 4
    inv = 1.0 / (ROPE_BASE ** (jnp.arange(0, 2 * half, 2, dtype=f32) / (2 * half)))
    lane = jnp.arange(LANES)
    d = lane % HEAD_DIM
    ang = jnp.where((d < 2 * half)[None, :], rows[:, None], cols[:, None]) * inv[d % half][None, :]
    first = ((d % (2 * half)) < half)[None, :]
    cos = jnp.cos(ang)
    sin = jnp.sin(ang)
    s1 = jnp.where(first, -sin, 0.0)
    s2 = jnp.where(first, 0.0, sin)
    pad = lambda t, v: jnp.concatenate([t, jnp.full((n_ctx, LANES), v, f32)], axis=0)
    return pad(cos, 1.0), pad(s1, 0.0), pad(s2, 0.0)


def _in_weights_bf16(w):
    scale_b = jnp.concatenate([jnp.ones((G_RY,), f32), jnp.full((G_G,), 0.5, f32)])
    return w[:, :W_A_COLS].astype(bf16), (w[:, W_A_COLS:] * scale_b[None, :]).astype(bf16)


def kernel(x, c, ctx, c_ctx, norm1, norm2, w_mod, b_mod, w_in, conv_a, w_conv_out, attn_sink, w_attn_out, rnn_conv_w,
           rnn_conv_b, rnn_w_a, rnn_b_a, rnn_w_x, rnn_b_x, rnn_lam, w_rnn_out, w_o, w_router, w_e_gate, w_e_up,
           w_e_down, final_norm):
    b, n_lat, d = x.shape
    n_ctx = ctx.shape[1]
    depth = w_mod.shape[0]
    assert d == D and n_ctx == TM and n_lat % TM == 0 and b < SUBLANES
    nt = n_lat + n_ctx
    ct = nt // TM - 1

    cstack = jnp.concatenate([c, c_ctx[None, :], jnp.zeros((SUBLANES - b - 1, D), f32)], axis=0)
    mod = _modulation(cstack, w_mod, b_mod)
    tabs = _rope_tables(n_lat, n_ctx)

    xs = x
    acc = None
    modsel_prev = None
    for l in range(depth):
        ctx_out = l < depth - 1
        ctx_sep = ctx if l == 0 else None
        modsel = jnp.stack([mod[l, :b], jnp.broadcast_to(mod[l, b], (b, N_MOD * D))], axis=1)[:, :, None, :]
        rnn_prm = []
        for dr in range(2):
            wg = (0.5 * jnp.concatenate([rnn_w_a[l, dr], rnn_w_x[l, dr]], axis=-1)).astype(bf16)
            rnn_prm.append((rnn_conv_w[l, dr], rnn_conv_b[l, dr][None, :], wg, rnn_b_a[l, dr][None, :],
                            rnn_b_x[l, dr][None, :], rnn_lam[l, dr][None, :]))
        w_a, w_b = _in_weights_bf16(w_in[l])
        outs = _proj_a(xs, ctx_sep, acc, modsel_prev, modsel, norm1[l][None, :], tabs, w_a, rnn_prm[0])
        if acc is not None:
            xs, outs = outs[0], outs[1:]
        za, zq, zkv, zrx, hn, hf = outs
        zry, zg, hb = _proj_b(hn, w_b, zrx, rnn_prm[1])
        att = _attention(attn_sink[l][None, :], zq, zkv, n_lat, (nt if ctx_out else n_lat) // QB)
        xs, h2t, aff_t = _merge(xs, ctx_sep, za, att, hf, hb, zry, zg, modsel, conv_a[l],
                                w_conv_out[l].astype(bf16),
                                w_attn_out[l].astype(bf16), (0.5 * w_rnn_out[l]).astype(bf16),
                                (0.5 * w_o[l]).astype(bf16),
                                norm2[l][None, :], w_router[l].T.astype(bf16), (ct + 1) if ctx_out else ct)
        acc = _moe(aff_t, h2t, w_e_gate, w_e_up, w_e_down, l, n_lat, n_ctx if ctx_out else 0)
        modsel_prev = modsel
    return _final(xs, acc, modsel_prev, final_norm[None, :], n_lat)
```

```python
import functools

import jax
import jax.numpy as jnp
from jax import lax
from jax.experimental import pallas as pl
from jax.experimental.pallas import tpu as pltpu

f32 = jnp.float32
bf16 = jnp.bfloat16
i32 = jnp.int32

D = 1024
EPS = 1e-6
CONV_W = 512
N_HEADS = 8
N_KV = 2
HEAD_DIM = 64
Q_W = N_HEADS * HEAD_DIM
KV_W = N_KV * HEAD_DIM
WINDOW = 128
GRID_W = 64
ROPE_BASE = 10000.0
NEG_INF = -1e30
RNN_W = 1024
RNN_BLOCKS = 8
RNN_BLK = RNN_W // RNN_BLOCKS
RNN_CONV_K = 4
RG_C = 8.0
N_EXPERTS = 16
EXPERT_FF = 1024
CAP_FACTOR = 2
N_MOD = 6

LANES = 128
SUBLANES = 8
BF16_ROWS = 16
VMEM_LIMIT = 56 * 1024 * 1024

TM = 256
QB = 128
K2_W = 2 * KV_W

G_A = 3 * CONV_W
G_QKV = Q_W + 2 * KV_W
G_RX = RNN_W
G_RY = RNN_W
G_G = 3 * D
IN_COLS = (G_A, G_QKV, G_RX, G_RY, G_G)
IN_TOTAL = sum(IN_COLS)
KV2_W = 2 * K2_W
ZA_W = 2 * CONV_W
OUT_COLS = (ZA_W, Q_W, KV2_W, G_RX, G_RY, G_G)


def _cparams(*sem):
    return pltpu.CompilerParams(dimension_semantics=sem, vmem_limit_bytes=VMEM_LIMIT)


def _sigmoid(x):
    return 0.5 * jnp.tanh(0.5 * x) + 0.5


def _rms_mod(x, g, sc, sh):
    y = x * lax.rsqrt(jnp.mean(x * x, axis=-1, keepdims=True) + EPS)
    return (y * g) * (1.0 + sc) + sh


def _mod_kernel(c_ref, w_ref, b_ref, o_ref):
    c = c_ref[...]
    s = (c * _sigmoid(c)).astype(bf16)
    o_ref[...] = jnp.dot(s, w_ref[...].astype(bf16), preferred_element_type=f32) + b_ref[...]


def _modulation(cstack, w_mod, b_mod):
    depth = w_mod.shape[0]
    tn = 1536
    return pl.pallas_call(
        _mod_kernel,
        out_shape=jax.ShapeDtypeStruct((depth, SUBLANES, N_MOD * D), f32),
        grid=(depth, N_MOD * D // tn),
        in_specs=[
            pl.BlockSpec((SUBLANES, D), lambda l, j: (0, 0)),
            pl.BlockSpec((None, D, tn), lambda l, j: (l, 0, j)),
            pl.BlockSpec((None, 1, tn), lambda l, j: (l, 0, j)),
        ],
        out_specs=pl.BlockSpec((None, SUBLANES, tn), lambda l, j: (l, 0, j)),
        compiler_params=_cparams("arbitrary", "arbitrary"),
        name="modulation",
    )(cstack, w_mod, b_mod.reshape(depth, 1, N_MOD * D))


def _tiles_to_rows(ref, rows):
    return jnp.concatenate([ref[pl.ds(c, rows, stride=SUBLANES), :] for c in range(D // LANES)], axis=1)


def _rows_to_tiles(ref, val, rows):
    per = D // LANES
    for r in range(rows // SUBLANES):
        for c in range(per):
            ref[pl.ds(r * SUBLANES * per + c, SUBLANES, stride=per), :] = val[r * SUBLANES:(r + 1) * SUBLANES,
                                                                              c * LANES:(c + 1) * LANES]


def _rope(z, cos, s1, s2):
    outs = []
    for c in range(z.shape[1] // LANES):
        x = z[:, c * LANES:(c + 1) * LANES]
        outs.append(x * cos + pltpu.roll(x, LANES - 16, axis=1) * s1 + pltpu.roll(x, 16, axis=1) * s2)
    return jnp.concatenate(outs, axis=1)


W_A_COLS = G_A + G_QKV + G_RX


def _proj_a_kernel(*refs, with_moe, ct):
    if with_moe:
        (x_ref, acc_ref, modp_ref), refs = refs[:3], refs[3:]
    else:
        (x_ref, ctx_ref), refs = refs[:2], refs[2:]
    mod_ref, n_ref, cos_ref, s1_ref, s2_ref, w_ref = refs[:6]
    prm, refs = refs[6:12], refs[12:]
    if with_moe:
        xo_ref, refs = refs[0], refs[1:]
    za_ref, zq_ref, zkv_ref, zrx_ref, h_ref, hf_ref, ubuf, obuf, hcar, halo_s = refs
    s = pl.program_id(1)
    chunk = _rnn_chunk(s, ct, False)

    @pl.when(s == 0)
    def _():
        hcar[...] = jnp.zeros_like(hcar)
        halo_s[...] = jnp.zeros_like(halo_s)

    if with_moe:
        x = x_ref[...] + modp_ref[:, 5 * D:6 * D] * _tiles_to_rows(acc_ref, TM)
        xo_ref[...] = x
    else:
        x = jnp.where(chunk == ct, ctx_ref[...], x_ref[...])
    h = _rms_mod(x, n_ref[...], mod_ref[:, D:2 * D], mod_ref[:, 0:D]).astype(bf16)
    h_ref[...] = h
    hal = jnp.where((chunk != 0) & (chunk != ct), halo_s[...], 0.0)
    quarter = G_RX // 4

    def scan_input(p):
        cols = slice(p * quarter, (p + 1) * quarter)
        u = jnp.dot(h, w_ref[:, G_A + G_QKV + p * quarter:G_A + G_QKV + (p + 1) * quarter],
                    preferred_element_type=f32)
        zrx_ref[:, cols] = u.astype(bf16)
        _rnn_fill(u, hal[:, cols], ubuf, False, c0=p * quarter // LANES)
        halo_s[:, cols] = u[TM - BF16_ROWS:, :]

    scan = lambda c: _rnn_tile(c, prm, ubuf, obuf, hcar, False)
    cw = CONV_W
    scan_input(0)
    a_x = jnp.dot(h, w_ref[:, 0:cw], preferred_element_type=f32)
    scan(0)
    scan_input(1)
    scan(1)
    a_b = jnp.dot(h, w_ref[:, cw:2 * cw], preferred_element_type=f32)
    scan(2)
    scan_input(2)
    scan(3)
    a_c = jnp.dot(h, w_ref[:, 2 * cw:G_A], preferred_element_type=f32)
    za_ref[...] = jnp.concatenate([a_c * a_x, a_b], axis=1).astype(bf16)
    scan(4)
    scan_input(3)
    scan(5)
    z = jnp.dot(h, w_ref[:, G_A:G_A + G_QKV], preferred_element_type=f32)
    qk = _rope(z[:, :Q_W + KV_W], cos_ref[...], s1_ref[...], s2_ref[...])
    zq_ref[...] = (qk[:, :Q_W] * (HEAD_DIM ** -0.5)).astype(bf16)
    lo = lax.broadcasted_iota(i32, (TM, LANES), 1) < HEAD_DIM
    dup = []
    for pair in (qk[:, Q_W:], z[:, Q_W + KV_W:]):
        swapped = pltpu.roll(pair, HEAD_DIM, axis=1)
        dup += [jnp.where(lo, pair, swapped), jnp.where(lo, swapped, pair)]
    zkv_ref[...] = jnp.concatenate(dup, axis=1).astype(bf16)
    scan(6)
    scan(7)
    hf_ref[...] = _rnn_states(obuf).astype(bf16)


def _proj_b_kernel(h_ref, w_ref, u_ref, halo_ref, *refs, ct):
    prm, (zry_ref, zg_ref, hb_ref, ubuf, obuf, hcar) = refs[:6], refs[6:]
    s = pl.program_id(1)
    chunk = _rnn_chunk(s, ct, True)

    @pl.when(s == 0)
    def _():
        hcar[...] = jnp.zeros_like(hcar)

    h = h_ref[...]
    hal = jnp.where((chunk != ct - 1) & (chunk != ct), halo_ref[...].astype(f32), 0.0)
    _rnn_fill(u_ref[...].astype(f32), hal, ubuf, True)
    slab = (G_RY + G_G) // RNN_LT
    for c in range(RNN_LT):
        z = jnp.dot(h, w_ref[:, c * slab:(c + 1) * slab], preferred_element_type=f32)
        if (c + 1) * slab <= G_RY:
            t_y = jnp.tanh(z * (0.7978845608028654 + (0.7978845608028654 * 0.044715) * (z * z)))
            zry_ref[:, c * slab:(c + 1) * slab] = (z * t_y + z).astype(bf16)
        else:
            zg_ref[:, c * slab - G_RY:(c + 1) * slab - G_RY] = z.astype(bf16)
        _rnn_tile(c, prm, ubuf, obuf, hcar, True)
    hb_ref[...] = _rnn_states(obuf).astype(bf16)


def _proj_a(xs, ctx, acc, modsel_prev, modsel, norm, tabs, w_a, rnn_prm):
    with_moe = acc is not None
    b = xs.shape[0]
    nt = xs.shape[1] + (0 if with_moe else ctx.shape[1])
    n_tiles = nt // TM
    ct = n_tiles - 1
    ch = lambda s: _rnn_chunk(s, ct, False)
    tile = lambda w: pl.BlockSpec((None, TM, w), lambda i, s: (i, ch(s), 0))
    mod_spec = pl.BlockSpec((None, None, 1, N_MOD * D), lambda i, s: (i, ch(s) // ct, 0, 0))
    tab_spec = pl.BlockSpec((TM, LANES), lambda i, s: (ch(s), 0))
    if with_moe:
        in_specs = [tile(D), pl.BlockSpec((None, TM * SUBLANES, LANES), lambda i, s: (i, ch(s), 0)), mod_spec]
        args = [xs, acc, modsel_prev]
    else:
        in_specs = [pl.BlockSpec((None, TM, D), lambda i, s: (i, jnp.minimum(ch(s), ct - 1), 0)),
                    pl.BlockSpec((None, TM, D), lambda i, s: (i, 0, 0))]
        args = [xs, ctx]
    in_specs += [mod_spec, pl.BlockSpec((1, D), lambda i, s: (0, 0)), tab_spec, tab_spec, tab_spec,
                 pl.BlockSpec((D, W_A_COLS), lambda i, s: (0, 0), pipeline_mode=pl.Buffered(1))]
    in_specs += _rnn_param_specs()
    args += [modsel, norm, *tabs, w_a, *rnn_prm]
    widths = (ZA_W, Q_W, KV2_W, G_RX, D, RNN_W)
    out_shape = [jax.ShapeDtypeStruct((b, nt, w), bf16) for w in widths]
    out_specs = [tile(w) for w in widths]
    if with_moe:
        out_shape = [jax.ShapeDtypeStruct((b, nt, D), f32)] + out_shape
        out_specs = [tile(D)] + out_specs
    return pl.pallas_call(
        functools.partial(_proj_a_kernel, with_moe=with_moe, ct=ct),
        out_shape=out_shape, grid=(b, n_tiles), in_specs=in_specs, out_specs=out_specs,
        scratch_shapes=_RNN_SCRATCH + [pltpu.VMEM((BF16_ROWS, RNN_W), f32)],
        compiler_params=_cparams("arbitrary", "arbitrary"), name="proj_a_rnn_fwd",
    )(*args)


def _proj_b(h, w_b, zrx, rnn_prm):
    b, nt, _ = h.shape
    n_tiles = nt // TM
    ct = n_tiles - 1
    per = TM // BF16_ROWS
    ch = lambda s: _rnn_chunk(s, ct, True)
    tile = lambda w: pl.BlockSpec((None, TM, w), lambda i, s: (i, ch(s), 0))
    halo = pl.BlockSpec((None, BF16_ROWS, RNN_W),
                        lambda i, s: (i, jnp.minimum((ch(s) + 1) * per, nt // BF16_ROWS - 1), 0))
    widths = (G_RY, G_G, RNN_W)
    return pl.pallas_call(
        functools.partial(_proj_b_kernel, ct=ct),
        out_shape=[jax.ShapeDtypeStruct((b, nt, w), bf16) for w in widths],
        grid=(b, n_tiles),
        in_specs=[tile(D), pl.BlockSpec((D, G_RY + G_G), lambda i, s: (0, 0), pipeline_mode=pl.Buffered(1)),
                  tile(RNN_W), halo] + _rnn_param_specs(),
        out_specs=[tile(w) for w in widths],
        scratch_shapes=_RNN_SCRATCH,
        compiler_params=_cparams("arbitrary", "arbitrary"), name="proj_b_rnn_bwd",
    )(h, w_b, zrx, zrx, *rnn_prm)


ATT_QBLOCKS = 2


def _attn_kernel(sink_ref, q_ref, kvp_ref, kvc_ref, kvn_ref, kvx_ref, o_ref, *, n_lat_blocks, n_ctx):
    group = N_HEADS // N_KV
    nlb = n_lat_blocks
    lo = lax.broadcasted_iota(i32, (QB, LANES), 1) < HEAD_DIM
    diff = lax.broadcasted_iota(i32, (QB, QB), 1) - lax.broadcasted_iota(i32, (QB, QB), 0)
    zero = jnp.zeros((QB, LANES), bf16)
    def score(sub, g):
        window = (kvp_ref, kvc_ref) if sub == 0 else (kvc_ref, kvn_ref)
        ksl = slice(g * LANES, (g + 1) * LANES)
        k = jnp.concatenate([r[:, ksl] for r in window] + [kvx_ref[:, ksl]], axis=0)
        parts = []
        for pr in range(group // 2):
            qp = q_ref[sub * QB:(sub + 1) * QB, (g * (group // 2) + pr) * LANES:(g * (group // 2) + pr + 1) * LANES]
            parts += [jnp.where(lo, qp, zero), jnp.where(lo, zero, qp)]
        return lax.dot_general(jnp.concatenate(parts, axis=0), k, (((1,), (1,)), ((), ())),
                               preferred_element_type=f32)

    def finish(sub, g, s):
        j = pl.program_id(1) * ATT_QBLOCKS + sub
        lat = j < nlb
        ok = (diff >= jnp.where(lat & (j >= 1), 0, QB),
              diff >= jnp.where(lat, -QB, QB),
              -diff >= jnp.where(lat & (j <= nlb - 2), 0, QB))
        rows = slice(sub * QB, (sub + 1) * QB)
        window = (kvp_ref, kvc_ref) if sub == 0 else (kvc_ref, kvn_ref)
        vsl = slice(K2_W + g * LANES, K2_W + (g + 1) * LANES)
        v = jnp.concatenate([r[:, vsl] for r in window] + [kvx_ref[:, vsl]], axis=0)
        ps, inv = [], []
        for hh in range(group):
            sh = s[hh * QB:(hh + 1) * QB]
            sm = jnp.concatenate([jnp.where(ok[kb], sh[:, kb * QB:(kb + 1) * QB], NEG_INF) for kb in range(3)]
                                 + [sh[:, 3 * QB:]], axis=1)
            sink = sink_ref[0, g * group + hh]
            m = jnp.maximum(jnp.max(sm, axis=1, keepdims=True), sink)
            p = jnp.exp(sm - m)
            inv.append(1.0 / (jnp.sum(p, axis=1, keepdims=True) + jnp.exp(sink - m)))
            ps.append(p.astype(bf16))
        o = jnp.dot(jnp.concatenate(ps, axis=0), v, preferred_element_type=f32) * jnp.concatenate(inv, axis=0)
        for pr in range(group // 2):
            pair = jnp.where(lo, o[(2 * pr) * QB:(2 * pr + 1) * QB], o[(2 * pr + 1) * QB:(2 * pr + 2) * QB])
            c0 = (g * (group // 2) + pr) * LANES
            o_ref[rows, c0:c0 + LANES] = pair.astype(bf16)

    units = [(sub, g) for sub in range(ATT_QBLOCKS) for g in range(N_KV)]
    pending = score(*units[0])
    for n, unit in enumerate(units):
        ahead = score(*units[n + 1]) if n + 1 < len(units) else None
        finish(*unit, pending)
        pending = ahead


def _attention(sink, zq, zkv, n_lat, n_qblocks):
    b, nt, _ = zq.shape
    n_ctx = nt - n_lat
    nlb = n_lat // QB
    cidx = n_lat // n_ctx
    nq = ATT_QBLOCKS
    assert nq == 2 and n_qblocks % nq == 0 and nlb % nq == 0
    edge = lambda d: pl.BlockSpec((None, QB, KV2_W), lambda i, j: (i, jnp.clip(nq * j + d, 0, nlb - 1), 0))
    ctx = pl.BlockSpec((None, n_ctx, KV2_W), lambda i, j: (i, cidx, 0))
    return pl.pallas_call(
        functools.partial(_attn_kernel, n_lat_blocks=nlb, n_ctx=n_ctx),
        out_shape=jax.ShapeDtypeStruct((b, n_qblocks * QB, Q_W), bf16),
        grid=(b, n_qblocks // nq),
        in_specs=[pl.BlockSpec(memory_space=pltpu.SMEM),
                  pl.BlockSpec((None, nq * QB, Q_W), lambda i, j: (i, j, 0)),
                  edge(-1), pl.BlockSpec((None, nq * QB, KV2_W), lambda i, j: (i, j, 0)), edge(nq), ctx],
        out_specs=pl.BlockSpec((None, nq * QB, Q_W), lambda i, j: (i, j, 0)),
        compiler_params=_cparams("arbitrary", "arbitrary"), name="attention",
    )(sink, zq, zkv, zkv, zkv, zkv)


def _rnn_chunk(s, ct, rev):
    return jnp.where(s == 0, ct, (ct - s) if rev else (s - 1))


def _block_scan(a, b, h_in, rowi, rev):
    n = SUBLANES
    order = range(n - 1, -1, -1) if rev else range(n)
    hs = [None] * n
    ps = [None] * n
    prev = None
    for j in order:
        if prev is None:
            hs[j], ps[j] = b[j], a[j]
        else:
            hs[j], ps[j] = a[j] * hs[prev] + b[j], a[j] * ps[prev]
        prev = j
    he, pe = hs[prev], ps[prev]
    for sh in (1, 2, 4):
        rs = n - sh if rev else sh
        msk = (rowi < n - sh) if rev else (rowi >= sh)
        he, pe = (jnp.where(msk, he + pe * pltpu.roll(he, rs, axis=0), he),
                  jnp.where(msk, pe * pltpu.roll(pe, rs, axis=0), pe))
    e = he + pe * h_in
    carry = jnp.where((rowi == n - 1) if rev else (rowi == 0), h_in, pltpu.roll(e, n - 1 if rev else 1, axis=0))
    out = [hs[j] + ps[j] * carry for j in range(n)]
    last = e[0:1, :] if rev else e[n - 1:n, :]
    return out, jnp.broadcast_to(last, (n, LANES))


RNN_LT = RNN_W // LANES


def _rnn_fill(u, hal, ubuf, rev, c0=0):
    for k in range(u.shape[1] // LANES):
        sl = slice(k * LANES, (k + 1) * LANES)
        if rev:
            ubuf[c0 + k, 0:TM, :] = u[:, sl]
            ubuf[c0 + k, TM:TM + BF16_ROWS, :] = hal[:, sl]
        else:
            ubuf[c0 + k, 0:BF16_ROWS, :] = hal[:, sl]
            ubuf[c0 + k, BF16_ROWS:BF16_ROWS + TM, :] = u[:, sl]


def _rnn_tile(c, prm, ubuf, obuf, hcar, rev):
    cw_ref, cb_ref, wg_ref, ba_ref, bx_ref, lam_ref = prm
    base = 0 if rev else BF16_ROWS - (RNN_CONV_K - 1)
    rowi = lax.broadcasted_iota(i32, (SUBLANES, LANES), 0)
    blk_rows = SUBLANES * SUBLANES
    n_blk = TM // blk_rows
    sl = slice(c * LANES, (c + 1) * LANES)
    taps = [cw_ref[k:k + 1, sl] for k in range(RNN_CONV_K)]
    xt = []
    for blk in range(n_blk):
        for j in range(SUBLANES):
            r0 = base + blk * blk_rows + j
            acc = cb_ref[:, sl] + ubuf[c, pl.ds(r0, SUBLANES, stride=SUBLANES), :] * taps[0]
            for k in range(1, RNN_CONV_K):
                acc = acc + ubuf[c, pl.ds(r0 + k, SUBLANES, stride=SUBLANES), :] * taps[k]
            xt.append(acc)
    xp = jnp.concatenate(xt, axis=0)
    g = jnp.dot(xp.astype(bf16), wg_ref[c], preferred_element_type=f32)
    t_r = jnp.tanh(g[:, :RNN_BLK] + 0.5 * ba_ref[:, sl])
    t_i = jnp.tanh(g[:, RNN_BLK:] + 0.5 * bx_ref[:, sl])
    c4 = (-0.5 * RG_C) * jnp.logaddexp(-lam_ref[:, sl], 0.0)
    log_a = c4 * t_r + c4
    a = jnp.exp(log_a)
    y = -jnp.tanh(log_a) * (a * a + 1.0)
    xh = 0.5 * xp
    bb = jnp.where(y > 0.0, y * lax.rsqrt(y), 0.0) * (xh * t_i + xh)
    h_in = hcar[:, sl]
    for blk in (range(n_blk - 1, -1, -1) if rev else range(n_blk)):
        rows = lambda j: slice((blk * SUBLANES + j) * SUBLANES, (blk * SUBLANES + j + 1) * SUBLANES)
        hs, h_in = _block_scan([a[rows(j)] for j in range(SUBLANES)], [bb[rows(j)] for j in range(SUBLANES)],
                               h_in, rowi, rev)
        for j in range(SUBLANES):
            obuf[c, pl.ds(blk * blk_rows + j, SUBLANES, stride=SUBLANES), :] = hs[j]
    hcar[:, sl] = h_in


def _rnn_states(obuf):
    return jnp.concatenate([obuf[c] for c in range(RNN_LT)], axis=1)


_RNN_SCRATCH = [pltpu.VMEM((RNN_LT, TM + BF16_ROWS, LANES), f32),
                pltpu.VMEM((RNN_LT, TM, LANES), f32),
                pltpu.VMEM((SUBLANES, RNN_W), f32)]


def _rnn_param_specs():
    full = lambda shape: pl.BlockSpec(shape, lambda i, s: (0,) * len(shape))
    return [full((RNN_CONV_K, RNN_W)), full((1, RNN_W)), full((RNN_BLOCKS, RNN_BLK, 2 * RNN_BLK)),
            full((1, RNN_W)), full((1, RNN_W)), full((1, RNN_W))]


def _merge_kernel(*refs, ct, split):
    (za_ref, zap_ref, zan_ref, att_ref, hf_ref, hb_ref, ry_ref, zg_ref, mod_ref, ca_ref, wc_ref,
     wa_ref, wr_ref, wo_ref, n2_ref, wrt_ref, shift_ref, xo_ref, h2_ref, aff_ref) = refs[2 if split else 1:]
    t = pl.program_id(1)
    if split:
        x_in = jnp.where(t == ct, refs[1][...], refs[0][...])
    else:
        x_in = refs[0][...]
    cw = CONV_W
    cu_b = za_ref[:, 0:cw]
    sh = jnp.dot(shift_ref[...], cu_b, preferred_element_type=f32)
    prev_ok = (t != 0) & (t != ct)
    next_ok = (t != ct - 1) & (t != ct)
    cu_p = jnp.where(prev_ok, zap_ref[BF16_ROWS - 1:BF16_ROWS, 0:cw].astype(f32), 0.0)
    cu_n = jnp.where(next_ok, zan_ref[0:1, 0:cw].astype(f32), 0.0)
    rowi = lax.broadcasted_iota(i32, (TM, cw), 0)
    cu_prev = jnp.where(rowi == 0, cu_p, sh[:TM])
    cu_next = jnp.where(rowi == TM - 1, cu_n, sh[TM:])
    y = cu_prev * ca_ref[0:1, :] + cu_b.astype(f32) * ca_ref[1:2, :] + cu_next * ca_ref[2:3, :]
    cnv = jnp.dot((za_ref[:, cw:2 * cw].astype(f32) * y).astype(bf16), wc_ref[...], preferred_element_type=f32)
    att = jnp.dot(att_ref[...], wa_ref[...], preferred_element_type=f32)
    rec = hf_ref[...].astype(f32) + hb_ref[...].astype(f32)
    rnn = jnp.dot((ry_ref[...].astype(f32) * rec).astype(bf16), wr_ref[...], preferred_element_type=f32)
    mix2 = None
    for k, br in enumerate((cnv, att, rnn)):
        term = jnp.tanh(zg_ref[:, k * D:(k + 1) * D].astype(f32)) * br + br
        mix2 = term if mix2 is None else mix2 + term
    x = x_in + mod_ref[:, 2 * D:3 * D] * jnp.dot(mix2.astype(bf16), wo_ref[...], preferred_element_type=f32)
    xo_ref[...] = x
    h2 = _rms_mod(x, n2_ref[...], mod_ref[:, 4 * D:5 * D], mod_ref[:, 3 * D:4 * D])
    _rows_to_tiles(h2_ref, h2, TM)
    logits = lax.dot_general(wrt_ref[...], h2.astype(bf16), (((1,), (1,)), ((), ())), preferred_element_type=f32)
    e = jnp.exp(logits - jnp.max(logits, axis=0, keepdims=True))
    aff_ref[...] = e / jnp.sum(e, axis=0, keepdims=True)


def _row_shift_operator():
    r = jnp.arange(2 * TM)[:, None]
    c = jnp.arange(TM)[None, :]
    return (c == jnp.where(r < TM, r - 1, r - TM + 1)).astype(bf16)


def _merge(xs, ctx, za, att, hf, hb, zry, zg, modsel, conv_a, wc, wa, wr, wo, norm2, wrt, n_tiles_eff):
    b, nt, _ = za.shape
    ct = nt // TM - 1
    per = TM // BF16_ROWS
    split = ctx is not None
    tile = lambda w: pl.BlockSpec((None, TM, w), lambda i, t: (i, t, 0))
    full = lambda shape: pl.BlockSpec(shape, lambda i, t: (0,) * len(shape), pipeline_mode=pl.Buffered(1))
    if split:
        x_specs = [pl.BlockSpec((None, TM, D), lambda i, t: (i, jnp.minimum(t, ct - 1), 0)),
                   pl.BlockSpec((None, TM, D), lambda i, t: (i, 0, 0))]
        x_args = [xs, ctx]
    else:
        x_specs = [tile(D)]
        x_args = [xs]
    return pl.pallas_call(
        functools.partial(_merge_kernel, ct=ct, split=split),
        out_shape=[jax.ShapeDtypeStruct((b, n_tiles_eff * TM, D), f32),
                   jax.ShapeDtypeStruct((b, n_tiles_eff * TM * SUBLANES, LANES), f32),
                   jax.ShapeDtypeStruct((b, N_EXPERTS, n_tiles_eff * TM), f32)],
        grid=(b, n_tiles_eff),
        in_specs=x_specs + [tile(ZA_W),
                  pl.BlockSpec((None, BF16_ROWS, ZA_W), lambda i, t: (i, jnp.maximum(t * per - 1, 0), 0)),
                  pl.BlockSpec((None, BF16_ROWS, ZA_W),
                               lambda i, t: (i, jnp.minimum((t + 1) * per, nt // BF16_ROWS - 1), 0)),
                  tile(Q_W), tile(RNN_W), tile(RNN_W), tile(RNN_W), tile(G_G),
                  pl.BlockSpec((None, None, 1, N_MOD * D), lambda i, t: (i, t // ct, 0, 0)),
                  full((3, CONV_W)), full((CONV_W, D)), full((Q_W, D)), full((RNN_W, D)), full((D, D)),
                  full((1, D)), full((N_EXPERTS, D)), full((2 * TM, TM))],
        out_specs=[tile(D), pl.BlockSpec((None, TM * SUBLANES, LANES), lambda i, t: (i, t, 0)),
                   pl.BlockSpec((None, N_EXPERTS, TM), lambda i, t: (i, 0, t))],
        compiler_params=_cparams("arbitrary", "arbitrary"), name="merge",
    )(*x_args, za, za, za, att, hf, hb, zry, zg, modsel, conv_a, wc, wa, wr, wo, norm2, wrt, _row_shift_operator())


ROUTE_QW = 64
ROUTE_TOK_SHIFT = 6
CUM_BLK = 256


def _cumsum_lanes(x, n):
    blk = min(CUM_BLK, n)
    tri = (lax.broadcasted_iota(i32, (blk, blk), 0) <= lax.broadcasted_iota(i32, (blk, blk), 1)).astype(bf16)
    carry = jnp.zeros((x.shape[0], 1), f32)
    outs = []
    for j in range(n // blk):
        c = jnp.dot(x[:, j * blk:(j + 1) * blk].astype(bf16), tri, preferred_element_type=f32) + carry
        outs.append(c)
        carry = c[:, blk - 1:blk]
    return jnp.concatenate(outs, axis=1)


def _route_kernel(aff_ref, idx_ref, val_ref, pos_s, *, n, cap):
    aff = aff_ref[...]
    n_rows = aff.shape[0]
    thr = jnp.zeros((n_rows, 1), i32)
    for bit in range(30, -1, -1):
        cand = thr | (1 << bit)
        cnt = jnp.sum(jnp.where(aff >= lax.bitcast_convert_type(cand, f32), 1.0, 0.0), axis=1, keepdims=True)
        thr = jnp.where(cnt >= float(cap), cand, thr)
    thr_f = lax.bitcast_convert_type(thr, f32)
    gt = aff > thr_f
    eq = aff == thr_f
    need = float(cap) - jnp.sum(jnp.where(gt, 1.0, 0.0), axis=1, keepdims=True)
    sel = gt | (eq & (_cumsum_lanes(eq.astype(f32), n) <= need))
    pos_s[...] = jnp.where(sel, _cumsum_lanes(sel.astype(f32), n).astype(i32) - 1, -1)

    qw = min(ROUTE_QW, cap)
    shift = qw.bit_length() - 1
    tok = lax.broadcasted_iota(i32, (1, n), 1)
    tok_hi = (tok >> ROUTE_TOK_SHIFT).astype(f32)
    tok_lo = (tok & ((1 << ROUTE_TOK_SHIFT) - 1)).astype(f32)
    hi_iota = lax.broadcasted_iota(i32, (SUBLANES, n), 0)
    lo_iota = lax.broadcasted_iota(i32, (qw, n), 0)

    def body(e, carry):
        pos = pos_s[pl.ds(e, 1), :]
        a0 = aff_ref[pl.ds(e, 1), :]
        a_h = a0.astype(bf16).astype(f32)
        a_m = (a0 - a_h).astype(bf16).astype(f32)
        a_l = ((a0 - a_h) - a_m).astype(bf16).astype(f32)
        in_hi = (pos >> shift) == hi_iota
        lhs = jnp.concatenate([jnp.where(in_hi, r, 0.0) for r in (tok_hi, tok_lo, a_h, a_m, a_l)], axis=0)
        onehot = jnp.where((pos & (qw - 1)) == lo_iota, 1.0, 0.0).astype(bf16)
        res = lax.dot_general(lhs.astype(bf16), onehot, (((1,), (1,)), ((), ())), preferred_element_type=f32)
        r_hi, r_lo, v_h, v_m, v_l = (res[k * SUBLANES:(k + 1) * SUBLANES] for k in range(5))
        idx_ref[e] = (r_hi * float(1 << ROUTE_TOK_SHIFT) + r_lo).astype(i32)[:cap // qw]
        val_ref[e] = ((v_h + v_m) + v_l)[:cap // qw]
        return carry

    lax.fori_loop(0, n_rows, body, 0)


def _route(aff_t, n, cap, lane_block):
    b = aff_t.shape[0]
    qw = min(ROUTE_QW, cap)
    assert cap % qw == 0 and cap // qw <= SUBLANES and qw & (qw - 1) == 0
    rows = b * N_EXPERTS
    idx, val = pl.pallas_call(
        functools.partial(_route_kernel, n=n, cap=cap),
        out_shape=[jax.ShapeDtypeStruct((rows, cap // qw, qw), i32),
                   jax.ShapeDtypeStruct((rows, cap // qw, qw), f32)],
        grid=(1,),
        in_specs=[pl.BlockSpec((rows, n), lambda i: (0, lane_block))],
        out_specs=[pl.BlockSpec((rows, cap // qw, qw), lambda i: (0, 0, 0)),
                   pl.BlockSpec((rows, cap // qw, qw), lambda i: (0, 0, 0))],
        scratch_shapes=[pltpu.VMEM((rows, n), i32)],
        compiler_params=_cparams("arbitrary"), name="route",
    )(aff_t.reshape(rows, aff_t.shape[2]))
    return idx.reshape(b, N_EXPERTS, cap), val.reshape(b, N_EXPERTS, cap)


MOE_PAIR = 2


def _gather_kernel(idx_ref, h_ref, tok_ref, tbuf, *, cap):
    sp = cap + SUBLANES
    for k in range(MOE_PAIR):
        base = (pl.program_id(0) * N_EXPERTS + pl.program_id(1) * MOE_PAIR + k) * cap
        for p in range(cap):
            i = idx_ref[base + p]
            tbuf[pl.ds(p, SUBLANES, stride=sp), :] = h_ref[pl.ds(pl.multiple_of(i * SUBLANES, SUBLANES), SUBLANES), :]
        tok_ref[k] = jnp.concatenate([tbuf[pl.ds(c * sp, cap), :] for c in range(D // LANES)], axis=1).astype(bf16)


def _gather(idx, h2t, cap):
    b, rows, _ = h2t.shape
    return pl.pallas_call(
        functools.partial(_gather_kernel, cap=cap),
        out_shape=jax.ShapeDtypeStruct((b, N_EXPERTS, cap, D), bf16),
        grid_spec=pltpu.PrefetchScalarGridSpec(
            num_scalar_prefetch=1, grid=(b, N_EXPERTS // MOE_PAIR),
            in_specs=[pl.BlockSpec((None, rows, LANES), lambda i, e, idx: (i, 0, 0))],
            out_specs=pl.BlockSpec((None, MOE_PAIR, cap, D), lambda i, e, idx: (i, e, 0, 0)),
            scratch_shapes=[pltpu.VMEM((SUBLANES * (cap + SUBLANES), LANES), f32)]),
        compiler_params=_cparams("arbitrary", "arbitrary"), name="moe_gather",
    )(idx, h2t)


def _cast3_kernel(a_ref, b_ref, c_ref, ao_ref, bo_ref, co_ref):
    ao_ref[...] = a_ref[...].astype(bf16)
    bo_ref[...] = b_ref[...].astype(bf16)
    co_ref[...] = c_ref[...].astype(bf16)


def _expert0_bf16(wg, wu, wd, layer):
    spec = pl.BlockSpec((None, None, D, EXPERT_FF), lambda i: (layer, 0, 0, 0))
    ospec = pl.BlockSpec((D, EXPERT_FF), lambda i: (0, 0))
    return pl.pallas_call(
        _cast3_kernel, out_shape=[jax.ShapeDtypeStruct((D, EXPERT_FF), bf16)] * 3, grid=(1,),
        in_specs=[spec] * 3, out_specs=[ospec] * 3, compiler_params=_cparams("arbitrary"), name="expert0_cast",
    )(wg, wu, wd)


def _ffn_kernel(tok_ref, g0_ref, u0_ref, d0_ref, gq_ref, uq_ref, dq_ref, o_ref, w_even, w_odd):
    e = pl.program_id(0)
    i = pl.program_id(1)

    @pl.when((e == 0) & (i == 0))
    def _():
        w_even[0] = g0_ref[...]
        w_even[1] = u0_ref[...]
        w_even[2] = d0_ref[...]

    rows = gq_ref.shape[0]
    r0 = pl.multiple_of(i * rows, rows)

    def step(use, fill):
        n_s, cap, _ = tok_ref.shape
        t = tok_ref[...].reshape(n_s * cap, D)
        g = jnp.dot(t, use[0], preferred_element_type=f32)
        for k, q_ref in enumerate((gq_ref, uq_ref, dq_ref)):
            fill[k, pl.ds(r0, rows), :] = q_ref[...].astype(bf16)
        u = jnp.dot(t, use[1], preferred_element_type=f32)
        a = (g * _sigmoid(g) * u).astype(bf16)
        o_ref[...] = jnp.dot(a, use[2], preferred_element_type=f32).astype(bf16).reshape(n_s, cap, D)

    @pl.when(e % 2 == 0)
    def _():
        step(w_even, w_odd)

    @pl.when(e % 2 == 1)
    def _():
        step(w_odd, w_even)


def _ffn(tok, wg, wu, wd, layer):
    b, _, cap, _ = tok.shape
    n_s = MOE_PAIR if b % MOE_PAIR == 0 else 1
    steps = b // n_s
    assert D == EXPERT_FF and D % steps == 0 and (D // steps) % BF16_ROWS == 0
    rows = D // steps
    w0 = _expert0_bf16(wg, wu, wd, layer)
    w0spec = pl.BlockSpec((D, EXPERT_FF), lambda e, i: (0, 0), pipeline_mode=pl.Buffered(1))
    qspec = pl.BlockSpec((None, None, rows, D), lambda e, i: (layer, jnp.minimum(e + 1, N_EXPERTS - 1), i, 0))
    return pl.pallas_call(
        _ffn_kernel,
        out_shape=jax.ShapeDtypeStruct((b, N_EXPERTS, cap, D), bf16),
        grid=(N_EXPERTS, steps),
        in_specs=[pl.BlockSpec((n_s, None, cap, D), lambda e, i: (i, e, 0, 0)),
                  w0spec, w0spec, w0spec, qspec, qspec, qspec],
        out_specs=pl.BlockSpec((n_s, None, cap, D), lambda e, i: (i, e, 0, 0)),
        scratch_shapes=[pltpu.VMEM((3, D, EXPERT_FF), bf16), pltpu.VMEM((3, D, EXPERT_FF), bf16)],
        compiler_params=_cparams("arbitrary", "arbitrary"), name="moe_ffn",
    )(tok, *w0, wg, wu, wd)


SCATTER_UNROLL = 8


def _scatter_kernel(idx_ref, val_ref, o_ref, acc_ref, tbuf, *, cap):
    e = pl.program_id(1)
    sp = cap + SUBLANES

    @pl.when(e == 0)
    def _():
        acc_ref[...] = jnp.zeros_like(acc_ref)

    for k in range(MOE_PAIR):
        base = (pl.program_id(0) * N_EXPERTS + e * MOE_PAIR + k) * cap
        for c in range(D // LANES):
            tbuf[pl.ds(c * sp, cap), :] = o_ref[k, :, c * LANES:(c + 1) * LANES].astype(f32)
        u = min(SCATTER_UNROLL, cap)
        for p0 in range(0, cap, u):
            pend = []
            for p in range(p0, p0 + u):
                r0 = pl.multiple_of(idx_ref[base + p] * SUBLANES, SUBLANES)
                pend.append((r0, acc_ref[pl.ds(r0, SUBLANES), :]
                             + tbuf[pl.ds(p, SUBLANES, stride=sp), :] * val_ref[base + p]))
            for r0, v in pend:
                acc_ref[pl.ds(r0, SUBLANES), :] = v


def _scatter(idx, val, o, rows):
    b, _, cap, _ = o.shape
    return pl.pallas_call(
        functools.partial(_scatter_kernel, cap=cap),
        out_shape=jax.ShapeDtypeStruct((b, rows, LANES), f32),
        grid_spec=pltpu.PrefetchScalarGridSpec(
            num_scalar_prefetch=2, grid=(b, N_EXPERTS // MOE_PAIR),
            in_specs=[pl.BlockSpec((None, MOE_PAIR, cap, D), lambda i, e, idx, val: (i, e, 0, 0))],
            out_specs=pl.BlockSpec((None, rows, LANES), lambda i, e, idx, val: (i, 0, 0)),
            scratch_shapes=[pltpu.VMEM((SUBLANES * (cap + SUBLANES), LANES), f32)]),
        compiler_params=_cparams("arbitrary", "arbitrary"), name="moe_scatter",
    )(idx, val, o)


def _moe(aff_t, h2t, wg, wu, wd, layer, n_lat, n_ctx):
    idx, val = _route(aff_t, n_lat, CAP_FACTOR * n_lat // N_EXPERTS, 0)
    if n_ctx:
        idx_c, val_c = _route(aff_t, n_ctx, CAP_FACTOR * n_ctx // N_EXPERTS, n_lat // n_ctx)
        idx = jnp.concatenate([idx, idx_c + n_lat], axis=-1)
        val = jnp.concatenate([val, val_c], axis=-1)
    cap = idx.shape[-1]
    idx = idx.reshape(-1)
    val = val.reshape(-1)
    tok = _gather(idx, h2t, cap)
    o = _ffn(tok, wg, wu, wd, layer)
    return _scatter(idx, val, o, h2t.shape[1])


def _final_kernel(x_ref, acc_ref, mod_ref, n_ref, o_ref):
    x = x_ref[...] + mod_ref[:, 5 * D:6 * D] * _tiles_to_rows(acc_ref, x_ref.shape[0])
    o_ref[...] = (x * lax.rsqrt(jnp.mean(x * x, axis=-1, keepdims=True) + EPS)) * n_ref[...]


def _final(xs, acc, modsel, norm, n_lat):
    b = xs.shape[0]
    tm = 2 * TM if n_lat % (2 * TM) == 0 else TM
    return pl.pallas_call(
        _final_kernel,
        out_shape=jax.ShapeDtypeStruct((b, n_lat, D), f32),
        grid=(b, n_lat // tm),
        in_specs=[pl.BlockSpec((None, tm, D), lambda i, t: (i, t, 0)),
                  pl.BlockSpec((None, tm * SUBLANES, LANES), lambda i, t: (i, t, 0)),
                  pl.BlockSpec((None, None, 1, N_MOD * D), lambda i, t: (i, 0, 0, 0)),
                  pl.BlockSpec((1, D), lambda i, t: (0, 0))],
        out_specs=pl.BlockSpec((None, tm, D), lambda i, t: (i, t, 0)),
        compiler_params=_cparams("arbitrary", "arbitrary"), name="final_norm",
    )(xs, acc, modsel, norm)


def _rope_tables(n_lat, n_ctx):
    pos = jnp.arange(n_lat, dtype=f32)
    rows = jnp.floor(pos / GRID_W)
    cols = pos - rows * GRID_W
    half = HEAD_DIM // 4
    inv = 1.0 / (ROPE_BASE ** (jnp.arange(0, 2 * half, 2, dtype=f32) / (2 * half)))
    lane = jnp.arange(LANES)
    d = lane % HEAD_DIM
    ang = jnp.where((d < 2 * half)[None, :], rows[:, None], cols[:, None]) * inv[d % half][None, :]
    first = ((d % (2 * half)) < half)[None, :]
    cos = jnp.cos(ang)
    sin = jnp.sin(ang)
    s1 = jnp.where(first, -sin, 0.0)
    s2 = jnp.where(first, 0.0, sin)
    pad = lambda t, v: jnp.concatenate([t, jnp.full((n_ctx, LANES), v, f32)], axis=0)
    return pad(cos, 1.0), pad(s1, 0.0), pad(s2, 0.0)


def _in_weights_bf16(w):
    scale_b = jnp.concatenate([jnp.ones((G_RY,), f32), jnp.full((G_G,), 0.5, f32)])
    return w[:, :W_A_COLS].astype(bf16), (w[:, W_A_COLS:] * scale_b[None, :]).astype(bf16)


def kernel(x, c, ctx, c_ctx, norm1, norm2, w_mod, b_mod, w_in, conv_a, w_conv_out, attn_sink, w_attn_out, rnn_conv_w,
           rnn_conv_b, rnn_w_a, rnn_b_a, rnn_w_x, rnn_b_x, rnn_lam, w_rnn_out, w_o, w_router, w_e_gate, w_e_up,
           w_e_down, final_norm):
    b, n_lat, d = x.shape
    n_ctx = ctx.shape[1]
    depth = w_mod.shape[0]
    assert d == D and n_ctx == TM and n_lat % TM == 0 and b < SUBLANES
    nt = n_lat + n_ctx
    ct = nt // TM - 1

    cstack = jnp.concatenate([c, c_ctx[None, :], jnp.zeros((SUBLANES - b - 1, D), f32)], axis=0)
    mod = _modulation(cstack, w_mod, b_mod)
    tabs = _rope_tables(n_lat, n_ctx)

    xs = x
    acc = None
    modsel_prev = None
    for l in range(depth):
        ctx_out = l < depth - 1
        ctx_sep = ctx if l == 0 else None
        modsel = jnp.stack([mod[l, :b], jnp.broadcast_to(mod[l, b], (b, N_MOD * D))], axis=1)[:, :, None, :]
        rnn_prm = []
        for dr in range(2):
            wg = (0.5 * jnp.concatenate([rnn_w_a[l, dr], rnn_w_x[l, dr]], axis=-1)).astype(bf16)
            rnn_prm.append((rnn_conv_w[l, dr], rnn_conv_b[l, dr][None, :], wg, rnn_b_a[l, dr][None, :],
                            rnn_b_x[l, dr][None, :], rnn_lam[l, dr][None, :]))
        w_a, w_b = _in_weights_bf16(w_in[l])
        outs = _proj_a(xs, ctx_sep, acc, modsel_prev, modsel, norm1[l][None, :], tabs, w_a, rnn_prm[0])
        if acc is not None:
            xs, outs = outs[0], outs[1:]
        za, zq, zkv, zrx, hn, hf = outs
        zry, zg, hb = _proj_b(hn, w_b, zrx, rnn_prm[1])
        att = _attention(attn_sink[l][None, :], zq, zkv, n_lat, (nt if ctx_out else n_lat) // QB)
        xs, h2t, aff_t = _merge(xs, ctx_sep, za, att, hf, hb, zry, zg, modsel, conv_a[l],
                                w_conv_out[l].astype(bf16),
                                w_attn_out[l].astype(bf16), (0.5 * w_rnn_out[l]).astype(bf16),
                                (0.5 * w_o[l]).astype(bf16),
                                norm2[l][None, :], w_router[l].T.astype(bf16), (ct + 1) if ctx_out else ct)
        acc = _moe(aff_t, h2t, w_e_gate, w_e_up, w_e_down, l, n_lat, n_ctx if ctx_out else 0)
        modsel_prev = modsel
    return _final(xs, acc, modsel_prev, final_norm[None, :], n_lat)
```

```python
import functools

import jax
import jax.numpy as jnp
from jax import lax
from jax.experimental import pallas as pl
from jax.experimental.pallas import tpu as pltpu

f32 = jnp.float32
bf16 = jnp.bfloat16
i32 = jnp.int32

D = 1024
EPS = 1e-6
CONV_W = 512
N_HEADS = 8
N_KV = 2
HEAD_DIM = 64
Q_W = N_HEADS * HEAD_DIM
KV_W = N_KV * HEAD_DIM
WINDOW = 128
GRID_W = 64
ROPE_BASE = 10000.0
NEG_INF = -1e30
RNN_W = 1024
RNN_BLOCKS = 8
RNN_BLK = RNN_W // RNN_BLOCKS
RNN_CONV_K = 4
RG_C = 8.0
N_EXPERTS = 16
EXPERT_FF = 1024
CAP_FACTOR = 2
N_MOD = 6

LANES = 128
SUBLANES = 8
BF16_ROWS = 16
VMEM_LIMIT = 56 * 1024 * 1024

TM = 256
QB = 128
K2_W = 2 * KV_W

G_A = 3 * CONV_W
G_QKV = Q_W + 2 * KV_W
G_RX = RNN_W
G_RY = RNN_W
G_G = 3 * D
IN_COLS = (G_A, G_QKV, G_RX, G_RY, G_G)
IN_TOTAL = sum(IN_COLS)
KV2_W = 2 * K2_W
ZA_W = 2 * CONV_W
OUT_COLS = (ZA_W, Q_W, KV2_W, G_RX, G_RY, G_G)


def _cparams(*sem):
    return pltpu.CompilerParams(dimension_semantics=sem, vmem_limit_bytes=VMEM_LIMIT)


def _sigmoid(x):
    return 0.5 * jnp.tanh(0.5 * x) + 0.5


def _rms_mod(x, g, sc, sh):
    y = x * lax.rsqrt(jnp.mean(x * x, axis=-1, keepdims=True) + EPS)
    return (y * g) * (1.0 + sc) + sh


def _mod_kernel(c_ref, w_ref, b_ref, o_ref):
    c = c_ref[...]
    s = (c * _sigmoid(c)).astype(bf16)
    o_ref[...] = jnp.dot(s, w_ref[...].astype(bf16), preferred_element_type=f32) + b_ref[...]


def _modulation(cstack, w_mod, b_mod):
    depth = w_mod.shape[0]
    tn = 1536
    return pl.pallas_call(
        _mod_kernel,
        out_shape=jax.ShapeDtypeStruct((depth, SUBLANES, N_MOD * D), f32),
        grid=(depth, N_MOD * D // tn),
        in_specs=[
            pl.BlockSpec((SUBLANES, D), lambda l, j: (0, 0)),
            pl.BlockSpec((None, D, tn), lambda l, j: (l, 0, j)),
            pl.BlockSpec((None, 1, tn), lambda l, j: (l, 0, j)),
        ],
        out_specs=pl.BlockSpec((None, SUBLANES, tn), lambda l, j: (l, 0, j)),
        compiler_params=_cparams("arbitrary", "arbitrary"),
        name="modulation",
    )(cstack, w_mod, b_mod.reshape(depth, 1, N_MOD * D))


def _tiles_to_rows(ref, rows):
    return jnp.concatenate([ref[pl.ds(c, rows, stride=SUBLANES), :] for c in range(D // LANES)], axis=1)


def _rows_to_tiles(ref, val, rows):
    per = D // LANES
    for r in range(rows // SUBLANES):
        for c in range(per):
            ref[pl.ds(r * SUBLANES * per + c, SUBLANES, stride=per), :] = val[r * SUBLANES:(r + 1) * SUBLANES,
                                                                              c * LANES:(c + 1) * LANES]


def _rope(z, cos, s1, s2):
    outs = []
    for c in range(z.shape[1] // LANES):
        x = z[:, c * LANES:(c + 1) * LANES]
        outs.append(x * cos + pltpu.roll(x, LANES - 16, axis=1) * s1 + pltpu.roll(x, 16, axis=1) * s2)
    return jnp.concatenate(outs, axis=1)


W_A_COLS = G_A + G_QKV + G_RX


def _proj_a_kernel(*refs, with_moe, ct):
    if with_moe:
        (x_ref, acc_ref, modp_ref), refs = refs[:3], refs[3:]
    else:
        (x_ref, ctx_ref), refs = refs[:2], refs[2:]
    mod_ref, n_ref, cos_ref, s1_ref, s2_ref, w_ref = refs[:6]
    prm, refs = refs[6:12], refs[12:]
    if with_moe:
        xo_ref, refs = refs[0], refs[1:]
    za_ref, zq_ref, zkv_ref, zrx_ref, h_ref, hf_ref, ubuf, obuf, hcar, halo_s = refs
    s = pl.program_id(1)
    chunk = _rnn_chunk(s, ct, False)

    @pl.when(s == 0)
    def _():
        hcar[...] = jnp.zeros_like(hcar)
        halo_s[...] = jnp.zeros_like(halo_s)

    if with_moe:
        x = x_ref[...] + modp_ref[:, 5 * D:6 * D] * _tiles_to_rows(acc_ref, TM)
        xo_ref[...] = x
    else:
        x = jnp.where(chunk == ct, ctx_ref[...], x_ref[...])
    h = _rms_mod(x, n_ref[...], mod_ref[:, D:2 * D], mod_ref[:, 0:D]).astype(bf16)
    h_ref[...] = h
    hal = jnp.where((chunk != 0) & (chunk != ct), halo_s[...], 0.0)
    quarter = G_RX // 4

    def scan_input(p):
        cols = slice(p * quarter, (p + 1) * quarter)
        u = jnp.dot(h, w_ref[:, G_A + G_QKV + p * quarter:G_A + G_QKV + (p + 1) * quarter],
                    preferred_element_type=f32)
        zrx_ref[:, cols] = u.astype(bf16)
        _rnn_fill(u, hal[:, cols], ubuf, False, c0=p * quarter // LANES)
        halo_s[:, cols] = u[TM - BF16_ROWS:, :]

    scan = lambda c: _rnn_tile(c, prm, ubuf, obuf, hcar, False)
    cw = CONV_W
    scan_input(0)
    a_x = jnp.dot(h, w_ref[:, 0:cw], preferred_element_type=f32)
    scan(0)
    scan_input(1)
    scan(1)
    a_b = jnp.dot(h, w_ref[:, cw:2 * cw], preferred_element_type=f32)
    scan(2)
    scan_input(2)
    scan(3)
    a_c = jnp.dot(h, w_ref[:, 2 * cw:G_A], preferred_element_type=f32)
    za_ref[...] = jnp.concatenate([a_c * a_x, a_b], axis=1).astype(bf16)
    scan(4)
    scan_input(3)
    scan(5)
    z = jnp.dot(h, w_ref[:, G_A:G_A + G_QKV], preferred_element_type=f32)
    qk = _rope(z[:, :Q_W + KV_W], cos_ref[...], s1_ref[...], s2_ref[...])
    zq_ref[...] = (qk[:, :Q_W] * (HEAD_DIM ** -0.5)).astype(bf16)
    lo = lax.broadcasted_iota(i32, (TM, LANES), 1) < HEAD_DIM
    dup = []
    for pair in (qk[:, Q_W:], z[:, Q_W + KV_W:]):
        swapped = pltpu.roll(pair, HEAD_DIM, axis=1)
        dup += [jnp.where(lo, pair, swapped), jnp.where(lo, swapped, pair)]
    zkv_ref[...] = jnp.concatenate(dup, axis=1).astype(bf16)
    scan(6)
    scan(7)
    hf_ref[...] = _rnn_states(obuf).astype(bf16)


def _proj_b_kernel(h_ref, w_ref, u_ref, halo_ref, *refs, ct):
    prm, (zry_ref, zg_ref, hb_ref, ubuf, obuf, hcar) = refs[:6], refs[6:]
    s = pl.program_id(1)
    chunk = _rnn_chunk(s, ct, True)

    @pl.when(s == 0)
    def _():
        hcar[...] = jnp.zeros_like(hcar)

    h = h_ref[...]
    hal = jnp.where((chunk != ct - 1) & (chunk != ct), halo_ref[...].astype(f32), 0.0)
    _rnn_fill(u_ref[...].astype(f32), hal, ubuf, True)
    slab = (G_RY + G_G) // RNN_LT
    for c in range(RNN_LT):
        z = jnp.dot(h, w_ref[:, c * slab:(c + 1) * slab], preferred_element_type=f32)
        if (c + 1) * slab <= G_RY:
            t_y = jnp.tanh(z * (0.7978845608028654 + (0.7978845608028654 * 0.044715) * (z * z)))
            zry_ref[:, c * slab:(c + 1) * slab] = (z * t_y + z).astype(bf16)
        else:
            zg_ref[:, c * slab - G_RY:(c + 1) * slab - G_RY] = z.astype(bf16)
        _rnn_tile(c, prm, ubuf, obuf, hcar, True)
    hb_ref[...] = _rnn_states(obuf).astype(bf16)


def _proj_a(xs, ctx, acc, modsel_prev, modsel, norm, tabs, w_a, rnn_prm):
    with_moe = acc is not None
    b = xs.shape[0]
    nt = xs.shape[1] + (0 if with_moe else ctx.shape[1])
    n_tiles = nt // TM
    ct = n_tiles - 1
    ch = lambda s: _rnn_chunk(s, ct, False)
    tile = lambda w: pl.BlockSpec((None, TM, w), lambda i, s: (i, ch(s), 0))
    mod_spec = pl.BlockSpec((None, None, 1, N_MOD * D), lambda i, s: (i, ch(s) // ct, 0, 0))
    tab_spec = pl.BlockSpec((TM, LANES), lambda i, s: (ch(s), 0))
    if with_moe:
        in_specs = [tile(D), pl.BlockSpec((None, TM * SUBLANES, LANES), lambda i, s: (i, ch(s), 0)), mod_spec]
        args = [xs, acc, modsel_prev]
    else:
        in_specs = [pl.BlockSpec((None, TM, D), lambda i, s: (i, jnp.minimum(ch(s), ct - 1), 0)),
                    pl.BlockSpec((None, TM, D), lambda i, s: (i, 0, 0))]
        args = [xs, ctx]
    in_specs += [mod_spec, pl.BlockSpec((1, D), lambda i, s: (0, 0)), tab_spec, tab_spec, tab_spec,
                 pl.BlockSpec((D, W_A_COLS), lambda i, s: (0, 0), pipeline_mode=pl.Buffered(1))]
    in_specs += _rnn_param_specs()
    args += [modsel, norm, *tabs, w_a, *rnn_prm]
    widths = (ZA_W, Q_W, KV2_W, G_RX, D, RNN_W)
    out_shape = [jax.ShapeDtypeStruct((b, nt, w), bf16) for w in widths]
    out_specs = [tile(w) for w in widths]
    if with_moe:
        out_shape = [jax.ShapeDtypeStruct((b, nt, D), f32)] + out_shape
        out_specs = [tile(D)] + out_specs
    return pl.pallas_call(
        functools.partial(_proj_a_kernel, with_moe=with_moe, ct=ct),
        out_shape=out_shape, grid=(b, n_tiles), in_specs=in_specs, out_specs=out_specs,
        scratch_shapes=_RNN_SCRATCH + [pltpu.VMEM((BF16_ROWS, RNN_W), f32)],
        compiler_params=_cparams("arbitrary", "arbitrary"), name="proj_a_rnn_fwd",
    )(*args)


def _proj_b(h, w_b, zrx, rnn_prm):
    b, nt, _ = h.shape
    n_tiles = nt // TM
    ct = n_tiles - 1
    per = TM // BF16_ROWS
    ch = lambda s: _rnn_chunk(s, ct, True)
    tile = lambda w: pl.BlockSpec((None, TM, w), lambda i, s: (i, ch(s), 0))
    halo = pl.BlockSpec((None, BF16_ROWS, RNN_W),
                        lambda i, s: (i, jnp.minimum((ch(s) + 1) * per, nt // BF16_ROWS - 1), 0))
    widths = (G_RY, G_G, RNN_W)
    return pl.pallas_call(
        functools.partial(_proj_b_kernel, ct=ct),
        out_shape=[jax.ShapeDtypeStruct((b, nt, w), bf16) for w in widths],
        grid=(b, n_tiles),
        in_specs=[tile(D), pl.BlockSpec((D, G_RY + G_G), lambda i, s: (0, 0), pipeline_mode=pl.Buffered(1)),
                  tile(RNN_W), halo] + _rnn_param_specs(),
        out_specs=[tile(w) for w in widths],
        scratch_shapes=_RNN_SCRATCH,
        compiler_params=_cparams("arbitrary", "arbitrary"), name="proj_b_rnn_bwd",
    )(h, w_b, zrx, zrx, *rnn_prm)


ATT_QBLOCKS = 2


def _attn_kernel(sink_ref, q_ref, kvp_ref, kvc_ref, kvn_ref, kvx_ref, o_ref, *, n_lat_blocks, n_ctx):
    group = N_HEADS // N_KV
    nlb = n_lat_blocks
    lo = lax.broadcasted_iota(i32, (QB, LANES), 1) < HEAD_DIM
    diff = lax.broadcasted_iota(i32, (QB, QB), 1) - lax.broadcasted_iota(i32, (QB, QB), 0)
    zero = jnp.zeros((QB, LANES), bf16)
    def score(sub, g):
        window = (kvp_ref, kvc_ref) if sub == 0 else (kvc_ref, kvn_ref)
        ksl = slice(g * LANES, (g + 1) * LANES)
        k = jnp.concatenate([r[:, ksl] for r in window] + [kvx_ref[:, ksl]], axis=0)
        parts = []
        for pr in range(group // 2):
            qp = q_ref[sub * QB:(sub + 1) * QB, (g * (group // 2) + pr) * LANES:(g * (group // 2) + pr + 1) * LANES]
            parts += [jnp.where(lo, qp, zero), jnp.where(lo, zero, qp)]
        return lax.dot_general(jnp.concatenate(parts, axis=0), k, (((1,), (1,)), ((), ())),
                               preferred_element_type=f32)

    def finish(sub, g, s):
        j = pl.program_id(1) * ATT_QBLOCKS + sub
        lat = j < nlb
        ok = (diff >= jnp.where(lat & (j >= 1), 0, QB),
              diff >= jnp.where(lat, -QB, QB),
              -diff >= jnp.where(lat & (j <= nlb - 2), 0, QB))
        rows = slice(sub * QB, (sub + 1) * QB)
        window = (kvp_ref, kvc_ref) if sub == 0 else (kvc_ref, kvn_ref)
        vsl = slice(K2_W + g * LANES, K2_W + (g + 1) * LANES)
        v = jnp.concatenate([r[:, vsl] for r in window] + [kvx_ref[:, vsl]], axis=0)
        ps, inv = [], []
        for hh in range(group):
            sh = s[hh * QB:(hh + 1) * QB]
            sm = jnp.concatenate([jnp.where(ok[kb], sh[:, kb * QB:(kb + 1) * QB], NEG_INF) for kb in range(3)]
                                 + [sh[:, 3 * QB:]], axis=1)
            sink = sink_ref[0, g * group + hh]
            m = jnp.maximum(jnp.max(sm, axis=1, keepdims=True), sink)
            p = jnp.exp(sm - m)
            inv.append(1.0 / (jnp.sum(p, axis=1, keepdims=True) + jnp.exp(sink - m)))
            ps.append(p.astype(bf16))
        o = jnp.dot(jnp.concatenate(ps, axis=0), v, preferred_element_type=f32) * jnp.concatenate(inv, axis=0)
        for pr in range(group // 2):
            pair = jnp.where(lo, o[(2 * pr) * QB:(2 * pr + 1) * QB], o[(2 * pr + 1) * QB:(2 * pr + 2) * QB])
            c0 = (g * (group // 2) + pr) * LANES
            o_ref[rows, c0:c0 + LANES] = pair.astype(bf16)

    units = [(sub, g) for sub in range(ATT_QBLOCKS) for g in range(N_KV)]
    pending = score(*units[0])
    for n, unit in enumerate(units):
        ahead = score(*units[n + 1]) if n + 1 < len(units) else None
        finish(*unit, pending)
        pending = ahead


def _attention(sink, zq, zkv, n_lat, n_qblocks):
    b, nt, _ = zq.shape
    n_ctx = nt - n_lat
    nlb = n_lat // QB
    cidx = n_lat // n_ctx
    nq = ATT_QBLOCKS
    assert nq == 2 and n_qblocks % nq == 0 and nlb % nq == 0
    edge = lambda d: pl.BlockSpec((None, QB, KV2_W), lambda i, j: (i, jnp.clip(nq * j + d, 0, nlb - 1), 0))
    ctx = pl.BlockSpec((None, n_ctx, KV2_W), lambda i, j: (i, cidx, 0))
    return pl.pallas_call(
        functools.partial(_attn_kernel, n_lat_blocks=nlb, n_ctx=n_ctx),
        out_shape=jax.ShapeDtypeStruct((b, n_qblocks * QB, Q_W), bf16),
        grid=(b, n_qblocks // nq),
        in_specs=[pl.BlockSpec(memory_space=pltpu.SMEM),
                  pl.BlockSpec((None, nq * QB, Q_W), lambda i, j: (i, j, 0)),
                  edge(-1), pl.BlockSpec((None, nq * QB, KV2_W), lambda i, j: (i, j, 0)), edge(nq), ctx],
        out_specs=pl.BlockSpec((None, nq * QB, Q_W), lambda i, j: (i, j, 0)),
        compiler_params=_cparams("arbitrary", "arbitrary"), name="attention",
    )(sink, zq, zkv, zkv, zkv, zkv)


def _rnn_chunk(s, ct, rev):
    return jnp.where(s == 0, ct, (ct - s) if rev else (s - 1))


def _block_scan(a, b, h_in, rowi, rev):
    n = SUBLANES
    order = range(n - 1, -1, -1) if rev else range(n)
    hs = [None] * n
    ps = [None] * n
    prev = None
    for j in order:
        if prev is None:
            hs[j], ps[j] = b[j], a[j]
        else:
            hs[j], ps[j] = a[j] * hs[prev] + b[j], a[j] * ps[prev]
        prev = j
    he, pe = hs[prev], ps[prev]
    for sh in (1, 2, 4):
        rs = n - sh if rev else sh
        msk = (rowi < n - sh) if rev else (rowi >= sh)
        he, pe = (jnp.where(msk, he + pe * pltpu.roll(he, rs, axis=0), he),
                  jnp.where(msk, pe * pltpu.roll(pe, rs, axis=0), pe))
    e = he + pe * h_in
    carry = jnp.where((rowi == n - 1) if rev else (rowi == 0), h_in, pltpu.roll(e, n - 1 if rev else 1, axis=0))
    out = [hs[j] + ps[j] * carry for j in range(n)]
    last = e[0:1, :] if rev else e[n - 1:n, :]
    return out, jnp.broadcast_to(last, (n, LANES))


RNN_LT = RNN_W // LANES


def _rnn_fill(u, hal, ubuf, rev, c0=0):
    for k in range(u.shape[1] // LANES):
        sl = slice(k * LANES, (k + 1) * LANES)
        if rev:
            ubuf[c0 + k, 0:TM, :] = u[:, sl]
            ubuf[c0 + k, TM:TM + BF16_ROWS, :] = hal[:, sl]
        else:
            ubuf[c0 + k, 0:BF16_ROWS, :] = hal[:, sl]
            ubuf[c0 + k, BF16_ROWS:BF16_ROWS + TM, :] = u[:, sl]


def _rnn_tile(c, prm, ubuf, obuf, hcar, rev):
    cw_ref, cb_ref, wg_ref, ba_ref, bx_ref, lam_ref = prm
    base = 0 if rev else BF16_ROWS - (RNN_CONV_K - 1)
    rowi = lax.broadcasted_iota(i32, (SUBLANES, LANES), 0)
    blk_rows = SUBLANES * SUBLANES
    n_blk = TM // blk_rows
    sl = slice(c * LANES, (c + 1) * LANES)
    taps = [cw_ref[k:k + 1, sl] for k in range(RNN_CONV_K)]
    xt = []
    for blk in range(n_blk):
        for j in range(SUBLANES):
            r0 = base + blk * blk_rows + j
            acc = cb_ref[:, sl] + ubuf[c, pl.ds(r0, SUBLANES, stride=SUBLANES), :] * taps[0]
            for k in range(1, RNN_CONV_K):
                acc = acc + ubuf[c, pl.ds(r0 + k, SUBLANES, stride=SUBLANES), :] * taps[k]
            xt.append(acc)
    xp = jnp.concatenate(xt, axis=0)
    g = jnp.dot(xp.astype(bf16), wg_ref[c], preferred_element_type=f32)
    t_r = jnp.tanh(g[:, :RNN_BLK] + 0.5 * ba_ref[:, sl])
    t_i = jnp.tanh(g[:, RNN_BLK:] + 0.5 * bx_ref[:, sl])
    c4 = (-0.5 * RG_C) * jnp.logaddexp(-lam_ref[:, sl], 0.0)
    log_a = c4 * t_r + c4
    a = jnp.exp(log_a)
    y = -jnp.tanh(log_a) * (a * a + 1.0)
    xh = 0.5 * xp
    bb = jnp.where(y > 0.0, y * lax.rsqrt(y), 0.0) * (xh * t_i + xh)
    h_in = hcar[:, sl]
    for blk in (range(n_blk - 1, -1, -1) if rev else range(n_blk)):
        rows = lambda j: slice((blk * SUBLANES + j) * SUBLANES, (blk * SUBLANES + j + 1) * SUBLANES)
        hs, h_in = _block_scan([a[rows(j)] for j in range(SUBLANES)], [bb[rows(j)] for j in range(SUBLANES)],
                               h_in, rowi, rev)
        for j in range(SUBLANES):
            obuf[c, pl.ds(blk * blk_rows + j, SUBLANES, stride=SUBLANES), :] = hs[j]
    hcar[:, sl] = h_in


def _rnn_states(obuf):
    return jnp.concatenate([obuf[c] for c in range(RNN_LT)], axis=1)


_RNN_SCRATCH = [pltpu.VMEM((RNN_LT, TM + BF16_ROWS, LANES), f32),
                pltpu.VMEM((RNN_LT, TM, LANES), f32),
                pltpu.VMEM((SUBLANES, RNN_W), f32)]


def _rnn_param_specs():
    full = lambda shape: pl.BlockSpec(shape, lambda i, s: (0,) * len(shape))
    return [full((RNN_CONV_K, RNN_W)), full((1, RNN_W)), full((RNN_BLOCKS, RNN_BLK, 2 * RNN_BLK)),
            full((1, RNN_W)), full((1, RNN_W)), full((1, RNN_W))]


def _merge_kernel(*refs, ct, split):
    (za_ref, zap_ref, zan_ref, att_ref, hf_ref, hb_ref, ry_ref, zg_ref, mod_ref, ca_ref, wc_ref,
     wa_ref, wr_ref, wo_ref, n2_ref, wrt_ref, shift_ref, xo_ref, h2_ref, aff_ref) = refs[2 if split else 1:]
    t = pl.program_id(1)
    if split:
        x_in = jnp.where(t == ct, refs[1][...], refs[0][...])
    else:
        x_in = refs[0][...]
    cw = CONV_W
    cu_b = za_ref[:, 0:cw]
    sh = jnp.dot(shift_ref[...], cu_b, preferred_element_type=f32)
    prev_ok = (t != 0) & (t != ct)
    next_ok = (t != ct - 1) & (t != ct)
    cu_p = jnp.where(prev_ok, zap_ref[BF16_ROWS - 1:BF16_ROWS, 0:cw].astype(f32), 0.0)
    cu_n = jnp.where(next_ok, zan_ref[0:1, 0:cw].astype(f32), 0.0)
    rowi = lax.broadcasted_iota(i32, (TM, cw), 0)
    cu_prev = jnp.where(rowi == 0, cu_p, sh[:TM])
    cu_next = jnp.where(rowi == TM - 1, cu_n, sh[TM:])
    y = cu_prev * ca_ref[0:1, :] + cu_b.astype(f32) * ca_ref[1:2, :] + cu_next * ca_ref[2:3, :]
    cnv = jnp.dot((za_ref[:, cw:2 * cw].astype(f32) * y).astype(bf16), wc_ref[...], preferred_element_type=f32)
    att = jnp.dot(att_ref[...], wa_ref[...], preferred_element_type=f32)
    rec = hf_ref[...].astype(f32) + hb_ref[...].astype(f32)
    rnn = jnp.dot((ry_ref[...].astype(f32) * rec).astype(bf16), wr_ref[...], preferred_element_type=f32)
    mix2 = None
    for k, br in enumerate((cnv, att, rnn)):
        term = jnp.tanh(zg_ref[:, k * D:(k + 1) * D].astype(f32)) * br + br
        mix2 = term if mix2 is None else mix2 + term
    x = x_in + mod_ref[:, 2 * D:3 * D] * jnp.dot(mix2.astype(bf16), wo_ref[...], preferred_element_type=f32)
    xo_ref[...] = x
    h2 = _rms_mod(x, n2_ref[...], mod_ref[:, 4 * D:5 * D], mod_ref[:, 3 * D:4 * D])
    _rows_to_tiles(h2_ref, h2, TM)
    logits = lax.dot_general(wrt_ref[...], h2.astype(bf16), (((1,), (1,)), ((), ())), preferred_element_type=f32)
    e = jnp.exp(logits - jnp.max(logits, axis=0, keepdims=True))
    aff_ref[...] = e / jnp.sum(e, axis=0, keepdims=True)


def _row_shift_operator():
    r = jnp.arange(2 * TM)[:, None]
    c = jnp.arange(TM)[None, :]
    return (c == jnp.where(r < TM, r - 1, r - TM + 1)).astype(bf16)


def _merge(xs, ctx, za, att, hf, hb, zry, zg, modsel, conv_a, wc, wa, wr, wo, norm2, wrt, n_tiles_eff):
    b, nt, _ = za.shape
    ct = nt // TM - 1
    per = TM // BF16_ROWS
    split = ctx is not None
    tile = lambda w: pl.BlockSpec((None, TM, w), lambda i, t: (i, t, 0))
    full = lambda shape: pl.BlockSpec(shape, lambda i, t: (0,) * len(shape), pipeline_mode=pl.Buffered(1))
    if split:
        x_specs = [pl.BlockSpec((None, TM, D), lambda i, t: (i, jnp.minimum(t, ct - 1), 0)),
                   pl.BlockSpec((None, TM, D), lambda i, t: (i, 0, 0))]
        x_args = [xs, ctx]
    else:
        x_specs = [tile(D)]
        x_args = [xs]
    return pl.pallas_call(
        functools.partial(_merge_kernel, ct=ct, split=split),
        out_shape=[jax.ShapeDtypeStruct((b, n_tiles_eff * TM, D), f32),
                   jax.ShapeDtypeStruct((b, n_tiles_eff * TM * SUBLANES, LANES), f32),
                   jax.ShapeDtypeStruct((b, N_EXPERTS, n_tiles_eff * TM), f32)],
        grid=(b, n_tiles_eff),
        in_specs=x_specs + [tile(ZA_W),
                  pl.BlockSpec((None, BF16_ROWS, ZA_W), lambda i, t: (i, jnp.maximum(t * per - 1, 0), 0)),
                  pl.BlockSpec((None, BF16_ROWS, ZA_W),
                               lambda i, t: (i, jnp.minimum((t + 1) * per, nt // BF16_ROWS - 1), 0)),
                  tile(Q_W), tile(RNN_W), tile(RNN_W), tile(RNN_W), tile(G_G),
                  pl.BlockSpec((None, None, 1, N_MOD * D), lambda i, t: (i, t // ct, 0, 0)),
                  full((3, CONV_W)), full((CONV_W, D)), full((Q_W, D)), full((RNN_W, D)), full((D, D)),
                  full((1, D)), full((N_EXPERTS, D)), full((2 * TM, TM))],
        out_specs=[tile(D), pl.BlockSpec((None, TM * SUBLANES, LANES), lambda i, t: (i, t, 0)),
                   pl.BlockSpec((None, N_EXPERTS, TM), lambda i, t: (i, 0, t))],
        compiler_params=_cparams("arbitrary", "arbitrary"), name="merge",
    )(*x_args, za, za, za, att, hf, hb, zry, zg, modsel, conv_a, wc, wa, wr, wo, norm2, wrt, _row_shift_operator())


ROUTE_QW = 64
ROUTE_TOK_SHIFT = 6
CUM_BLK = 256


def _cumsum_lanes(x, n):
    blk = min(CUM_BLK, n)
    tri = (lax.broadcasted_iota(i32, (blk, blk), 0) <= lax.broadcasted_iota(i32, (blk, blk), 1)).astype(bf16)
    carry = jnp.zeros((x.shape[0], 1), f32)
    outs = []
    for j in range(n // blk):
        c = jnp.dot(x[:, j * blk:(j + 1) * blk].astype(bf16), tri, preferred_element_type=f32) + carry
        outs.append(c)
        carry = c[:, blk - 1:blk]
    return jnp.concatenate(outs, axis=1)


def _route_kernel(aff_ref, idx_ref, val_ref, pos_s, *, n, cap):
    aff = aff_ref[...]
    n_rows = aff.shape[0]
    thr = jnp.zeros((n_rows, 1), i32)
    for bit in range(30, -1, -1):
        cand = thr | (1 << bit)
        cnt = jnp.sum(jnp.where(aff >= lax.bitcast_convert_type(cand, f32), 1.0, 0.0), axis=1, keepdims=True)
        thr = jnp.where(cnt >= float(cap), cand, thr)
    thr_f = lax.bitcast_convert_type(thr, f32)
    gt = aff > thr_f
    eq = aff == thr_f
    need = float(cap) - jnp.sum(jnp.where(gt, 1.0, 0.0), axis=1, keepdims=True)
    sel = gt | (eq & (_cumsum_lanes(eq.astype(f32), n) <= need))
    pos_s[...] = jnp.where(sel, _cumsum_lanes(sel.astype(f32), n).astype(i32) - 1, -1)

    qw = min(ROUTE_QW, cap)
    shift = qw.bit_length() - 1
    tok = lax.broadcasted_iota(i32, (1, n), 1)
    tok_hi = (tok >> ROUTE_TOK_SHIFT).astype(f32)
    tok_lo = (tok & ((1 << ROUTE_TOK_SHIFT) - 1)).astype(f32)
    hi_iota = lax.broadcasted_iota(i32, (SUBLANES, n), 0)
    lo_iota = lax.broadcasted_iota(i32, (qw, n), 0)

    def body(e, carry):
        pos = pos_s[pl.ds(e, 1), :]
        a0 = aff_ref[pl.ds(e, 1), :]
        a_h = a0.astype(bf16).astype(f32)
        a_m = (a0 - a_h).astype(bf16).astype(f32)
        a_l = ((a0 - a_h) - a_m).astype(bf16).astype(f32)
        in_hi = (pos >> shift) == hi_iota
        lhs = jnp.concatenate([jnp.where(in_hi, r, 0.0) for r in (tok_hi, tok_lo, a_h, a_m, a_l)], axis=0)
        onehot = jnp.where((pos & (qw - 1)) == lo_iota, 1.0, 0.0).astype(bf16)
        res = lax.dot_general(lhs.astype(bf16), onehot, (((1,), (1,)), ((), ())), preferred_element_type=f32)
        r_hi, r_lo, v_h, v_m, v_l = (res[k * SUBLANES:(k + 1) * SUBLANES] for k in range(5))
        idx_ref[e] = (r_hi * float(1 << ROUTE_TOK_SHIFT) + r_lo).astype(i32)[:cap // qw]
        val_ref[e] = ((v_h + v_m) + v_l)[:cap // qw]
        return carry

    lax.fori_loop(0, n_rows, body, 0)


def _route(aff_t, n, cap, lane_block):
    b = aff_t.shape[0]
    qw = min(ROUTE_QW, cap)
    assert cap % qw == 0 and cap // qw <= SUBLANES and qw & (qw - 1) == 0
    rows = b * N_EXPERTS
    idx, val = pl.pallas_call(
        functools.partial(_route_kernel, n=n, cap=cap),
        out_shape=[jax.ShapeDtypeStruct((rows, cap // qw, qw), i32),
                   jax.ShapeDtypeStruct((rows, cap // qw, qw), f32)],
        grid=(1,),
        in_specs=[pl.BlockSpec((rows, n), lambda i: (0, lane_block))],
        out_specs=[pl.BlockSpec((rows, cap // qw, qw), lambda i: (0, 0, 0)),
                   pl.BlockSpec((rows, cap // qw, qw), lambda i: (0, 0, 0))],
        scratch_shapes=[pltpu.VMEM((rows, n), i32)],
        compiler_params=_cparams("arbitrary"), name="route",
    )(aff_t.reshape(rows, aff_t.shape[2]))
    return idx.reshape(b, N_EXPERTS, cap), val.reshape(b, N_EXPERTS, cap)


MOE_PAIR = 2


def _gather_kernel(idx_ref, h_ref, tok_ref, tbuf, *, cap):
    sp = cap + SUBLANES
    for k in range(MOE_PAIR):
        base = (pl.program_id(0) * N_EXPERTS + pl.program_id(1) * MOE_PAIR + k) * cap
        for p in range(cap):
            i = idx_ref[base + p]
            tbuf[pl.ds(p, SUBLANES, stride=sp), :] = h_ref[pl.ds(pl.multiple_of(i * SUBLANES, SUBLANES), SUBLANES), :]
        tok_ref[k] = jnp.concatenate([tbuf[pl.ds(c * sp, cap), :] for c in range(D // LANES)], axis=1).astype(bf16)


def _gather(idx, h2t, cap):
    b, rows, _ = h2t.shape
    return pl.pallas_call(
        functools.partial(_gather_kernel, cap=cap),
        out_shape=jax.ShapeDtypeStruct((b, N_EXPERTS, cap, D), bf16),
        grid_spec=pltpu.PrefetchScalarGridSpec(
            num_scalar_prefetch=1, grid=(b, N_EXPERTS // MOE_PAIR),
            in_specs=[pl.BlockSpec((None, rows, LANES), lambda i, e, idx: (i, 0, 0))],
            out_specs=pl.BlockSpec((None, MOE_PAIR, cap, D), lambda i, e, idx: (i, e, 0, 0)),
            scratch_shapes=[pltpu.VMEM((SUBLANES * (cap + SUBLANES), LANES), f32)]),
        compiler_params=_cparams("arbitrary", "arbitrary"), name="moe_gather",
    )(idx, h2t)


def _cast3_kernel(a_ref, b_ref, c_ref, ao_ref, bo_ref, co_ref):
    ao_ref[...] = a_ref[...].astype(bf16)
    bo_ref[...] = b_ref[...].astype(bf16)
    co_ref[...] = c_ref[...].astype(bf16)


def _expert0_bf16(wg, wu, wd, layer):
    spec = pl.BlockSpec((None, None, D, EXPERT_FF), lambda i: (layer, 0, 0, 0))
    ospec = pl.BlockSpec((D, EXPERT_FF), lambda i: (0, 0))
    return pl.pallas_call(
        _cast3_kernel, out_shape=[jax.ShapeDtypeStruct((D, EXPERT_FF), bf16)] * 3, grid=(1,),
        in_specs=[spec] * 3, out_specs=[ospec] * 3, compiler_params=_cparams("arbitrary"), name="expert0_cast",
    )(wg, wu, wd)


def _ffn_kernel(tok_ref, g0_ref, u0_ref, d0_ref, gq_ref, uq_ref, dq_ref, o_ref, w_even, w_odd):
    e = pl.program_id(0)
    i = pl.program_id(1)

    @pl.when((e == 0) & (i == 0))
    def _():
        w_even[0] = g0_ref[...]
        w_even[1] = u0_ref[...]
        w_even[2] = d0_ref[...]

    rows = gq_ref.shape[0]
    r0 = pl.multiple_of(i * rows, rows)

    def step(use, fill):
        n_s, cap, _ = tok_ref.shape
        t = tok_ref[...].reshape(n_s * cap, D)
        g = jnp.dot(t, use[0], preferred_element_type=f32)
        for k, q_ref in enumerate((gq_ref, uq_ref, dq_ref)):
            fill[k, pl.ds(r0, rows), :] = q_ref[...].astype(bf16)
        u = jnp.dot(t, use[1], preferred_element_type=f32)
        a = (g * _sigmoid(g) * u).astype(bf16)
        o_ref[...] = jnp.dot(a, use[2], preferred_element_type=f32).astype(bf16).reshape(n_s, cap, D)

    @pl.when(e % 2 == 0)
    def _():
        step(w_even, w_odd)

    @pl.when(e % 2 == 1)
    def _():
        step(w_odd, w_even)


def _ffn(tok, wg, wu, wd, layer):
    b, _, cap, _ = tok.shape
    n_s = MOE_PAIR if b % MOE_PAIR == 0 else 1
    steps = b // n_s
    assert D == EXPERT_FF and D % steps == 0 and (D // steps) % BF16_ROWS == 0
    rows = D // steps
    w0 = _expert0_bf16(wg, wu, wd, layer)
    w0spec = pl.BlockSpec((D, EXPERT_FF), lambda e, i: (0, 0), pipeline_mode=pl.Buffered(1))
    qspec = pl.BlockSpec((None, None, rows, D), lambda e, i: (layer, jnp.minimum(e + 1, N_EXPERTS - 1), i, 0))
    return pl.pallas_call(
        _ffn_kernel,
        out_shape=jax.ShapeDtypeStruct((b, N_EXPERTS, cap, D), bf16),
        grid=(N_EXPERTS, steps),
        in_specs=[pl.BlockSpec((n_s, None, cap, D), lambda e, i: (i, e, 0, 0)),
                  w0spec, w0spec, w0spec, qspec, qspec, qspec],
        out_specs=pl.BlockSpec((n_s, None, cap, D), lambda e, i: (i, e, 0, 0)),
        scratch_shapes=[pltpu.VMEM((3, D, EXPERT_FF), bf16), pltpu.VMEM((3, D, EXPERT_FF), bf16)],
        compiler_params=_cparams("arbitrary", "arbitrary"), name="moe_ffn",
    )(tok, *w0, wg, wu, wd)


SCATTER_UNROLL = 8


def _scatter_kernel(idx_ref, val_ref, o_ref, acc_ref, tbuf, *, cap):
    e = pl.program_id(1)
    sp = cap + SUBLANES

    @pl.when(e == 0)
    def _():
        acc_ref[...] = jnp.zeros_like(acc_ref)

    for k in range(MOE_PAIR):
        base = (pl.program_id(0) * N_EXPERTS + e * MOE_PAIR + k) * cap
        for c in range(D // LANES):
            tbuf[pl.ds(c * sp, cap), :] = o_ref[k, :, c * LANES:(c + 1) * LANES].astype(f32)
        u = min(SCATTER_UNROLL, cap)
        for p0 in range(0, cap, u):
            pend = []
            for p in range(p0, p0 + u):
                r0 = pl.multiple_of(idx_ref[base + p] * SUBLANES, SUBLANES)
                pend.append((r0, acc_ref[pl.ds(r0, SUBLANES), :]
                             + tbuf[pl.ds(p, SUBLANES, stride=sp), :] * val_ref[base + p]))
            for r0, v in pend:
                acc_ref[pl.ds(r0, SUBLANES), :] = v


def _scatter(idx, val, o, rows):
    b, _, cap, _ = o.shape
    return pl.pallas_call(
        functools.partial(_scatter_kernel, cap=cap),
        out_shape=jax.ShapeDtypeStruct((b, rows, LANES), f32),
        grid_spec=pltpu.PrefetchScalarGridSpec(
            num_scalar_prefetch=2, grid=(b, N_EXPERTS // MOE_PAIR),
            in_specs=[pl.BlockSpec((None, MOE_PAIR, cap, D), lambda i, e, idx, val: (i, e, 0, 0))],
            out_specs=pl.BlockSpec((None, rows, LANES), lambda i, e, idx, val: (i, 0, 0)),
            scratch_shapes=[pltpu.VMEM((SUBLANES * (cap + SUBLANES), LANES), f32)]),
        compiler_params=_cparams("arbitrary", "arbitrary"), name="moe_scatter",
    )(idx, val, o)


def _moe(aff_t, h2t, wg, wu, wd, layer, n_lat, n_ctx):
    idx, val = _route(aff_t, n_lat, CAP_FACTOR * n_lat // N_EXPERTS, 0)
    if n_ctx:
        idx_c, val_c = _route(aff_t, n_ctx, CAP_FACTOR * n_ctx // N_EXPERTS, n_lat // n_ctx)
        idx = jnp.concatenate([idx, idx_c + n_lat], axis=-1)
        val = jnp.concatenate([val, val_c], axis=-1)
    cap = idx.shape[-1]
    idx = idx.reshape(-1)
    val = val.reshape(-1)
    tok = _gather(idx, h2t, cap)
    o = _ffn(tok, wg, wu, wd, layer)
    return _scatter(idx, val, o, h2t.shape[1])


def _final_kernel(x_ref, acc_ref, mod_ref, n_ref, o_ref):
    x = x_ref[...] + mod_ref[:, 5 * D:6 * D] * _tiles_to_rows(acc_ref, x_ref.shape[0])
    o_ref[...] = (x * lax.rsqrt(jnp.mean(x * x, axis=-1, keepdims=True) + EPS)) * n_ref[...]


def _final(xs, acc, modsel, norm, n_lat):
    b = xs.shape[0]
    tm = 2 * TM if n_lat % (2 * TM) == 0 else TM
    return pl.pallas_call(
        _final_kernel,
        out_shape=jax.ShapeDtypeStruct((b, n_lat, D), f32),
        grid=(b, n_lat // tm),
        in_specs=[pl.BlockSpec((None, tm, D), lambda i, t: (i, t, 0)),
                  pl.BlockSpec((None, tm * SUBLANES, LANES), lambda i, t: (i, t, 0)),
                  pl.BlockSpec((None, None, 1, N_MOD * D), lambda i, t: (i, 0, 0, 0)),
                  pl.BlockSpec((1, D), lambda i, t: (0, 0))],
        out_specs=pl.BlockSpec((None, tm, D), lambda i, t: (i, t, 0)),
        compiler_params=_cparams("arbitrary", "arbitrary"), name="final_norm",
    )(xs, acc, modsel, norm)


def _rope_tables(n_lat, n_ctx):
    pos = jnp.arange(n_lat, dtype=f32)
    rows = jnp.floor(pos / GRID_W)
    cols = pos - rows * GRID_W
    half = HEAD_DIM // 4
    inv = 1.0 / (ROPE_BASE ** (jnp.arange(0, 2 * half, 2, dtype=f32) / (2 * half)))
    lane = jnp.arange(LANES)
    d = lane % HEAD_DIM
    ang = jnp.where((d < 2 * half)[None, :], rows[:, None], cols[:, None]) * inv[d % half][None, :]
    first = ((d % (2 * half)) < half)[None, :]
    cos = jnp.cos(ang)
    sin = jnp.sin(ang)
    s1 = jnp.where(first, -sin, 0.0)
    s2 = jnp.where(first, 0.0, sin)
    pad = lambda t, v: jnp.concatenate([t, jnp.full((n_ctx, LANES), v, f32)], axis=0)
    return pad(cos, 1.0), pad(s1, 0.0), pad(s2, 0.0)


CAST_COLS = 256


def _cast_cols_kernel(w_ref, o_ref, *, first_half_block):
    scale = jnp.where(pl.program_id(0) >= first_half_block, 0.5, 1.0)
    o_ref[...] = (w_ref[...] * scale).astype(bf16)


def _cast_cols(w_stack, layer, col0, n_cols, first_half_col):
    assert col0 % CAST_COLS == 0 and n_cols % CAST_COLS == 0 and first_half_col % CAST_COLS == 0
    rows = w_stack.shape[1]
    return pl.pallas_call(
        functools.partial(_cast_cols_kernel, first_half_block=(first_half_col - col0) // CAST_COLS),
        out_shape=jax.ShapeDtypeStruct((rows, n_cols), bf16), grid=(n_cols // CAST_COLS,),
        in_specs=[pl.BlockSpec((None, rows, CAST_COLS), lambda j: (layer, 0, j + col0 // CAST_COLS))],
        out_specs=pl.BlockSpec((rows, CAST_COLS), lambda j: (0, j)),
        compiler_params=_cparams("arbitrary"), name="cast_cols",
    )(w_stack)


def _in_weights_bf16(w_in, layer):
    return (_cast_cols(w_in, layer, 0, W_A_COLS, IN_TOTAL),
            _cast_cols(w_in, layer, W_A_COLS, IN_TOTAL - W_A_COLS, IN_TOTAL - G_G))


def kernel(x, c, ctx, c_ctx, norm1, norm2, w_mod, b_mod, w_in, conv_a, w_conv_out, attn_sink, w_attn_out, rnn_conv_w,
           rnn_conv_b, rnn_w_a, rnn_b_a, rnn_w_x, rnn_b_x, rnn_lam, w_rnn_out, w_o, w_router, w_e_gate, w_e_up,
           w_e_down, final_norm):
    b, n_lat, d = x.shape
    n_ctx = ctx.shape[1]
    depth = w_mod.shape[0]
    assert d == D and n_ctx == TM and n_lat % TM == 0 and b < SUBLANES
    nt = n_lat + n_ctx
    ct = nt // TM - 1

    cstack = jnp.concatenate([c, c_ctx[None, :], jnp.zeros((SUBLANES - b - 1, D), f32)], axis=0)
    mod = _modulation(cstack, w_mod, b_mod)
    tabs = _rope_tables(n_lat, n_ctx)

    xs = x
    acc = None
    modsel_prev = None
    for l in range(depth):
        ctx_out = l < depth - 1
        ctx_sep = ctx if l == 0 else None
        modsel = jnp.stack([mod[l, :b], jnp.broadcast_to(mod[l, b], (b, N_MOD * D))], axis=1)[:, :, None, :]
        rnn_prm = []
        for dr in range(2):
            wg = (0.5 * jnp.concatenate([rnn_w_a[l, dr], rnn_w_x[l, dr]], axis=-1)).astype(bf16)
            rnn_prm.append((rnn_conv_w[l, dr], rnn_conv_b[l, dr][None, :], wg, rnn_b_a[l, dr][None, :],
                            rnn_b_x[l, dr][None, :], rnn_lam[l, dr][None, :]))
        w_a, w_b = _in_weights_bf16(w_in, l)
        outs = _proj_a(xs, ctx_sep, acc, modsel_prev, modsel, norm1[l][None, :], tabs, w_a, rnn_prm[0])
        if acc is not None:
            xs, outs = outs[0], outs[1:]
        za, zq, zkv, zrx, hn, hf = outs
        zry, zg, hb = _proj_b(hn, w_b, zrx, rnn_prm[1])
        att = _attention(attn_sink[l][None, :], zq, zkv, n_lat, (nt if ctx_out else n_lat) // QB)
        xs, h2t, aff_t = _merge(xs, ctx_sep, za, att, hf, hb, zry, zg, modsel, conv_a[l],
                                w_conv_out[l].astype(bf16),
                                w_attn_out[l].astype(bf16), (0.5 * w_rnn_out[l]).astype(bf16),
                                (0.5 * w_o[l]).astype(bf16),
                                norm2[l][None, :], w_router[l].T.astype(bf16), (ct + 1) if ctx_out else ct)
        acc = _moe(aff_t, h2t, w_e_gate, w_e_up, w_e_down, l, n_lat, n_ctx if ctx_out else 0)
        modsel_prev = modsel
    return _final(xs, acc, modsel_prev, final_norm[None, :], n_lat)
```

```python
import functools

import jax
import jax.numpy as jnp
from jax import lax
from jax.experimental import pallas as pl
from jax.experimental.pallas import tpu as pltpu

f32 = jnp.float32
bf16 = jnp.bfloat16
i32 = jnp.int32

D = 1024
EPS = 1e-6
CONV_W = 512
N_HEADS = 8
N_KV = 2
HEAD_DIM = 64
Q_W = N_HEADS * HEAD_DIM
KV_W = N_KV * HEAD_DIM
WINDOW = 128
GRID_W = 64
ROPE_BASE = 10000.0
NEG_INF = -1e30
RNN_W = 1024
RNN_BLOCKS = 8
RNN_BLK = RNN_W // RNN_BLOCKS
RNN_CONV_K = 4
RG_C = 8.0
N_EXPERTS = 16
EXPERT_FF = 1024
CAP_FACTOR = 2
N_MOD = 6

LANES = 128
SUBLANES = 8
BF16_ROWS = 16
VMEM_LIMIT = 56 * 1024 * 1024

TM = 256
QB = 128
K2_W = 2 * KV_W

G_A = 3 * CONV_W
G_QKV = Q_W + 2 * KV_W
G_RX = RNN_W
G_RY = RNN_W
G_G = 3 * D
IN_COLS = (G_A, G_QKV, G_RX, G_RY, G_G)
IN_TOTAL = sum(IN_COLS)
KV2_W = 2 * K2_W
ZA_W = 2 * CONV_W
OUT_COLS = (ZA_W, Q_W, KV2_W, G_RX, G_RY, G_G)


def _cparams(*sem):
    return pltpu.CompilerParams(dimension_semantics=sem, vmem_limit_bytes=VMEM_LIMIT)


def _sigmoid(x):
    return 0.5 * jnp.tanh(0.5 * x) + 0.5


def _rms_mod(x, g, sc, sh):
    y = x * lax.rsqrt(jnp.mean(x * x, axis=-1, keepdims=True) + EPS)
    return (y * g) * (1.0 + sc) + sh


def _mod_kernel(c_ref, w_ref, b_ref, o_ref):
    c = c_ref[...]
    s = (c * _sigmoid(c)).astype(bf16)
    o_ref[...] = jnp.dot(s, w_ref[...].astype(bf16), preferred_element_type=f32) + b_ref[...]


def _modulation(cstack, w_mod, b_mod):
    depth = w_mod.shape[0]
    tn = 1536
    return pl.pallas_call(
        _mod_kernel,
        out_shape=jax.ShapeDtypeStruct((depth, SUBLANES, N_MOD * D), f32),
        grid=(depth, N_MOD * D // tn),
        in_specs=[
            pl.BlockSpec((SUBLANES, D), lambda l, j: (0, 0)),
            pl.BlockSpec((None, D, tn), lambda l, j: (l, 0, j)),
            pl.BlockSpec((None, 1, tn), lambda l, j: (l, 0, j)),
        ],
        out_specs=pl.BlockSpec((None, SUBLANES, tn), lambda l, j: (l, 0, j)),
        compiler_params=_cparams("arbitrary", "arbitrary"),
        name="modulation",
    )(cstack, w_mod, b_mod.reshape(depth, 1, N_MOD * D))


def _tiles_to_rows(ref, rows):
    return jnp.concatenate([ref[pl.ds(c, rows, stride=SUBLANES), :] for c in range(D // LANES)], axis=1)


def _rows_to_tiles(ref, val, rows):
    per = D // LANES
    for r in range(rows // SUBLANES):
        for c in range(per):
            ref[pl.ds(r * SUBLANES * per + c, SUBLANES, stride=per), :] = val[r * SUBLANES:(r + 1) * SUBLANES,
                                                                              c * LANES:(c + 1) * LANES]


def _rope(z, cos, s1, s2):
    outs = []
    for c in range(z.shape[1] // LANES):
        x = z[:, c * LANES:(c + 1) * LANES]
        outs.append(x * cos + pltpu.roll(x, LANES - 16, axis=1) * s1 + pltpu.roll(x, 16, axis=1) * s2)
    return jnp.concatenate(outs, axis=1)


W_A_COLS = G_A + G_QKV + G_RX


def _proj_a_kernel(*refs, with_moe, ct):
    if with_moe:
        (x_ref, acc_ref, modp_ref), refs = refs[:3], refs[3:]
    else:
        (x_ref, ctx_ref), refs = refs[:2], refs[2:]
    mod_ref, n_ref, cos_ref, s1_ref, s2_ref, w_ref = refs[:6]
    prm, refs = refs[6:12], refs[12:]
    if with_moe:
        xo_ref, refs = refs[0], refs[1:]
    za_ref, zq_ref, zkv_ref, zrx_ref, h_ref, hf_ref, ubuf, obuf, hcar, halo_s = refs
    s = pl.program_id(1)
    chunk = _rnn_chunk(s, ct, False)

    @pl.when(s == 0)
    def _():
        hcar[...] = jnp.zeros_like(hcar)
        halo_s[...] = jnp.zeros_like(halo_s)

    if with_moe:
        x = x_ref[...] + modp_ref[:, 5 * D:6 * D] * _tiles_to_rows(acc_ref, TM)
        xo_ref[...] = x
    else:
        x = jnp.where(chunk == ct, ctx_ref[...], x_ref[...])
    h = _rms_mod(x, n_ref[...], mod_ref[:, D:2 * D], mod_ref[:, 0:D]).astype(bf16)
    h_ref[...] = h
    hal = jnp.where((chunk != 0) & (chunk != ct), halo_s[...], 0.0)
    quarter = G_RX // 4

    def scan_input(p):
        cols = slice(p * quarter, (p + 1) * quarter)
        u = jnp.dot(h, w_ref[:, G_A + G_QKV + p * quarter:G_A + G_QKV + (p + 1) * quarter],
                    preferred_element_type=f32)
        zrx_ref[:, cols] = u.astype(bf16)
        _rnn_fill(u, hal[:, cols], ubuf, False, c0=p * quarter // LANES)
        halo_s[:, cols] = u[TM - BF16_ROWS:, :]

    scan = lambda c: _rnn_tile(c, prm, ubuf, obuf, hcar, False)
    cw = CONV_W
    scan_input(0)
    a_x = jnp.dot(h, w_ref[:, 0:cw], preferred_element_type=f32)
    scan(0)
    scan_input(1)
    scan(1)
    a_b = jnp.dot(h, w_ref[:, cw:2 * cw], preferred_element_type=f32)
    scan(2)
    scan_input(2)
    scan(3)
    a_c = jnp.dot(h, w_ref[:, 2 * cw:G_A], preferred_element_type=f32)
    za_ref[...] = jnp.concatenate([a_c * a_x, a_b], axis=1).astype(bf16)
    scan(4)
    scan_input(3)
    scan(5)
    z = jnp.dot(h, w_ref[:, G_A:G_A + G_QKV], preferred_element_type=f32)
    qk = _rope(z[:, :Q_W + KV_W], cos_ref[...], s1_ref[...], s2_ref[...])
    zq_ref[...] = (qk[:, :Q_W] * (HEAD_DIM ** -0.5)).astype(bf16)
    lo = lax.broadcasted_iota(i32, (TM, LANES), 1) < HEAD_DIM
    dup = []
    for pair in (qk[:, Q_W:], z[:, Q_W + KV_W:]):
        swapped = pltpu.roll(pair, HEAD_DIM, axis=1)
        dup += [jnp.where(lo, pair, swapped), jnp.where(lo, swapped, pair)]
    zkv_ref[...] = jnp.concatenate(dup, axis=1).astype(bf16)
    scan(6)
    scan(7)
    hf_ref[...] = _rnn_states(obuf).astype(bf16)


def _proj_b_kernel(h_ref, w_ref, u_ref, halo_ref, *refs, ct):
    prm, (zry_ref, zg_ref, hb_ref, ubuf, obuf, hcar) = refs[:6], refs[6:]
    s = pl.program_id(1)
    chunk = _rnn_chunk(s, ct, True)

    @pl.when(s == 0)
    def _():
        hcar[...] = jnp.zeros_like(hcar)

    h = h_ref[...]
    hal = jnp.where((chunk != ct - 1) & (chunk != ct), halo_ref[...].astype(f32), 0.0)
    _rnn_fill(u_ref[...].astype(f32), hal, ubuf, True)
    slab = (G_RY + G_G) // RNN_LT
    for c in range(RNN_LT):
        z = jnp.dot(h, w_ref[:, c * slab:(c + 1) * slab], preferred_element_type=f32)
        if (c + 1) * slab <= G_RY:
            t_y = jnp.tanh(z * (0.7978845608028654 + (0.7978845608028654 * 0.044715) * (z * z)))
            zry_ref[:, c * slab:(c + 1) * slab] = (z * t_y + z).astype(bf16)
        else:
            zg_ref[:, c * slab - G_RY:(c + 1) * slab - G_RY] = z.astype(bf16)
        _rnn_tile(c, prm, ubuf, obuf, hcar, True)
    hb_ref[...] = _rnn_states(obuf).astype(bf16)


def _proj_a(xs, ctx, acc, modsel_prev, modsel, norm, tabs, w_a, rnn_prm):
    with_moe = acc is not None
    b = xs.shape[0]
    nt = xs.shape[1] + (0 if with_moe else ctx.shape[1])
    n_tiles = nt // TM
    ct = n_tiles - 1
    ch = lambda s: _rnn_chunk(s, ct, False)
    tile = lambda w: pl.BlockSpec((None, TM, w), lambda i, s: (i, ch(s), 0))
    mod_spec = pl.BlockSpec((None, None, 1, N_MOD * D), lambda i, s: (i, ch(s) // ct, 0, 0))
    tab_spec = pl.BlockSpec((TM, LANES), lambda i, s: (ch(s), 0))
    if with_moe:
        in_specs = [tile(D), pl.BlockSpec((None, TM * SUBLANES, LANES), lambda i, s: (i, ch(s), 0)), mod_spec]
        args = [xs, acc, modsel_prev]
    else:
        in_specs = [pl.BlockSpec((None, TM, D), lambda i, s: (i, jnp.minimum(ch(s), ct - 1), 0)),
                    pl.BlockSpec((None, TM, D), lambda i, s: (i, 0, 0))]
        args = [xs, ctx]
    in_specs += [mod_spec, pl.BlockSpec((1, D), lambda i, s: (0, 0)), tab_spec, tab_spec, tab_spec,
                 pl.BlockSpec((D, W_A_COLS), lambda i, s: (0, 0), pipeline_mode=pl.Buffered(1))]
    in_specs += _rnn_param_specs()
    args += [modsel, norm, *tabs, w_a, *rnn_prm]
    widths = (ZA_W, Q_W, KV2_W, G_RX, D, RNN_W)
    out_shape = [jax.ShapeDtypeStruct((b, nt, w), bf16) for w in widths]
    out_specs = [tile(w) for w in widths]
    if with_moe:
        out_shape = [jax.ShapeDtypeStruct((b, nt, D), f32)] + out_shape
        out_specs = [tile(D)] + out_specs
    return pl.pallas_call(
        functools.partial(_proj_a_kernel, with_moe=with_moe, ct=ct),
        out_shape=out_shape, grid=(b, n_tiles), in_specs=in_specs, out_specs=out_specs,
        scratch_shapes=_RNN_SCRATCH + [pltpu.VMEM((BF16_ROWS, RNN_W), f32)],
        compiler_params=_cparams("arbitrary", "arbitrary"), name="proj_a_rnn_fwd",
    )(*args)


def _proj_b(h, w_b, zrx, rnn_prm):
    b, nt, _ = h.shape
    n_tiles = nt // TM
    ct = n_tiles - 1
    per = TM // BF16_ROWS
    ch = lambda s: _rnn_chunk(s, ct, True)
    tile = lambda w: pl.BlockSpec((None, TM, w), lambda i, s: (i, ch(s), 0))
    halo = pl.BlockSpec((None, BF16_ROWS, RNN_W),
                        lambda i, s: (i, jnp.minimum((ch(s) + 1) * per, nt // BF16_ROWS - 1), 0))
    widths = (G_RY, G_G, RNN_W)
    return pl.pallas_call(
        functools.partial(_proj_b_kernel, ct=ct),
        out_shape=[jax.ShapeDtypeStruct((b, nt, w), bf16) for w in widths],
        grid=(b, n_tiles),
        in_specs=[tile(D), pl.BlockSpec((D, G_RY + G_G), lambda i, s: (0, 0), pipeline_mode=pl.Buffered(1)),
                  tile(RNN_W), halo] + _rnn_param_specs(),
        out_specs=[tile(w) for w in widths],
        scratch_shapes=_RNN_SCRATCH,
        compiler_params=_cparams("arbitrary", "arbitrary"), name="proj_b_rnn_bwd",
    )(h, w_b, zrx, zrx, *rnn_prm)


ATT_QBLOCKS = 2


def _attn_kernel(sink_ref, q_ref, kvp_ref, kvc_ref, kvn_ref, kvx_ref, o_ref, *, n_lat_blocks, n_ctx):
    group = N_HEADS // N_KV
    nlb = n_lat_blocks
    lo = lax.broadcasted_iota(i32, (QB, LANES), 1) < HEAD_DIM
    diff = lax.broadcasted_iota(i32, (QB, QB), 1) - lax.broadcasted_iota(i32, (QB, QB), 0)
    zero = jnp.zeros((QB, LANES), bf16)
    def score(sub, g):
        window = (kvp_ref, kvc_ref) if sub == 0 else (kvc_ref, kvn_ref)
        ksl = slice(g * LANES, (g + 1) * LANES)
        k = jnp.concatenate([r[:, ksl] for r in window] + [kvx_ref[:, ksl]], axis=0)
        parts = []
        for pr in range(group // 2):
            qp = q_ref[sub * QB:(sub + 1) * QB, (g * (group // 2) + pr) * LANES:(g * (group // 2) + pr + 1) * LANES]
            parts += [jnp.where(lo, qp, zero), jnp.where(lo, zero, qp)]
        return lax.dot_general(jnp.concatenate(parts, axis=0), k, (((1,), (1,)), ((), ())),
                               preferred_element_type=f32)

    def finish(sub, g, s):
        j = pl.program_id(1) * ATT_QBLOCKS + sub
        lat = j < nlb
        ok = (diff >= jnp.where(lat & (j >= 1), 0, QB),
              diff >= jnp.where(lat, -QB, QB),
              -diff >= jnp.where(lat & (j <= nlb - 2), 0, QB))
        rows = slice(sub * QB, (sub + 1) * QB)
        window = (kvp_ref, kvc_ref) if sub == 0 else (kvc_ref, kvn_ref)
        vsl = slice(K2_W + g * LANES, K2_W + (g + 1) * LANES)
        v = jnp.concatenate([r[:, vsl] for r in window] + [kvx_ref[:, vsl]], axis=0)
        ps, inv = [], []
        for hh in range(group):
            sh = s[hh * QB:(hh + 1) * QB]
            sm = jnp.concatenate([jnp.where(ok[kb], sh[:, kb * QB:(kb + 1) * QB], NEG_INF) for kb in range(3)]
                                 + [sh[:, 3 * QB:]], axis=1)
            sink = sink_ref[0, g * group + hh]
            m = jnp.maximum(jnp.max(sm, axis=1, keepdims=True), sink)
            p = jnp.exp(sm - m)
            inv.append(1.0 / (jnp.sum(p, axis=1, keepdims=True) + jnp.exp(sink - m)))
            ps.append(p.astype(bf16))
        o = jnp.dot(jnp.concatenate(ps, axis=0), v, preferred_element_type=f32) * jnp.concatenate(inv, axis=0)
        for pr in range(group // 2):
            pair = jnp.where(lo, o[(2 * pr) * QB:(2 * pr + 1) * QB], o[(2 * pr + 1) * QB:(2 * pr + 2) * QB])
            c0 = (g * (group // 2) + pr) * LANES
            o_ref[rows, c0:c0 + LANES] = pair.astype(bf16)

    units = [(sub, g) for sub in range(ATT_QBLOCKS) for g in range(N_KV)]
    pending = score(*units[0])
    for n, unit in enumerate(units):
        ahead = score(*units[n + 1]) if n + 1 < len(units) else None
        finish(*unit, pending)
        pending = ahead


def _attention(sink, zq, zkv, n_lat, n_qblocks):
    b, nt, _ = zq.shape
    n_ctx = nt - n_lat
    nlb = n_lat // QB
    cidx = n_lat // n_ctx
    nq = ATT_QBLOCKS
    assert nq == 2 and n_qblocks % nq == 0 and nlb % nq == 0
    edge = lambda d: pl.BlockSpec((None, QB, KV2_W), lambda i, j: (i, jnp.clip(nq * j + d, 0, nlb - 1), 0))
    ctx = pl.BlockSpec((None, n_ctx, KV2_W), lambda i, j: (i, cidx, 0))
    return pl.pallas_call(
        functools.partial(_attn_kernel, n_lat_blocks=nlb, n_ctx=n_ctx),
        out_shape=jax.ShapeDtypeStruct((b, n_qblocks * QB, Q_W), bf16),
        grid=(b, n_qblocks // nq),
        in_specs=[pl.BlockSpec(memory_space=pltpu.SMEM),
                  pl.BlockSpec((None, nq * QB, Q_W), lambda i, j: (i, j, 0)),
                  edge(-1), pl.BlockSpec((None, nq * QB, KV2_W), lambda i, j: (i, j, 0)), edge(nq), ctx],
        out_specs=pl.BlockSpec((None, nq * QB, Q_W), lambda i, j: (i, j, 0)),
        compiler_params=_cparams("arbitrary", "arbitrary"), name="attention",
    )(sink, zq, zkv, zkv, zkv, zkv)


def _rnn_chunk(s, ct, rev):
    return jnp.where(s == 0, ct, (ct - s) if rev else (s - 1))


def _block_scan(a, b, h_in, rowi, rev):
    n = SUBLANES
    order = range(n - 1, -1, -1) if rev else range(n)
    hs = [None] * n
    ps = [None] * n
    prev = None
    for j in order:
        if prev is None:
            hs[j], ps[j] = b[j], a[j]
        else:
            hs[j], ps[j] = a[j] * hs[prev] + b[j], a[j] * ps[prev]
        prev = j
    he, pe = hs[prev], ps[prev]
    for sh in (1, 2, 4):
        rs = n - sh if rev else sh
        msk = (rowi < n - sh) if rev else (rowi >= sh)
        he, pe = (jnp.where(msk, he + pe * pltpu.roll(he, rs, axis=0), he),
                  jnp.where(msk, pe * pltpu.roll(pe, rs, axis=0), pe))
    e = he + pe * h_in
    carry = jnp.where((rowi == n - 1) if rev else (rowi == 0), h_in, pltpu.roll(e, n - 1 if rev else 1, axis=0))
    out = [hs[j] + ps[j] * carry for j in range(n)]
    last = e[0:1, :] if rev else e[n - 1:n, :]
    return out, jnp.broadcast_to(last, (n, LANES))


RNN_LT = RNN_W // LANES


def _rnn_fill(u, hal, ubuf, rev, c0=0):
    for k in range(u.shape[1] // LANES):
        sl = slice(k * LANES, (k + 1) * LANES)
        if rev:
            ubuf[c0 + k, 0:TM, :] = u[:, sl]
            ubuf[c0 + k, TM:TM + BF16_ROWS, :] = hal[:, sl]
        else:
            ubuf[c0 + k, 0:BF16_ROWS, :] = hal[:, sl]
            ubuf[c0 + k, BF16_ROWS:BF16_ROWS + TM, :] = u[:, sl]


def _rnn_tile(c, prm, ubuf, obuf, hcar, rev):
    cw_ref, cb_ref, wg_ref, ba_ref, bx_ref, lam_ref = prm
    base = 0 if rev else BF16_ROWS - (RNN_CONV_K - 1)
    rowi = lax.broadcasted_iota(i32, (SUBLANES, LANES), 0)
    blk_rows = SUBLANES * SUBLANES
    n_blk = TM // blk_rows
    sl = slice(c * LANES, (c + 1) * LANES)
    taps = [cw_ref[k:k + 1, sl] for k in range(RNN_CONV_K)]
    xt = []
    for blk in range(n_blk):
        for j in range(SUBLANES):
            r0 = base + blk * blk_rows + j
            acc = cb_ref[:, sl] + ubuf[c, pl.ds(r0, SUBLANES, stride=SUBLANES), :] * taps[0]
            for k in range(1, RNN_CONV_K):
                acc = acc + ubuf[c, pl.ds(r0 + k, SUBLANES, stride=SUBLANES), :] * taps[k]
            xt.append(acc)
    xp = jnp.concatenate(xt, axis=0)
    g = jnp.dot(xp.astype(bf16), wg_ref[c], preferred_element_type=f32)
    t_r = jnp.tanh(g[:, :RNN_BLK] + 0.5 * ba_ref[:, sl])
    t_i = jnp.tanh(g[:, RNN_BLK:] + 0.5 * bx_ref[:, sl])
    c4 = (-0.5 * RG_C) * jnp.logaddexp(-lam_ref[:, sl], 0.0)
    log_a = c4 * t_r + c4
    a = jnp.exp(log_a)
    y = -jnp.tanh(log_a) * (a * a + 1.0)
    xh = 0.5 * xp
    bb = jnp.where(y > 0.0, y * lax.rsqrt(y), 0.0) * (xh * t_i + xh)
    h_in = hcar[:, sl]
    for blk in (range(n_blk - 1, -1, -1) if rev else range(n_blk)):
        rows = lambda j: slice((blk * SUBLANES + j) * SUBLANES, (blk * SUBLANES + j + 1) * SUBLANES)
        hs, h_in = _block_scan([a[rows(j)] for j in range(SUBLANES)], [bb[rows(j)] for j in range(SUBLANES)],
                               h_in, rowi, rev)
        for j in range(SUBLANES):
            obuf[c, pl.ds(blk * blk_rows + j, SUBLANES, stride=SUBLANES), :] = hs[j]
    hcar[:, sl] = h_in


def _rnn_states(obuf):
    return jnp.concatenate([obuf[c] for c in range(RNN_LT)], axis=1)


_RNN_SCRATCH = [pltpu.VMEM((RNN_LT, TM + BF16_ROWS, LANES), f32),
                pltpu.VMEM((RNN_LT, TM, LANES), f32),
                pltpu.VMEM((SUBLANES, RNN_W), f32)]


def _rnn_param_specs():
    full = lambda shape: pl.BlockSpec(shape, lambda i, s: (0,) * len(shape))
    return [full((RNN_CONV_K, RNN_W)), full((1, RNN_W)), full((RNN_BLOCKS, RNN_BLK, 2 * RNN_BLK)),
            full((1, RNN_W)), full((1, RNN_W)), full((1, RNN_W))]


def _merge_kernel(*refs, ct, split):
    (za_ref, zap_ref, zan_ref, att_ref, hf_ref, hb_ref, ry_ref, zg_ref, mod_ref, ca_ref, wc_ref,
     wa_ref, wr_ref, wo_ref, n2_ref, wrt_ref, shift_ref, xo_ref, h2_ref, aff_ref) = refs[2 if split else 1:]
    t = pl.program_id(1)
    if split:
        x_in = jnp.where(t == ct, refs[1][...], refs[0][...])
    else:
        x_in = refs[0][...]
    cw = CONV_W
    cu_b = za_ref[:, 0:cw]
    sh = jnp.dot(shift_ref[...], cu_b, preferred_element_type=f32)
    prev_ok = (t != 0) & (t != ct)
    next_ok = (t != ct - 1) & (t != ct)
    cu_p = jnp.where(prev_ok, zap_ref[BF16_ROWS - 1:BF16_ROWS, 0:cw].astype(f32), 0.0)
    cu_n = jnp.where(next_ok, zan_ref[0:1, 0:cw].astype(f32), 0.0)
    rowi = lax.broadcasted_iota(i32, (TM, cw), 0)
    cu_prev = jnp.where(rowi == 0, cu_p, sh[:TM])
    cu_next = jnp.where(rowi == TM - 1, cu_n, sh[TM:])
    y = cu_prev * ca_ref[0:1, :] + cu_b.astype(f32) * ca_ref[1:2, :] + cu_next * ca_ref[2:3, :]
    cnv = jnp.dot((za_ref[:, cw:2 * cw].astype(f32) * y).astype(bf16), wc_ref[...], preferred_element_type=f32)
    att = jnp.dot(att_ref[...], wa_ref[...], preferred_element_type=f32)
    rec = hf_ref[...].astype(f32) + hb_ref[...].astype(f32)
    rnn = jnp.dot((ry_ref[...].astype(f32) * rec).astype(bf16), wr_ref[...], preferred_element_type=f32)
    mix2 = None
    for k, br in enumerate((cnv, att, rnn)):
        term = jnp.tanh(zg_ref[:, k * D:(k + 1) * D].astype(f32)) * br + br
        mix2 = term if mix2 is None else mix2 + term
    x = x_in + mod_ref[:, 2 * D:3 * D] * jnp.dot(mix2.astype(bf16), wo_ref[...], preferred_element_type=f32)
    xo_ref[...] = x
    h2 = _rms_mod(x, n2_ref[...], mod_ref[:, 4 * D:5 * D], mod_ref[:, 3 * D:4 * D])
    _rows_to_tiles(h2_ref, h2, TM)
    logits = lax.dot_general(wrt_ref[...], h2.astype(bf16), (((1,), (1,)), ((), ())), preferred_element_type=f32)
    e = jnp.exp(logits - jnp.max(logits, axis=0, keepdims=True))
    aff_ref[...] = e / jnp.sum(e, axis=0, keepdims=True)


def _row_shift_operator():
    r = jnp.arange(2 * TM)[:, None]
    c = jnp.arange(TM)[None, :]
    return (c == jnp.where(r < TM, r - 1, r - TM + 1)).astype(bf16)


def _merge(xs, ctx, za, att, hf, hb, zry, zg, modsel, conv_a, wc, wa, wr, wo, norm2, wrt, n_tiles_eff):
    b, nt, _ = za.shape
    ct = nt // TM - 1
    per = TM // BF16_ROWS
    split = ctx is not None
    tile = lambda w: pl.BlockSpec((None, TM, w), lambda i, t: (i, t, 0))
    full = lambda shape: pl.BlockSpec(shape, lambda i, t: (0,) * len(shape), pipeline_mode=pl.Buffered(1))
    if split:
        x_specs = [pl.BlockSpec((None, TM, D), lambda i, t: (i, jnp.minimum(t, ct - 1), 0)),
                   pl.BlockSpec((None, TM, D), lambda i, t: (i, 0, 0))]
        x_args = [xs, ctx]
    else:
        x_specs = [tile(D)]
        x_args = [xs]
    return pl.pallas_call(
        functools.partial(_merge_kernel, ct=ct, split=split),
        out_shape=[jax.ShapeDtypeStruct((b, n_tiles_eff * TM, D), f32),
                   jax.ShapeDtypeStruct((b, n_tiles_eff * TM * SUBLANES, LANES), f32),
                   jax.ShapeDtypeStruct((b, N_EXPERTS, n_tiles_eff * TM), f32)],
        grid=(b, n_tiles_eff),
        in_specs=x_specs + [tile(ZA_W),
                  pl.BlockSpec((None, BF16_ROWS, ZA_W), lambda i, t: (i, jnp.maximum(t * per - 1, 0), 0)),
                  pl.BlockSpec((None, BF16_ROWS, ZA_W),
                               lambda i, t: (i, jnp.minimum((t + 1) * per, nt // BF16_ROWS - 1), 0)),
                  tile(Q_W), tile(RNN_W), tile(RNN_W), tile(RNN_W), tile(G_G),
                  pl.BlockSpec((None, None, 1, N_MOD * D), lambda i, t: (i, t // ct, 0, 0)),
                  full((3, CONV_W)), full((CONV_W, D)), full((Q_W, D)), full((RNN_W, D)), full((D, D)),
                  full((1, D)), full((N_EXPERTS, D)), full((2 * TM, TM))],
        out_specs=[tile(D), pl.BlockSpec((None, TM * SUBLANES, LANES), lambda i, t: (i, t, 0)),
                   pl.BlockSpec((None, N_EXPERTS, TM), lambda i, t: (i, 0, t))],
        compiler_params=_cparams("arbitrary", "arbitrary"), name="merge",
    )(*x_args, za, za, za, att, hf, hb, zry, zg, modsel, conv_a, wc, wa, wr, wo, norm2, wrt, _row_shift_operator())


ROUTE_QW = 64
ROUTE_TOK_SHIFT = 6
CUM_BLK = 256


def _cumsum_lanes(x, n):
    blk = min(CUM_BLK, n)
    tri = (lax.broadcasted_iota(i32, (blk, blk), 0) <= lax.broadcasted_iota(i32, (blk, blk), 1)).astype(bf16)
    carry = jnp.zeros((x.shape[0], 1), f32)
    outs = []
    for j in range(n // blk):
        c = jnp.dot(x[:, j * blk:(j + 1) * blk].astype(bf16), tri, preferred_element_type=f32) + carry
        outs.append(c)
        carry = c[:, blk - 1:blk]
    return jnp.concatenate(outs, axis=1)


def _route_kernel(aff_ref, idx_ref, val_ref, pos_s, *, n, cap):
    aff = aff_ref[...]
    n_rows = aff.shape[0]
    thr = jnp.zeros((n_rows, 1), i32)
    for bit in range(30, -1, -1):
        cand = thr | (1 << bit)
        cnt = jnp.sum(jnp.where(aff >= lax.bitcast_convert_type(cand, f32), 1.0, 0.0), axis=1, keepdims=True)
        thr = jnp.where(cnt >= float(cap), cand, thr)
    thr_f = lax.bitcast_convert_type(thr, f32)
    gt = aff > thr_f
    eq = aff == thr_f
    need = float(cap) - jnp.sum(jnp.where(gt, 1.0, 0.0), axis=1, keepdims=True)
    sel = gt | (eq & (_cumsum_lanes(eq.astype(f32), n) <= need))
    pos_s[...] = jnp.where(sel, _cumsum_lanes(sel.astype(f32), n).astype(i32) - 1, -1)

    qw = min(ROUTE_QW, cap)
    shift = qw.bit_length() - 1
    tok = lax.broadcasted_iota(i32, (1, n), 1)
    tok_hi = (tok >> ROUTE_TOK_SHIFT).astype(f32)
    tok_lo = (tok & ((1 << ROUTE_TOK_SHIFT) - 1)).astype(f32)
    hi_iota = lax.broadcasted_iota(i32, (SUBLANES, n), 0)
    lo_iota = lax.broadcasted_iota(i32, (qw, n), 0)

    def body(e, carry):
        pos = pos_s[pl.ds(e, 1), :]
        a0 = aff_ref[pl.ds(e, 1), :]
        a_h = a0.astype(bf16).astype(f32)
        a_m = (a0 - a_h).astype(bf16).astype(f32)
        a_l = ((a0 - a_h) - a_m).astype(bf16).astype(f32)
        in_hi = (pos >> shift) == hi_iota
        lhs = jnp.concatenate([jnp.where(in_hi, r, 0.0) for r in (tok_hi, tok_lo, a_h, a_m, a_l)], axis=0)
        onehot = jnp.where((pos & (qw - 1)) == lo_iota, 1.0, 0.0).astype(bf16)
        res = lax.dot_general(lhs.astype(bf16), onehot, (((1,), (1,)), ((), ())), preferred_element_type=f32)
        r_hi, r_lo, v_h, v_m, v_l = (res[k * SUBLANES:(k + 1) * SUBLANES] for k in range(5))
        idx_ref[e] = (r_hi * float(1 << ROUTE_TOK_SHIFT) + r_lo).astype(i32)[:cap // qw]
        val_ref[e] = ((v_h + v_m) + v_l)[:cap // qw]
        return carry

    lax.fori_loop(0, n_rows, body, 0)


def _route(aff_t, n, cap, lane_block):
    b = aff_t.shape[0]
    qw = min(ROUTE_QW, cap)
    assert cap % qw == 0 and cap // qw <= SUBLANES and qw & (qw - 1) == 0
    rows = b * N_EXPERTS
    idx, val = pl.pallas_call(
        functools.partial(_route_kernel, n=n, cap=cap),
        out_shape=[jax.ShapeDtypeStruct((rows, cap // qw, qw), i32),
                   jax.ShapeDtypeStruct((rows, cap // qw, qw), f32)],
        grid=(1,),
        in_specs=[pl.BlockSpec((rows, n), lambda i: (0, lane_block))],
        out_specs=[pl.BlockSpec((rows, cap // qw, qw), lambda i: (0, 0, 0)),
                   pl.BlockSpec((rows, cap // qw, qw), lambda i: (0, 0, 0))],
        scratch_shapes=[pltpu.VMEM((rows, n), i32)],
        compiler_params=_cparams("arbitrary"), name="route",
    )(aff_t.reshape(rows, aff_t.shape[2]))
    return idx.reshape(b, N_EXPERTS, cap), val.reshape(b, N_EXPERTS, cap)


MOE_PAIR = 2
GATHER_GROUP = 4


def _gather_kernel(idx_ref, h_ref, tok_ref, tbuf, *, cap):
    sp = cap + SUBLANES
    for k in range(GATHER_GROUP):
        base = (pl.program_id(0) * N_EXPERTS + pl.program_id(1) * GATHER_GROUP + k) * cap
        for p in range(cap):
            i = idx_ref[base + p]
            tbuf[pl.ds(p, SUBLANES, stride=sp), :] = h_ref[pl.ds(pl.multiple_of(i * SUBLANES, SUBLANES), SUBLANES), :]
        tok_ref[k] = jnp.concatenate([tbuf[pl.ds(c * sp, cap), :] for c in range(D // LANES)], axis=1).astype(bf16)


def _gather(idx, h2t, cap):
    b, rows, _ = h2t.shape
    return pl.pallas_call(
        functools.partial(_gather_kernel, cap=cap),
        out_shape=jax.ShapeDtypeStruct((b, N_EXPERTS, cap, D), bf16),
        grid_spec=pltpu.PrefetchScalarGridSpec(
            num_scalar_prefetch=1, grid=(b, N_EXPERTS // GATHER_GROUP),
            in_specs=[pl.BlockSpec((None, rows, LANES), lambda i, e, idx: (i, 0, 0))],
            out_specs=pl.BlockSpec((None, GATHER_GROUP, cap, D), lambda i, e, idx: (i, e, 0, 0)),
            scratch_shapes=[pltpu.VMEM((SUBLANES * (cap + SUBLANES), LANES), f32)]),
        compiler_params=_cparams("arbitrary", "arbitrary"), name="moe_gather",
    )(idx, h2t)


def _cast3_kernel(a_ref, b_ref, c_ref, ao_ref, bo_ref, co_ref):
    ao_ref[...] = a_ref[...].astype(bf16)
    bo_ref[...] = b_ref[...].astype(bf16)
    co_ref[...] = c_ref[...].astype(bf16)


def _expert0_bf16(wg, wu, wd, layer):
    spec = pl.BlockSpec((None, None, D, EXPERT_FF), lambda i: (layer, 0, 0, 0))
    ospec = pl.BlockSpec((D, EXPERT_FF), lambda i: (0, 0))
    return pl.pallas_call(
        _cast3_kernel, out_shape=[jax.ShapeDtypeStruct((D, EXPERT_FF), bf16)] * 3, grid=(1,),
        in_specs=[spec] * 3, out_specs=[ospec] * 3, compiler_params=_cparams("arbitrary"), name="expert0_cast",
    )(wg, wu, wd)


def _ffn_kernel(tok_ref, g0_ref, u0_ref, d0_ref, gq_ref, uq_ref, dq_ref, o_ref, w_even, w_odd):
    e = pl.program_id(0)
    i = pl.program_id(1)

    @pl.when((e == 0) & (i == 0))
    def _():
        w_even[0] = g0_ref[...]
        w_even[1] = u0_ref[...]
        w_even[2] = d0_ref[...]

    rows = gq_ref.shape[0]
    r0 = pl.multiple_of(i * rows, rows)

    def step(use, fill):
        n_s, cap, _ = tok_ref.shape
        t = tok_ref[...].reshape(n_s * cap, D)
        g = jnp.dot(t, use[0], preferred_element_type=f32)
        for k, q_ref in enumerate((gq_ref, uq_ref, dq_ref)):
            fill[k, pl.ds(r0, rows), :] = q_ref[...].astype(bf16)
        u = jnp.dot(t, use[1], preferred_element_type=f32)
        a = (g * _sigmoid(g) * u).astype(bf16)
        o_ref[...] = jnp.dot(a, use[2], preferred_element_type=f32).astype(bf16).reshape(n_s, cap, D)

    @pl.when(e % 2 == 0)
    def _():
        step(w_even, w_odd)

    @pl.when(e % 2 == 1)
    def _():
        step(w_odd, w_even)


def _ffn(tok, wg, wu, wd, layer):
    b, _, cap, _ = tok.shape
    n_s = MOE_PAIR if b % MOE_PAIR == 0 else 1
    steps = b // n_s
    assert D == EXPERT_FF and D % steps == 0 and (D // steps) % BF16_ROWS == 0
    rows = D // steps
    w0 = _expert0_bf16(wg, wu, wd, layer)
    w0spec = pl.BlockSpec((D, EXPERT_FF), lambda e, i: (0, 0), pipeline_mode=pl.Buffered(1))
    qspec = pl.BlockSpec((None, None, rows, D), lambda e, i: (layer, jnp.minimum(e + 1, N_EXPERTS - 1), i, 0))
    return pl.pallas_call(
        _ffn_kernel,
        out_shape=jax.ShapeDtypeStruct((b, N_EXPERTS, cap, D), bf16),
        grid=(N_EXPERTS, steps),
        in_specs=[pl.BlockSpec((n_s, None, cap, D), lambda e, i: (i, e, 0, 0)),
                  w0spec, w0spec, w0spec, qspec, qspec, qspec],
        out_specs=pl.BlockSpec((n_s, None, cap, D), lambda e, i: (i, e, 0, 0)),
        scratch_shapes=[pltpu.VMEM((3, D, EXPERT_FF), bf16), pltpu.VMEM((3, D, EXPERT_FF), bf16)],
        compiler_params=_cparams("arbitrary", "arbitrary"), name="moe_ffn",
    )(tok, *w0, wg, wu, wd)


SCATTER_UNROLL = 8


def _scatter_kernel(idx_ref, val_ref, o_ref, acc_ref, tbuf, *, cap):
    e = pl.program_id(1)
    sp = cap + SUBLANES

    @pl.when(e == 0)
    def _():
        acc_ref[...] = jnp.zeros_like(acc_ref)

    for k in range(MOE_PAIR):
        base = (pl.program_id(0) * N_EXPERTS + e * MOE_PAIR + k) * cap
        for c in range(D // LANES):
            tbuf[pl.ds(c * sp, cap), :] = o_ref[k, :, c * LANES:(c + 1) * LANES].astype(f32)
        u = min(SCATTER_UNROLL, cap)
        for p0 in range(0, cap, u):
            pend = []
            for p in range(p0, p0 + u):
                r0 = pl.multiple_of(idx_ref[base + p] * SUBLANES, SUBLANES)
                pend.append((r0, acc_ref[pl.ds(r0, SUBLANES), :]
                             + tbuf[pl.ds(p, SUBLANES, stride=sp), :] * val_ref[base + p]))
            for r0, v in pend:
                acc_ref[pl.ds(r0, SUBLANES), :] = v


def _scatter(idx, val, o, rows):
    b, _, cap, _ = o.shape
    return pl.pallas_call(
        functools.partial(_scatter_kernel, cap=cap),
        out_shape=jax.ShapeDtypeStruct((b, rows, LANES), f32),
        grid_spec=pltpu.PrefetchScalarGridSpec(
            num_scalar_prefetch=2, grid=(b, N_EXPERTS // MOE_PAIR),
            in_specs=[pl.BlockSpec((None, MOE_PAIR, cap, D), lambda i, e, idx, val: (i, e, 0, 0))],
            out_specs=pl.BlockSpec((None, rows, LANES), lambda i, e, idx, val: (i, 0, 0)),
            scratch_shapes=[pltpu.VMEM((SUBLANES * (cap + SUBLANES), LANES), f32)]),
        compiler_params=_cparams("arbitrary", "arbitrary"), name="moe_scatter",
    )(idx, val, o)


def _moe(aff_t, h2t, wg, wu, wd, layer, n_lat, n_ctx):
    idx, val = _route(aff_t, n_lat, CAP_FACTOR * n_lat // N_EXPERTS, 0)
    if n_ctx:
        idx_c, val_c = _route(aff_t, n_ctx, CAP_FACTOR * n_ctx // N_EXPERTS, n_lat // n_ctx)
        idx = jnp.concatenate([idx, idx_c + n_lat], axis=-1)
        val = jnp.concatenate([val, val_c], axis=-1)
    cap = idx.shape[-1]
    idx = idx.reshape(-1)
    val = val.reshape(-1)
    tok = _gather(idx, h2t, cap)
    o = _ffn(tok, wg, wu, wd, layer)
    return _scatter(idx, val, o, h2t.shape[1])


def _final_kernel(x_ref, acc_ref, mod_ref, n_ref, o_ref):
    x = x_ref[...] + mod_ref[:, 5 * D:6 * D] * _tiles_to_rows(acc_ref, x_ref.shape[0])
    o_ref[...] = (x * lax.rsqrt(jnp.mean(x * x, axis=-1, keepdims=True) + EPS)) * n_ref[...]


def _final(xs, acc, modsel, norm, n_lat):
    b = xs.shape[0]
    tm = 2 * TM if n_lat % (2 * TM) == 0 else TM
    return pl.pallas_call(
        _final_kernel,
        out_shape=jax.ShapeDtypeStruct((b, n_lat, D), f32),
        grid=(b, n_lat // tm),
        in_specs=[pl.BlockSpec((None, tm, D), lambda i, t: (i, t, 0)),
                  pl.BlockSpec((None, tm * SUBLANES, LANES), lambda i, t: (i, t, 0)),
                  pl.BlockSpec((None, None, 1, N_MOD * D), lambda i, t: (i, 0, 0, 0)),
                  pl.BlockSpec((1, D), lambda i, t: (0, 0))],
        out_specs=pl.BlockSpec((None, tm, D), lambda i, t: (i, t, 0)),
        compiler_params=_cparams("arbitrary", "arbitrary"), name="final_norm",
    )(xs, acc, modsel, norm)


def _rope_tables(n_lat, n_ctx):
    pos = jnp.arange(n_lat, dtype=f32)
    rows = jnp.floor(pos / GRID_W)
    cols = pos - rows * GRID_W
    half = HEAD_DIM // 4
    inv = 1.0 / (ROPE_BASE ** (jnp.arange(0, 2 * half, 2, dtype=f32) / (2 * half)))
    lane = jnp.arange(LANES)
    d = lane % HEAD_DIM
    ang = jnp.where((d < 2 * half)[None, :], rows[:, None], cols[:, None]) * inv[d % half][None, :]
    first = ((d % (2 * half)) < half)[None, :]
    cos = jnp.cos(ang)
    sin = jnp.sin(ang)
    s1 = jnp.where(first, -sin, 0.0)
    s2 = jnp.where(first, 0.0, sin)
    pad = lambda t, v: jnp.concatenate([t, jnp.full((n_ctx, LANES), v, f32)], axis=0)
    return pad(cos, 1.0), pad(s1, 0.0), pad(s2, 0.0)


CAST_COLS = 256


def _cast_cols_kernel(w_ref, o_ref, *, first_half_block):
    scale = jnp.where(pl.program_id(0) >= first_half_block, 0.5, 1.0)
    o_ref[...] = (w_ref[...] * scale).astype(bf16)


def _cast_cols(w_stack, layer, col0, n_cols, first_half_col, blk=CAST_COLS):
    assert col0 % blk == 0 and n_cols % blk == 0 and (first_half_col - col0) % blk == 0 and blk % LANES == 0
    rows = w_stack.shape[1]
    return pl.pallas_call(
        functools.partial(_cast_cols_kernel, first_half_block=(first_half_col - col0) // blk),
        out_shape=jax.ShapeDtypeStruct((rows, n_cols), bf16), grid=(n_cols // blk,),
        in_specs=[pl.BlockSpec((None, rows, blk), lambda j: (layer, 0, j + col0 // blk))],
        out_specs=pl.BlockSpec((rows, blk), lambda j: (0, j)),
        compiler_params=_cparams("arbitrary"), name="cast_cols",
    )(w_stack)


def _in_weights_bf16(w_in, layer):
    return (_cast_cols(w_in, layer, 0, W_A_COLS, W_A_COLS, blk=W_A_COLS // 2),
            _cast_cols(w_in, layer, W_A_COLS, IN_TOTAL - W_A_COLS, IN_TOTAL - G_G))


def kernel(x, c, ctx, c_ctx, norm1, norm2, w_mod, b_mod, w_in, conv_a, w_conv_out, attn_sink, w_attn_out, rnn_conv_w,
           rnn_conv_b, rnn_w_a, rnn_b_a, rnn_w_x, rnn_b_x, rnn_lam, w_rnn_out, w_o, w_router, w_e_gate, w_e_up,
           w_e_down, final_norm):
    b, n_lat, d = x.shape
    n_ctx = ctx.shape[1]
    depth = w_mod.shape[0]
    assert d == D and n_ctx == TM and n_lat % TM == 0 and b < SUBLANES
    nt = n_lat + n_ctx
    ct = nt // TM - 1

    cstack = jnp.concatenate([c, c_ctx[None, :], jnp.zeros((SUBLANES - b - 1, D), f32)], axis=0)
    mod = _modulation(cstack, w_mod, b_mod)
    tabs = _rope_tables(n_lat, n_ctx)

    xs = x
    acc = None
    modsel_prev = None
    for l in range(depth):
        ctx_out = l < depth - 1
        ctx_sep = ctx if l == 0 else None
        modsel = jnp.stack([mod[l, :b], jnp.broadcast_to(mod[l, b], (b, N_MOD * D))], axis=1)[:, :, None, :]
        rnn_prm = []
        for dr in range(2):
            wg = (0.5 * jnp.concatenate([rnn_w_a[l, dr], rnn_w_x[l, dr]], axis=-1)).astype(bf16)
            rnn_prm.append((rnn_conv_w[l, dr], rnn_conv_b[l, dr][None, :], wg, rnn_b_a[l, dr][None, :],
                            rnn_b_x[l, dr][None, :], rnn_lam[l, dr][None, :]))
        w_a, w_b = _in_weights_bf16(w_in, l)
        outs = _proj_a(xs, ctx_sep, acc, modsel_prev, modsel, norm1[l][None, :], tabs, w_a, rnn_prm[0])
        if acc is not None:
            xs, outs = outs[0], outs[1:]
        za, zq, zkv, zrx, hn, hf = outs
        zry, zg, hb = _proj_b(hn, w_b, zrx, rnn_prm[1])
        att = _attention(attn_sink[l][None, :], zq, zkv, n_lat, (nt if ctx_out else n_lat) // QB)
        xs, h2t, aff_t = _merge(xs, ctx_sep, za, att, hf, hb, zry, zg, modsel, conv_a[l],
                                w_conv_out[l].astype(bf16),
                                w_attn_out[l].astype(bf16), (0.5 * w_rnn_out[l]).astype(bf16),
                                (0.5 * w_o[l]).astype(bf16),
                                norm2[l][None, :], w_router[l].T.astype(bf16), (ct + 1) if ctx_out else ct)
        acc = _moe(aff_t, h2t, w_e_gate, w_e_up, w_e_down, l, n_lat, n_ctx if ctx_out else 0)
        modsel_prev = modsel
    return _final(xs, acc, modsel_prev, final_norm[None, :], n_lat)
```
